```python
import math, functools
import jax, jax.numpy as jnp
from jax import lax
import numpy as np

D_MODEL = 1024
BATCH = 1
SEQ = 16384
DEPTH = 1
DEC_BATCH = 32
DEC_SEQ = 4
PAST_LEN = 16384
PAGE_SIZE = 128

MIX_WIDTH = D_MODEL
HEAD_DIM = 64
RWKV_WIDTH = MIX_WIDTH // 2
ATT_WIDTH = MIX_WIDTH - RWKV_WIDTH
N_RWKV_HEADS = RWKV_WIDTH // HEAD_DIM
N_ATT_HEADS = ATT_WIDTH // HEAD_DIM
DECAY_LORA = 64
AAA_LORA = 64
GATE_LORA = 128
RWKV_COLS = 3 * RWKV_WIDTH + DECAY_LORA + AAA_LORA + GATE_LORA
N_IDX_HEADS = 8
IDX_DIM = 64
TOPK_MAX = 256
Q_BLOCK = 128
ATT_COLS = 3 * ATT_WIDTH + N_IDX_HEADS * IDX_DIM + IDX_DIM + N_IDX_HEADS
IN_COLS = RWKV_COLS + ATT_COLS
N_BUCKETS = 32
MAX_DISTANCE = 128
N_EXPERTS = 32
TOP_K = 4
D_FF = D_MODEL
SWIGLU_LIMIT = 7.0
SWIGLU_ALPHA = 1.702
ROW_BLOCK = 128
RMS_EPS = 1e-6
GN_EPS = HEAD_DIM * 1e-5
POOL_EXTRA = 4

kernel_name = 'hymba_rwkv7_dsa_moe_adaln_step'


def rms_norm(x, g):
    xf = x.astype(jnp.float32)
    y = xf * lax.rsqrt(jnp.mean(xf * xf, axis=-1, keepdims=True) + RMS_EPS)
    return (y * g.astype(jnp.float32)).astype(x.dtype)


def split_cols(p, sizes):
    return jnp.split(p, np.cumsum(sizes)[:-1].tolist(), axis=-1)


def adaln(c, w_ada, b_ada):
    mod = jax.nn.silu(c) @ w_ada + b_ada
    return jnp.split(mod[:, None, :], 6, axis=-1)


def modulate(x, g, shift, scale):
    return rms_norm(x, g) * (1 + scale) + shift


def wkv_step(S, inp):
    r_t, w_t, k_t, v_t, kk_t, a_t = inp
    sa = jnp.einsum('bhij,bhj->bhi', S, -kk_t)
    S = (S * w_t[:, :, None, :] + sa[..., None] * (kk_t * a_t)[:, :, None, :]
         + v_t[..., None] * k_t[:, :, None, :])
    return S, jnp.einsum('bhij,bhj->bhi', S, r_t)


def rwkv_mixer(p, p_prev, s0, lw):
    B, T, _ = p.shape
    f32 = jnp.float32
    H, N = N_RWKV_HEADS, HEAD_DIM
    ps = p + (p_prev - p) * lw['mu_shift']
    r, k, v, xw, xa, xg = split_cols(ps, [RWKV_WIDTH] * 3 + [DECAY_LORA, AAA_LORA, GATE_LORA])
    log_w = -jax.nn.softplus(-(lw['w0'] + jnp.tanh(xw) @ lw['w_decay2']).astype(f32)) - 0.5
    decay = jnp.exp(-jnp.exp(log_w))
    a = jax.nn.sigmoid((lw['a0'] + xa @ lw['w_a2']).astype(f32))
    g = jax.nn.sigmoid(xg) @ lw['w_g2']
    hd = lambda t: t.astype(f32).reshape(B, T, H, N)
    r, k, v, decay, a = hd(r), hd(k), hd(v), hd(decay), hd(a)
    kk = k * lw['k_k'].reshape(H, N).astype(f32)
    kk = kk * lax.rsqrt(jnp.maximum(jnp.sum(kk * kk, axis=-1, keepdims=True), 1e-24))
    k = k * (1 + (a - 1) * lw['k_a'].reshape(H, N).astype(f32))
    xs = tuple(jnp.moveaxis(t, 1, 0) for t in (r, decay, k, v, kk, a))
    s_fin, ys = lax.scan(wkv_step, s0.astype(f32), xs)
    y = jnp.moveaxis(ys, 0, 1)
    mu = jnp.mean(y, axis=-1, keepdims=True)
    var = jnp.mean(jnp.square(y - mu), axis=-1, keepdims=True)
    y = ((y - mu) * lax.rsqrt(var + GN_EPS) * lw['lnx_g'].reshape(H, N).astype(f32)
         + lw['lnx_b'].reshape(H, N).astype(f32))
    y = y + jnp.sum(r * k * lw['r_k'].astype(f32), axis=-1, keepdims=True) * v
    out = y.reshape(B, T, RWKV_WIDTH).astype(p.dtype) * g
    return out, s_fin, p[:, -1]


def att_split(p):
    B, T = p.shape[:2]
    q, k, v, qi, ki, wi = split_cols(p, [ATT_WIDTH] * 3 + [N_IDX_HEADS * IDX_DIM, IDX_DIM, N_IDX_HEADS])
    hd = lambda t: t.reshape(B, T, N_ATT_HEADS, HEAD_DIM)
    return hd(q), hd(k), hd(v), qi.reshape(B, T, N_IDX_HEADS, IDX_DIM), ki, wi


def index_scores(qi, wi, ki, q_pos, k_pos):
    s = jnp.einsum('bqhd,bsd->bqhs', qi, ki).astype(jnp.float32) * IDX_DIM ** -0.5
    I = jnp.einsum('bqhs,bqh->bqs', jax.nn.relu(s), wi.astype(jnp.float32) * N_IDX_HEADS ** -0.5)
    causal = k_pos[None, :] <= q_pos[:, None]
    return jnp.where(causal[None], I, -jnp.inf)


def t5_bucket(dist):
    n = jnp.maximum(dist, 0)
    max_exact = N_BUCKETS // 2
    nf = jnp.maximum(n, 1).astype(jnp.float32)
    large = max_exact + (jnp.log(nf / max_exact) / math.log(MAX_DISTANCE / max_exact)
                         * (N_BUCKETS - max_exact)).astype(jnp.int32)
    large = jnp.minimum(large, N_BUCKETS - 1)
    return jnp.where(n < max_exact, n, large)


def sparse_attend(q, k_sel, v_sel, sel_pos, q_pos, rel_bias):
    logits = jnp.einsum('bqhd,bqkhd->bqhk', q, k_sel).astype(jnp.float32) * HEAD_DIM ** -0.5
    dist = q_pos[None, :, None] - sel_pos
    bias = jnp.swapaxes(rel_bias[t5_bucket(dist)], -1, -2).astype(jnp.float32)
    logits = jnp.where((dist >= 0)[:, :, None, :], logits + bias, -jnp.inf)
    probs = jax.nn.softmax(logits, axis=-1).astype(v_sel.dtype)
    return jnp.einsum('bqhk,bqkhd->bqhd', probs, v_sel)


def prompt_attention(q, k, v, qi, ki, wi, rel_bias):
    B, S = q.shape[:2]
    topk = min(TOPK_MAX, S // 4)
    key_pos = jnp.arange(S)
    take = jax.vmap(lambda rows, idx: rows[idx])

    def block(i):
        start = i * Q_BLOCK
        sl = lambda t: lax.dynamic_slice_in_dim(t, start, Q_BLOCK, axis=1)
        q_pos = start + jnp.arange(Q_BLOCK)
        scores = index_scores(sl(qi), sl(wi), ki, q_pos, key_pos)
        _, sel = lax.top_k(scores, topk)
        return sparse_attend(sl(q), take(k, sel), take(v, sel), sel, q_pos, rel_bias)

    out = lax.map(block, jnp.arange(S // Q_BLOCK))
    return jnp.moveaxis(out, 0, 1).reshape(B, S, ATT_WIDTH)


def sample_attention(q, k_new, v_new, qi, ki_new, wi, ck, cv, ckidx, page_table, rel_bias):
    B, T = q.shape[:2]
    past = page_table.shape[1] * PAGE_SIZE
    L = past + T
    topk = min(TOPK_MAX, L // 4)
    ki_past = ckidx[page_table].reshape(B, past, IDX_DIM)
    ki_all = jnp.concatenate([ki_past.astype(ki_new.dtype), ki_new], axis=1)
    q_pos = past + jnp.arange(T)
    scores = index_scores(qi, wi, ki_all, q_pos, jnp.arange(L))
    _, sel = lax.top_k(scores, topk)
    in_past = (sel < past)[..., None, None]
    sp = jnp.minimum(sel, past - 1)
    phys = jax.vmap(lambda pt, s: pt[s // PAGE_SIZE])(page_table, sp)
    off = sp % PAGE_SIZE
    sn = jnp.clip(sel - past, 0, T - 1)
    take = jax.vmap(lambda rows, idx: rows[idx])
    k_sel = jnp.where(in_past, ck[phys, off].astype(k_new.dtype), take(k_new, sn))
    v_sel = jnp.where(in_past, cv[phys, off].astype(v_new.dtype), take(v_new, sn))
    return sparse_attend(q, k_sel, v_sel, sel, q_pos, rel_bias).reshape(B, T, ATT_WIDTH)


def prompt_mixer(p_rwkv, p_att, lw, rel_bias):
    B = p_rwkv.shape[0]
    prev = jnp.concatenate([jnp.zeros_like(p_rwkv[:, :1]), p_rwkv[:, :-1]], axis=1)
    s0 = jnp.zeros((B, N_RWKV_HEADS, HEAD_DIM, HEAD_DIM), jnp.float32)
    r_out, s_fin, last = rwkv_mixer(p_rwkv, prev, s0, lw)
    q, k, v, qi, ki, wi = att_split(p_att)
    a_out = prompt_attention(q, k, v, qi, ki, wi, rel_bias)
    return r_out, a_out, (k, v, ki, s_fin, last)


def sample_mixer(p_rwkv, p_att, lw, rel_bias, ck, cv, ckidx, page_table, s_wkv, s_shift):
    prev = jnp.concatenate([s_shift[:, None].astype(p_rwkv.dtype), p_rwkv[:, :-1]], axis=1)
    r_out, s_fin, last = rwkv_mixer(p_rwkv, prev, s_wkv, lw)
    q, k, v, qi, ki, wi = att_split(p_att)
    a_out = sample_attention(q, k, v, qi, ki, wi, ck, cv, ckidx, page_table, rel_bias)
    return r_out, a_out, (k, v, ki, s_fin, last)


def moe_ffn(h, lw):
    B, T, D = h.shape
    n = B * T
    xt = h.reshape(n, D)
    logits = (xt @ lw['w_router'] + lw['b_router']).astype(jnp.float32)
    top_val, top_idx = lax.top_k(logits, TOP_K)
    gates = jax.nn.softmax(top_val, axis=-1)
    flat_e = top_idx.reshape(-1)
    order = jnp.argsort(flat_e)
    sorted_e = flat_e[order]
    counts = jnp.bincount(flat_e, length=N_EXPERTS)
    padded = (counts + ROW_BLOCK - 1) // ROW_BLOCK * ROW_BLOCK
    pad_end = jnp.cumsum(padded)
    pad_start = pad_end - padded
    start = jnp.cumsum(counts) - counts
    slot = pad_start[sorted_e] + jnp.arange(n * TOP_K) - start[sorted_e]
    n_blocks = -(-(n * TOP_K) // ROW_BLOCK) + N_EXPERTS
    rows = jnp.zeros((n_blocks * ROW_BLOCK, D), h.dtype).at[slot].set(xt[order // TOP_K])
    blk_e = jnp.minimum(jnp.searchsorted(pad_end, jnp.arange(n_blocks) * ROW_BLOCK, side='right'),
                        N_EXPERTS - 1)

    def expert_block(args):
        xb, e = args
        gu = xb @ lw['w_gu'][e] + lw['b_gu'][e]
        glu, lin = jnp.split(gu, 2, axis=-1)
        glu = jnp.minimum(glu, SWIGLU_LIMIT)
        lin = jnp.clip(lin, -SWIGLU_LIMIT, SWIGLU_LIMIT)
        act = glu * jax.nn.sigmoid(SWIGLU_ALPHA * glu) * (lin + 1)
        return act @ lw['w_down'][e] + lw['b_down'][e]

    out_rows = lax.map(expert_block, (rows.reshape(n_blocks, ROW_BLOCK, D), blk_e)).reshape(-1, D)
    per_assign = jnp.zeros((n * TOP_K, D), h.dtype).at[order].set(out_rows[slot])
    y = jnp.einsum('tkd,tk->td', per_assign.reshape(n, TOP_K, D), gates.astype(h.dtype))
    return y.reshape(B, T, D)


def layer(x, c, lw, mixer):
    sh1, sc1, g1, sh2, sc2, g2 = adaln(c, lw['w_ada'], lw['b_ada'])
    p = modulate(x, lw['norm1_g'], sh1, sc1) @ lw['w_in']
    rwkv_out, att_out, new_state = mixer(p[..., :RWKV_COLS], p[..., RWKV_COLS:], lw)
    x = x + g1 * (jnp.concatenate([rwkv_out, att_out], axis=-1) @ lw['w_out'])
    x = x + g2 * moe_ffn(modulate(x, lw['norm2_g'], sh2, sc2), lw)
    return x, new_state


def setup_inputs(seed: int = 0) -> dict:
    key = jax.random.key(seed)
    ks = list(jax.random.split(key, 48))
    nrm = lambda shape, scale=1.0: jax.random.normal(ks.pop(), shape, jnp.float32) * scale
    L, D, E, F = DEPTH, D_MODEL, N_EXPERTS, D_FF
    n_pages = PAST_LEN // PAGE_SIZE
    n_used = DEC_BATCH * n_pages
    n_pool = n_used + n_used // POOL_EXTRA
    perm = jax.random.permutation(ks.pop(), n_pool)
    page_table = perm[:n_used].reshape(DEC_BATCH, n_pages).astype(jnp.int32)
    return {
        'x_prompt': nrm((BATCH, SEQ, D)),
        'x_sample': nrm((DEC_BATCH, DEC_SEQ, D)),
        'c_prompt': nrm((BATCH, D)),
        'c_sample': nrm((DEC_BATCH, D)),
        'cache_k': nrm((L, n_pool, PAGE_SIZE, N_ATT_HEADS, HEAD_DIM)),
        'cache_v': nrm((L, n_pool, PAGE_SIZE, N_ATT_HEADS, HEAD_DIM)),
        'cache_kidx': nrm((L, n_pool, PAGE_SIZE, IDX_DIM)),
        'page_table': page_table,
        'state_wkv': nrm((L, DEC_BATCH, N_RWKV_HEADS, HEAD_DIM, HEAD_DIM), 0.5),
        'state_shift': nrm((L, DEC_BATCH, RWKV_COLS)),
        'w_ada': nrm((L, D, 6 * D), 0.2 * D ** -0.5),
        'b_ada': nrm((L, 6 * D), 0.01),
        'norm1_g': 1.0 + nrm((L, D), 0.02),
        'w_in': nrm((L, D, IN_COLS), D ** -0.5),
        'mu_shift': jax.random.uniform(ks.pop(), (L, RWKV_COLS), jnp.float32),
        'w0': -2.0 + nrm((L, RWKV_WIDTH), 0.5),
        'w_decay2': nrm((L, DECAY_LORA, RWKV_WIDTH), 0.1 * DECAY_LORA ** -0.5),
        'a0': nrm((L, RWKV_WIDTH), 0.1),
        'w_a2': nrm((L, AAA_LORA, RWKV_WIDTH), 0.1 * AAA_LORA ** -0.5),
        'w_g2': nrm((L, GATE_LORA, RWKV_WIDTH), GATE_LORA ** -0.5),
        'k_k': 1.0 + nrm((L, RWKV_WIDTH), 0.1),
        'k_a': 1.0 + nrm((L, RWKV_WIDTH), 0.1),
        'r_k': nrm((L, N_RWKV_HEADS, HEAD_DIM), 0.1),
        'lnx_g': 1.0 + nrm((L, RWKV_WIDTH), 0.02),
        'lnx_b': nrm((L, RWKV_WIDTH), 0.01),
        'rel_bias': nrm((N_BUCKETS, N_ATT_HEADS), 0.5),
        'w_out': nrm((L, MIX_WIDTH, D), MIX_WIDTH ** -0.5),
        'norm2_g': 1.0 + nrm((L, D), 0.02),
        'w_router': nrm((L, D, E), D ** -0.5),
        'b_router': nrm((L, E), 0.01),
        'w_gu': nrm((L, E, D, 2 * F), D ** -0.5),
        'b_gu': nrm((L, E, 2 * F), 0.01),
        'w_down': nrm((L, E, F, D), F ** -0.5),
        'b_down': nrm((L, E, D), 0.01),
        'normf_g': 1.0 + nrm((D,), 0.02),
    }


def reference(x_prompt, x_sample, c_prompt, c_sample, cache_k, cache_v, cache_kidx, page_table,
              state_wkv, state_shift, w_ada, b_ada, norm1_g, w_in, mu_shift, w0, w_decay2, a0,
              w_a2, w_g2, k_k, k_a, r_k, lnx_g, lnx_b, rel_bias, w_out, norm2_g, w_router,
              b_router, w_gu, b_gu, w_down, b_down, normf_g):
    xp, xs = x_prompt, x_sample
    prompt_states, sample_states = [], []
    for l in range(DEPTH):
        lw = {'w_ada': w_ada[l], 'b_ada': b_ada[l], 'norm1_g': norm1_g[l], 'w_in': w_in[l],
              'mu_shift': mu_shift[l], 'w0': w0[l], 'w_decay2': w_decay2[l], 'a0': a0[l],
              'w_a2': w_a2[l], 'w_g2': w_g2[l], 'k_k': k_k[l], 'k_a': k_a[l], 'r_k': r_k[l],
              'lnx_g': lnx_g[l], 'lnx_b': lnx_b[l], 'w_out': w_out[l], 'norm2_g': norm2_g[l],
              'w_router': w_router[l], 'b_router': b_router[l], 'w_gu': w_gu[l],
              'b_gu': b_gu[l], 'w_down': w_down[l], 'b_down': b_down[l]}
        xp, st_p = layer(xp, c_prompt, lw, functools.partial(prompt_mixer, rel_bias=rel_bias))
        xs, st_s = layer(xs, c_sample, lw, functools.partial(
            sample_mixer, rel_bias=rel_bias, ck=cache_k[l], cv=cache_v[l], ckidx=cache_kidx[l],
            page_table=page_table, s_wkv=state_wkv[l], s_shift=state_shift[l]))
        prompt_states.append(st_p)
        sample_states.append(st_s)
    stack = lambda states, i: jnp.stack([s[i] for s in states])
    y_prompt = rms_norm(xp, normf_g)
    y_sample = rms_norm(xs, normf_g)
    k_prompt = stack(prompt_states, 0)
    v_prompt = stack(prompt_states, 1)
    kidx_prompt = stack(prompt_states, 2)
    wkv_prompt = stack(prompt_states, 3)
    shift_prompt = stack(prompt_states, 4)
    k_sample = stack(sample_states, 0)
    v_sample = stack(sample_states, 1)
    kidx_sample = stack(sample_states, 2)
    wkv_sample = stack(sample_states, 3)
    shift_sample = stack(sample_states, 4)
    return (y_prompt, y_sample, k_prompt, v_prompt, kidx_prompt, wkv_prompt, shift_prompt,
            k_sample, v_sample, kidx_sample, wkv_sample, shift_sample)
```

```python
import functools
import math

import jax
import jax.numpy as jnp
import numpy as np
from jax import lax
from jax.experimental import pallas as pl
from jax.experimental.pallas import tpu as pltpu

F32 = jnp.float32
BF16 = jnp.bfloat16
I32 = jnp.int32

HEAD_DIM = 64
N_HEADS = 8
GROUP_WIDTH = N_HEADS * HEAD_DIM
DECAY_LORA, AAA_LORA, GATE_LORA = 64, 64, 128
RWKV_COLS = 3 * GROUP_WIDTH + DECAY_LORA + AAA_LORA + GATE_LORA
IDX_DIM = 64
N_IDX_HEADS = 8
TOPK_MAX = 256
N_BUCKETS = 32
MAX_DISTANCE = 128
N_EXPERTS = 32
TOP_K = 4
SWIGLU_LIMIT = 7.0
SWIGLU_ALPHA = 1.702
RMS_EPS = 1e-6
GN_EPS = HEAD_DIM * 1e-5
PAGE_SIZE = 128

LANES = 128
SUBLANES = 8
VMEM_LIMIT = 56 * 1024 * 1024

NEG_BIG = -1e30
KEY_NEG_INF = -2139095041


def _cparams(sem):
    return pltpu.CompilerParams(dimension_semantics=sem, vmem_limit_bytes=VMEM_LIMIT)


def _dot(a, b):
    return jnp.dot(a, b, preferred_element_type=F32)


def _dot_nt(a, b):
    return lax.dot_general(a, b, (((1,), (1,)), ((), ())), preferred_element_type=F32)


def _split2(a):
    hi = a.astype(BF16)
    lo = (a - hi.astype(F32)).astype(BF16)
    return hi, lo


def _split3(a):
    hi = a.astype(BF16)
    r1 = a - hi.astype(F32)
    mid = r1.astype(BF16)
    lo = (r1 - mid.astype(F32)).astype(BF16)
    return hi, mid, lo


def _rep(x, n, axis):
    return jnp.concatenate([x] * n, axis=axis)


def _float_key(x):
    b = pltpu.bitcast(x, I32)
    return b ^ ((b >> 31) & 0x7FFFFFFF)


def _ada_kernel(c_ref, w_ref, b_ref, o_ref):
    c = c_ref[...]
    s = c * jax.nn.sigmoid(c)
    o_ref[...] = _dot(s.astype(BF16), w_ref[...].astype(BF16)) + b_ref[...]


def _adaln(c, w_ada, b_ada):
    r, d = c.shape
    n = w_ada.shape[1]
    tn = 1536
    return pl.pallas_call(
        _ada_kernel,
        grid=(n // tn,),
        in_specs=[pl.BlockSpec((r, d), lambda j: (0, 0)),
                  pl.BlockSpec((d, tn), lambda j: (0, j)),
                  pl.BlockSpec((1, tn), lambda j: (0, j))],
        out_specs=pl.BlockSpec((r, tn), lambda j: (0, j)),
        out_shape=jax.ShapeDtypeStruct((r, n), F32),
        compiler_params=_cparams(("arbitrary",)),
    )(c, w_ada, b_ada.reshape(1, n))


def _inproj_kernel(x_ref, g_ref, sh_ref, sc_ref, wr_ref, wa_ref, wk_ref,
                   pr_ref, k_ref, v_ref, ki_ref, wi_ref, qs_ref, kb_ref, vb_ref, qib_ref, kib_ref):
    x = x_ref[...]
    y = x * lax.rsqrt(jnp.mean(x * x, axis=-1, keepdims=True) + RMS_EPS) * g_ref[...]
    h = (y * (1.0 + sc_ref[...]) + sh_ref[...]).astype(BF16)
    pr_ref[...] = _dot(h, wr_ref[...])
    a = _dot(h, wa_ref[...])
    w = GROUP_WIDTH
    q, k, v, qi = a[:, :w], a[:, w:2 * w], a[:, 2 * w:3 * w], a[:, 3 * w:4 * w]
    k_ref[...] = k
    v_ref[...] = v
    qs_ref[...] = (q * HEAD_DIM ** -0.5).astype(BF16)
    kb_ref[...] = k.astype(BF16)
    vb_ref[...] = v.astype(BF16)
    qib_ref[...] = (qi * IDX_DIM ** -0.5).astype(BF16)
    kw = _dot(h, wk_ref[...])
    ki = kw[:, :IDX_DIM]
    ki_ref[...] = ki
    kib_ref[...] = ki.astype(BF16)
    wi_ref[...] = kw[:, IDX_DIM:IDX_DIM + N_IDX_HEADS] * N_IDX_HEADS ** -0.5


def _in_proj(x, norm_g, shift, scale, w_in):
    t, d = x.shape
    tm = min(512, t)
    per_row = shift.shape[0] != 1
    w = GROUP_WIDTH
    a0 = RWKV_COLS
    wr = w_in[:, :a0].astype(BF16)
    wa = w_in[:, a0:a0 + 4 * w].astype(BF16)
    wk = jnp.pad(w_in[:, a0 + 4 * w:], ((0, 0), (0, LANES - IDX_DIM - N_IDX_HEADS))).astype(BF16)
    mod_spec = pl.BlockSpec((tm, d), lambda i: (i, 0)) if per_row else pl.BlockSpec((1, d), lambda i: (0, 0))
    row = lambda n: pl.BlockSpec((tm, n), lambda i: (i, 0))
    full = lambda a: pl.BlockSpec(a.shape, lambda i: (0, 0))
    sds = lambda n, dt: jax.ShapeDtypeStruct((t, n), dt)
    return pl.pallas_call(
        _inproj_kernel,
        grid=(t // tm,),
        in_specs=[row(d), pl.BlockSpec((1, d), lambda i: (0, 0)), mod_spec, mod_spec, full(wr), full(wa), full(wk)],
        out_specs=[row(a0), row(w), row(w), row(IDX_DIM), row(N_IDX_HEADS), row(w), row(w), row(w), row(w),
                   row(IDX_DIM)],
        out_shape=[sds(a0, F32), sds(w, F32), sds(w, F32), sds(IDX_DIM, F32), sds(N_IDX_HEADS, F32),
                   sds(w, BF16), sds(w, BF16), sds(w, BF16), sds(w, BF16), sds(IDX_DIM, BF16)],
        compiler_params=_cparams(("arbitrary",)),
    )(x, norm_g.reshape(1, d), shift, scale, wr, wa, wk)


def _bias_band_kernel(rb_ref, o_ref, *, off):
    _, rows, cols = o_ref.shape
    r = lax.broadcasted_iota(I32, (rows, cols), 0)
    c = lax.broadcasted_iota(I32, (rows, cols), 1)
    n = jnp.maximum(r + off - c, 0)
    max_exact = N_BUCKETS // 2
    nf = jnp.maximum(n, 1).astype(F32)
    large = max_exact + (jnp.log(nf / max_exact) / math.log(MAX_DISTANCE / max_exact)
                         * (N_BUCKETS - max_exact)).astype(I32)
    large = jnp.minimum(large, N_BUCKETS - 1)
    bucket = jnp.where(n < max_exact, n, large)
    for h in range(N_HEADS):
        far = rb_ref[N_BUCKETS - 1, h]
        acc = jnp.zeros((rows, cols), F32)
        for b in range(N_BUCKETS - 1):
            acc = jnp.where(bucket == b, rb_ref[b, h] - far, acc)
        o_ref[h] = acc


def _bias_band(rel_bias, rows, cols, off):
    return pl.pallas_call(
        functools.partial(_bias_band_kernel, off=off),
        in_specs=[pl.BlockSpec(memory_space=pltpu.SMEM)],
        out_specs=pl.BlockSpec((N_HEADS, rows, cols), lambda: (0, 0, 0)),
        out_shape=jax.ShapeDtypeStruct((N_HEADS, rows, cols), F32),
    )(rel_bias)


def _kth_largest_key(count_ge, rows, topk):
    lo0 = jnp.full((rows, LANES), KEY_NEG_INF, I32)
    hi0 = jnp.full((rows, LANES), 0x7F800000, I32)

    def body(_, carry):
        lo, hi = carry
        mid = (lo | hi) - ((lo ^ hi) >> 1)
        ge = count_ge(mid) >= topk
        return jnp.where(ge, mid, lo), jnp.where(ge, hi, mid - 1)

    lo, _ = lax.fori_loop(0, 32, body, (lo0, hi0))
    return lo


PQB = 256
PKT = 512
PKB = 1024
PSUB = 256


def _prompt_att_kernel(qs_ref, qib_ref, wi_ref, kib_ref, kb_ref, vb_ref, band_ref, o_ref,
                       keys_ref, tau_ref, wrep_ref, m_ref, l_ref, acc_ref, *, topk, qb_rows):
    qb = pl.program_id(0)
    j = pl.program_id(1)
    q_lo = qb * qb_rows
    n_kt = (q_lo + qb_rows + PKT - 1) // PKT

    @pl.when(j == 0)
    def _index_phase():
        w = wi_ref[...]
        for h in range(N_IDX_HEADS):
            wrep_ref[h] = jnp.broadcast_to(w[:, h:h + 1], (qb_rows, LANES))
        row_pos = q_lo + lax.broadcasted_iota(I32, (qb_rows, PKT), 0)

        def tile_body(kt, _):
            k0 = pl.multiple_of(kt * PKT, PKT)
            ki = kib_ref[pl.ds(k0, PKT), :]
            acc = jnp.zeros((qb_rows, PKT), F32)
            for h in range(N_IDX_HEADS):
                s = _dot_nt(qib_ref[:, h * IDX_DIM:(h + 1) * IDX_DIM], ki)
                acc = acc + jnp.maximum(s, 0.0) * _rep(wrep_ref[h], PKT // LANES, axis=1)
            col_pos = k0 + lax.broadcasted_iota(I32, (qb_rows, PKT), 1)
            keys_ref[kt] = jnp.where(col_pos <= row_pos, _float_key(acc), KEY_NEG_INF)
            return 0

        lax.fori_loop(0, n_kt, tile_body, 0)

        def count_ge(mid):
            mid_t = _rep(mid, PKT // LANES, axis=1)

            def cbody(kt, c):
                return c + jnp.where(keys_ref[kt] >= mid_t, 1.0, 0.0)

            c = lax.fori_loop(0, n_kt, cbody, jnp.zeros((qb_rows, PKT), F32))
            tot = jnp.sum(c, axis=1, keepdims=True)
            return jnp.broadcast_to(tot, (qb_rows, LANES))

        tau = _kth_largest_key(count_ge, qb_rows, topk)
        tau_ref[...] = jnp.maximum(tau, KEY_NEG_INF + 1)
        m_ref[...] = jnp.full(m_ref.shape, NEG_BIG, F32)
        l_ref[...] = jnp.zeros(l_ref.shape, F32)
        acc_ref[...] = jnp.zeros(acc_ref.shape, F32)

    kb = j - 1
    last_kb = (q_lo + qb_rows - 1) // PKB

    @pl.when(jnp.logical_and(j >= 1, kb <= last_kb))
    def _attend_phase():
        tau_t = _rep(tau_ref[...], PSUB // LANES, axis=1)
        for u in range(PKB // PSUB):
            s0 = kb * PKB + u * PSUB

            def sub_tile(near, u=u, s0=s0):
                kt = (kb * PKB + u * PSUB) // PKT
                c0 = (u * PSUB) % PKT
                sel = keys_ref[kt, :, c0:c0 + PSUB] >= tau_t
                if near:
                    which = jnp.clip((q_lo - s0) // PSUB, 0, 1)
                for h in range(N_HEADS):
                    hs = slice(h * HEAD_DIM, (h + 1) * HEAD_DIM)
                    s = _dot_nt(qs_ref[:, hs], kb_ref[u * PSUB:(u + 1) * PSUB, hs])
                    if near:
                        s = s + band_ref[1 - which, h]
                    s = jnp.where(sel, s, NEG_BIG)
                    m_old = m_ref[h]
                    m_new = jnp.maximum(m_old, jnp.broadcast_to(jnp.max(s, axis=1, keepdims=True),
                                                                (qb_rows, LANES)))
                    p = jnp.exp(s - _rep(m_new, PSUB // LANES, axis=1))
                    alpha = jnp.exp(m_old - m_new)
                    l_ref[h] = alpha * l_ref[h] + jnp.broadcast_to(jnp.sum(p, axis=1, keepdims=True),
                                                                   (qb_rows, LANES))
                    m_ref[h] = m_new
                    pv = _dot(p.astype(BF16), vb_ref[u * PSUB:(u + 1) * PSUB, hs])
                    acc_ref[:, hs] = acc_ref[:, hs] * alpha[:, :HEAD_DIM] + pv

            is_near = s0 + PSUB > q_lo - PSUB
            in_range = s0 < q_lo + qb_rows

            @pl.when(jnp.logical_and(in_range, is_near))
            def _():
                sub_tile(True)

            @pl.when(jnp.logical_and(in_range, jnp.logical_not(is_near)))
            def _():
                sub_tile(False)

    @pl.when(j == last_kb + 1)
    def _finish():
        for h in range(N_HEADS):
            hs = slice(h * HEAD_DIM, (h + 1) * HEAD_DIM)
            o_ref[:, hs] = acc_ref[:, hs] / l_ref[h][:, :HEAD_DIM]


def _prompt_attention(qs, qib, wi, kib, kb, vb, rel_bias):
    s, w = qs.shape
    topk = min(TOPK_MAX, s // 4)
    qb_rows = min(PQB, s)
    assert s % qb_rows == 0 and s % PKB == 0 and qb_rows == PSUB
    nqb = s // qb_rows
    nkb = s // PKB
    band = _bias_band(rel_bias, qb_rows, 2 * PSUB, PSUB)
    band = jnp.stack([band[:, :, :PSUB], band[:, :, PSUB:]])

    def kv_map(i, j):
        return (jnp.minimum(jnp.maximum(j - 1, 0), (i * qb_rows + qb_rows - 1) // PKB), 0)

    qrow = lambda n: pl.BlockSpec((qb_rows, n), lambda i, j: (i, 0))
    return pl.pallas_call(
        functools.partial(_prompt_att_kernel, topk=topk, qb_rows=qb_rows),
        grid=(nqb, nkb + 1),
        in_specs=[qrow(w), qrow(w), qrow(N_IDX_HEADS),
                  pl.BlockSpec((s, IDX_DIM), lambda i, j: (0, 0)),
                  pl.BlockSpec((PKB, w), kv_map), pl.BlockSpec((PKB, w), kv_map),
                  pl.BlockSpec(band.shape, lambda i, j: (0, 0, 0, 0))],
        out_specs=qrow(w),
        out_shape=jax.ShapeDtypeStruct((s, w), F32),
        scratch_shapes=[pltpu.VMEM((s // PKT, qb_rows, PKT), I32),
                        pltpu.VMEM((qb_rows, LANES), I32),
                        pltpu.VMEM((N_IDX_HEADS, qb_rows, LANES), F32),
                        pltpu.VMEM((N_HEADS, qb_rows, LANES), F32),
                        pltpu.VMEM((N_HEADS, qb_rows, LANES), F32),
                        pltpu.VMEM((qb_rows, w), F32)],
        compiler_params=_cparams(("arbitrary", "arbitrary")),
    )(qs, qib, wi, kib, kb, vb, band)


RC = 16
RNB = 16
RBLK = RC * RNB


def _dot3(a, b):
    ah, al = _split2(a)
    bh, bl = _split2(b)
    return _dot(ah, bh) + (_dot(ah, bl) + _dot(al, bh))


def _dot3_nt(a, b):
    ah, al = _split2(a)
    bh, bl = _split2(b)
    return _dot_nt(ah, bh) + (_dot_nt(ah, bl) + _dot_nt(al, bh))


def _dot3_tn(a, b):
    dn = (((0,), (0,)), ((), ()))
    ah, al = _split2(a)
    bh, bl = _split2(b)
    f = lambda x, y: lax.dot_general(x, y, dn, preferred_element_type=F32)
    return f(ah, bh) + (f(ah, bl) + f(al, bh))


def _dot_exact_lhs(m_bf, x):
    hi, mid, lo = _split3(x)
    return _dot(m_bf, hi) + (_dot(m_bf, mid) + _dot(m_bf, lo))


def _dot_exact_rhs(x, m_bf):
    hi, mid, lo = _split3(x)
    return _dot(hi, m_bf) + (_dot(mid, m_bf) + _dot(lo, m_bf))


def _rwkv_kernel(*refs, chain, t_valid):
    if chain:
        p_ref, = refs[:1]
        rest = refs[1:]
    else:
        p_ref, pprev_ref, s0_ref = refs[:3]
        rest = refs[3:]
    (mu_ref, w0_ref, wd2_ref, a0_ref, wa2_ref, wg2_ref, kk_ref, ka_ref, rk_ref, lng_ref, lnb_ref,
     lt_ref, bo_ref, bd_ref, out_ref, sfin_ref,
     carry_ref, s_ref, wa_s, rq_s, uv_s, yv_s, bt_s, kt_s, v_s, gc_s, y_s) = rest
    i = pl.program_id(0)
    w = GROUP_WIDTH
    p = p_ref[...]
    row = lax.broadcasted_iota(I32, (RBLK, 1), 0)
    if chain:
        @pl.when(i == 0)
        def _():
            carry_ref[...] = jnp.zeros(carry_ref.shape, F32)
            s_ref[...] = jnp.zeros(s_ref.shape, F32)

        pprev = jnp.where(row == 0, carry_ref[...], pltpu.roll(p, 1, axis=0))
        carry_ref[...] = p[RBLK - 1:RBLK, :]
    else:
        pprev = pprev_ref[...]
    ps = p + (pprev - p) * mu_ref[...]
    r, k, v = ps[:, :w], ps[:, w:2 * w], ps[:, 2 * w:3 * w]
    o = 3 * w
    xw = ps[:, o:o + DECAY_LORA]
    xa = ps[:, o + DECAY_LORA:o + DECAY_LORA + AAA_LORA]
    xg = ps[:, o + DECAY_LORA + AAA_LORA:]
    dec = w0_ref[...] + _dot(jnp.tanh(xw).astype(BF16), wd2_ref[...])
    softplus = jnp.maximum(-dec, 0.0) + jnp.log(1.0 + jnp.exp(-jnp.abs(dec)))
    lw = -jnp.exp(-softplus - 0.5)
    a = jax.nn.sigmoid(a0_ref[...] + _dot(xa.astype(BF16), wa2_ref[...]))
    g = _dot(jax.nn.sigmoid(xg).astype(BF16), wg2_ref[...])
    kk = k * kk_ref[...]
    kk = kk * lax.rsqrt(jnp.maximum(_dot_exact_rhs(kk * kk, bd_ref[...]), 1e-24))
    k2 = k * (1.0 + (a - 1.0) * ka_ref[...])
    alpha = -kk
    beta = kk * a
    if t_valid < RC:
        valid = (row % RC) < t_valid
        zero = lambda x: jnp.where(valid, x, 0.0)
        lw, alpha, beta, k2, r, v = zero(lw), zero(alpha), zero(beta), zero(k2), zero(r), zero(v)
    cl = _dot_exact_lhs(lt_ref[...], lw)
    ct = _dot_exact_lhs(bo_ref[...], lw)
    g_in = jnp.exp(cl)
    g_ex = jnp.exp(cl - lw)
    g_inv = jnp.exp(-cl)
    g_end = jnp.exp(ct - cl)
    g_all = jnp.exp(ct)
    at, rt = alpha * g_ex, r * g_in
    bh, kh = beta * g_inv, k2 * g_inv
    bt, kt = beta * g_end, k2 * g_end

    ri = lax.broadcasted_iota(I32, (RBLK, RBLK), 0)
    ci = lax.broadcasted_iota(I32, (RBLK, RBLK), 1)
    same = (ri // RC) == (ci // RC)
    strict = jnp.logical_and(same, ci < ri)
    incl = jnp.logical_and(same, ci <= ri)
    eye = (ri == ci).astype(F32)

    for h in range(N_HEADS):
        hs = slice(h * HEAD_DIM, (h + 1) * HEAD_DIM)
        at_h, rt_h, v_h = at[:, hs], rt[:, hs], v[:, hs]
        gm = _dot3_nt(jnp.concatenate([at_h, rt_h], axis=0), jnp.concatenate([bh[:, hs], kh[:, hs]], axis=0))
        n1 = jnp.where(strict, gm[:RBLK, :RBLK], 0.0)
        a_ak = jnp.where(strict, gm[:RBLK, RBLK:], 0.0)
        a_rb = jnp.where(incl, gm[RBLK:, :RBLK], 0.0)
        a_rk = jnp.where(incl, gm[RBLK:, RBLK:], 0.0)
        tinv = eye + n1
        npow = n1
        for _ in range(int(math.log2(RC)) - 1):
            npow = _dot3(npow, npow)
            tinv = tinv + _dot3(npow, tinv)
        wcat = _dot3(tinv, jnp.concatenate([at_h, _dot3(a_ak, v_h)], axis=1))
        ry = _dot3(a_rb, wcat)
        wa_s[h] = wcat[:, :HEAD_DIM]
        uv_s[h] = wcat[:, HEAD_DIM:]
        rq_s[h] = rt_h + ry[:, :HEAD_DIM]
        yv_s[h] = ry[:, HEAD_DIM:] + _dot3(a_rk, v_h)
        bt_s[h] = bt[:, hs]
        kt_s[h] = kt[:, hs]
        v_s[h] = v_h
        gc_s[h] = g_all[:, hs]

    def chunk_body(c, _):
        c0 = pl.multiple_of(c * RC, RC)
        sl = pl.ds(c0, RC)
        for h in range(N_HEADS):
            s_old = s0_ref[c, h] if not chain else s_ref[h]
            res = _dot3_nt(jnp.concatenate([wa_s[h, sl, :], rq_s[h, sl, :]], axis=0), s_old)
            u = res[:RC] + uv_s[h, sl, :]
            y_s[h, sl, :] = res[RC:] + yv_s[h, sl, :]
            upd = _dot3_tn(jnp.concatenate([u, v_s[h, sl, :]], axis=0),
                           jnp.concatenate([bt_s[h, sl, :], kt_s[h, sl, :]], axis=0))
            s_new = s_old * gc_s[h, pl.ds(c0, 1), :] + upd
            if chain:
                s_ref[h] = s_new
            else:
                sfin_ref[c, h] = s_new
        return 0

    lax.fori_loop(0, RNB, chunk_body, 0)
    if chain:
        sfin_ref[...] = s_ref[...]

    for h in range(N_HEADS):
        hs = slice(h * HEAD_DIM, (h + 1) * HEAD_DIM)
        y = y_s[h]
        mean = jnp.mean(y, axis=-1, keepdims=True)
        var = jnp.mean(jnp.square(y - mean), axis=-1, keepdims=True)
        yn = (y - mean) * lax.rsqrt(var + GN_EPS) * lng_ref[:, hs] + lnb_ref[:, hs]
        bonus = jnp.sum(r[:, hs] * k2[:, hs] * rk_ref[:, hs], axis=-1, keepdims=True) * v[:, hs]
        out_ref[:, hs] = (yn + bonus) * g[:, hs]


def _rwkv_consts():
    idx = np.arange(RBLK)
    same = (idx[:, None] // RC) == (idx[None, :] // RC)
    lt = (same & (idx[None, :] <= idx[:, None])).astype(np.float32)
    bo = same.astype(np.float32)
    lane = np.arange(GROUP_WIDTH)
    bd = ((lane[:, None] // HEAD_DIM) == (lane[None, :] // HEAD_DIM)).astype(np.float32)
    return jnp.asarray(lt, BF16), jnp.asarray(bo, BF16), jnp.asarray(bd, BF16)


def _rwkv(p, lw, chain, pprev=None, s0=None, t_valid=RC):
    rows = p.shape[0]
    nblk = rows // RBLK
    w = GROUP_WIDTH
    vec = lambda a: a.reshape(1, -1)
    consts = [vec(lw['mu_shift']), vec(lw['w0']), lw['w_decay2'].astype(BF16), vec(lw['a0']),
              lw['w_a2'].astype(BF16), lw['w_g2'].astype(BF16), vec(lw['k_k']), vec(lw['k_a']), vec(lw['r_k']),
              vec(lw['lnx_g']), vec(lw['lnx_b'])] + list(_rwkv_consts())
    full = lambda a: pl.BlockSpec(a.shape, lambda i: (0,) * a.ndim)
    blk = pl.BlockSpec((RBLK, RWKV_COLS), lambda i: (i, 0))
    hshape = (N_HEADS, HEAD_DIM, HEAD_DIM)
    if chain:
        ins, in_specs = [p], [blk]
        sfin_spec = pl.BlockSpec(hshape, lambda i: (0, 0, 0))
        sfin_shape = jax.ShapeDtypeStruct(hshape, F32)
    else:
        sspec = pl.BlockSpec((RNB,) + hshape, lambda i: (i, 0, 0, 0))
        ins, in_specs = [p, pprev, s0], [blk, blk, sspec]
        sfin_spec = sspec
        sfin_shape = jax.ShapeDtypeStruct((nblk * RNB,) + hshape, F32)
    hm = lambda: pltpu.VMEM((N_HEADS, RBLK, HEAD_DIM), F32)
    return pl.pallas_call(
        functools.partial(_rwkv_kernel, chain=chain, t_valid=t_valid),
        grid=(nblk,),
        in_specs=in_specs + [full(c) for c in consts],
        out_specs=[pl.BlockSpec((RBLK, w), lambda i: (i, 0)), sfin_spec],
        out_shape=[jax.ShapeDtypeStruct((rows, w), F32), sfin_shape],
        scratch_shapes=[pltpu.VMEM((1, RWKV_COLS), F32), pltpu.VMEM(hshape, F32)] + [hm() for _ in range(9)],
        compiler_params=_cparams(("arbitrary",)),
    )(*ins, *consts)


RT_TM = 256


def _route_kernel(x_ref, ro_ref, ao_ref, wor_ref, woa_ref, g1_ref, sh_ref, sc_ref, n2_ref, wr_ref, br_ref,
                  ltri_ref, x1_ref, h2_ref, idx_ref, gate_ref, rank_ref, cnt_ref, run_ref):
    @pl.when(pl.program_id(0) == 0)
    def _():
        run_ref[...] = jnp.zeros(run_ref.shape, F32)

    tm = x_ref.shape[0]
    mix = _dot(ro_ref[...].astype(BF16), wor_ref[...]) + _dot(ao_ref[...].astype(BF16), woa_ref[...])
    x1 = x_ref[...] + g1_ref[...] * mix
    x1_ref[...] = x1
    y = x1 * lax.rsqrt(jnp.mean(x1 * x1, axis=-1, keepdims=True) + RMS_EPS) * n2_ref[...]
    h2 = (y * (1.0 + sc_ref[...]) + sh_ref[...]).astype(BF16)
    h2_ref[...] = h2
    logits = _dot(h2, wr_ref[...]) + br_ref[...]
    lane = lax.broadcasted_iota(I32, (tm, LANES), 1)
    lane_f = lane.astype(F32)
    lg = logits
    vals, idxs = [], []
    for _ in range(TOP_K):
        m = jnp.max(lg, axis=1, keepdims=True)
        idx = jnp.min(jnp.where(lg == m, lane_f, float(LANES)), axis=1, keepdims=True)
        vals.append(m)
        idxs.append(idx)
        lg = jnp.where(lane_f == idx, -3e38, lg)
    es = [jnp.exp(v - vals[0]) for v in vals]
    den = es[0] + es[1] + es[2] + es[3]
    ohs = [(lane_f == idx).astype(F32) for idx in idxs]
    oh_all = ohs[0] + ohs[1] + ohs[2] + ohs[3]
    base = run_ref[...] + _dot(ltri_ref[...], oh_all.astype(BF16))
    idx_out = jnp.zeros((tm, LANES), F32)
    gate_out = jnp.zeros((tm, LANES), F32)
    rank_out = jnp.zeros((tm, LANES), F32)
    for kk in range(TOP_K):
        rank = jnp.sum(ohs[kk] * base, axis=1, keepdims=True)
        idx_out = jnp.where(lane == kk, idxs[kk], idx_out)
        gate_out = jnp.where(lane == kk, es[kk] / den, gate_out)
        rank_out = jnp.where(lane == kk, rank, rank_out)
    idx_ref[...] = idx_out[:, :TOP_K].astype(I32)
    gate_ref[...] = gate_out[:, :TOP_K]
    rank_ref[...] = rank_out[:, :TOP_K].astype(I32)
    run_ref[...] = run_ref[...] + jnp.sum(oh_all, axis=0, keepdims=True)
    cnt_ref[...] = run_ref[...]


def _out_proj_route(x, ro, ao, w_out, g1, sh2, sc2, norm2_g, w_router, b_router):
    t, d = x.shape
    tm = min(RT_TM, t)
    w = GROUP_WIDTH
    per_row = g1.shape[0] != 1
    wor = w_out[:w].astype(BF16)
    woa = w_out[w:].astype(BF16)
    wr = jnp.pad(w_router, ((0, 0), (0, LANES - N_EXPERTS))).astype(BF16)
    br = jnp.pad(b_router.reshape(1, -1), ((0, 0), (0, LANES - N_EXPERTS)), constant_values=NEG_BIG)
    ltri = jnp.asarray(np.tril(np.ones((tm, tm), np.float32), -1), BF16)
    mod_spec = pl.BlockSpec((tm, d), lambda i: (i, 0)) if per_row else pl.BlockSpec((1, d), lambda i: (0, 0))
    row = lambda n: pl.BlockSpec((tm, n), lambda i: (i, 0))
    full = lambda a: pl.BlockSpec(a.shape, lambda i: (0, 0))
    return pl.pallas_call(
        _route_kernel,
        grid=(t // tm,),
        in_specs=[row(d), row(w), row(w), full(wor), full(woa), mod_spec, mod_spec, mod_spec,
                  pl.BlockSpec((1, d), lambda i: (0, 0)), full(wr), full(br), full(ltri)],
        out_specs=[row(d), row(d), row(TOP_K), row(TOP_K), row(TOP_K), pl.BlockSpec((1, LANES), lambda i: (0, 0))],
        out_shape=[jax.ShapeDtypeStruct((t, d), F32), jax.ShapeDtypeStruct((t, d), BF16),
                   jax.ShapeDtypeStruct((t, TOP_K), I32), jax.ShapeDtypeStruct((t, TOP_K), F32),
                   jax.ShapeDtypeStruct((t, TOP_K), I32), jax.ShapeDtypeStruct((1, LANES), F32)],
        scratch_shapes=[pltpu.VMEM((1, LANES), F32)],
        compiler_params=_cparams(("arbitrary",)),
    )(x, ro, ao, wor, woa, g1, sh2, sc2, norm2_g.reshape(1, d), wr, br, ltri)


EX_TM = 256
BF16_ROW = (SUBLANES, LANES)


def _dispatch_kernel(slot_ref, h_ref, init_ref, out_ref, sem):
    del init_ref
    tm = h_ref.shape[0]

    def body(r, _):
        for kk in range(TOP_K):
            pltpu.make_async_copy(h_ref.at[r], out_ref.at[slot_ref[r * TOP_K + kk]], sem).start()
        return 0

    lax.fori_loop(0, tm, body, 0)
    for _ in range(TOP_K):
        pltpu.make_async_copy(h_ref, out_ref.at[pl.ds(0, tm)], sem).wait()


def _dispatch(h2, slot, rows_sorted):
    t, d = h2.shape
    tm = min(RT_TM, t)
    assert d == SUBLANES * LANES
    h3 = h2.reshape((t,) + BF16_ROW)
    return pl.pallas_call(
        _dispatch_kernel,
        grid=(t // tm,),
        in_specs=[pl.BlockSpec((tm * TOP_K,), lambda i: (i,), memory_space=pltpu.SMEM),
                  pl.BlockSpec((tm,) + BF16_ROW, lambda i: (i, 0, 0)),
                  pl.BlockSpec(memory_space=pl.ANY)],
        out_specs=pl.BlockSpec(memory_space=pl.ANY),
        out_shape=jax.ShapeDtypeStruct(rows_sorted.shape, rows_sorted.dtype),
        scratch_shapes=[pltpu.SemaphoreType.DMA(())],
        input_output_aliases={2: 0},
        compiler_params=_cparams(("arbitrary",)),
    )(slot.reshape(-1), h3, rows_sorted)


def _expert_kernel(be_ref, nv_ref, x_ref, wgu_ref, bgu_ref, wd_ref, bd_ref, o_ref, wgu_bf, wd_bf):
    i = pl.program_id(0)
    changed = jnp.logical_or(i == 0, be_ref[i] != be_ref[jnp.maximum(i - 1, 0)])

    @pl.when(changed)
    def _():
        wgu_bf[...] = wgu_ref[0].astype(BF16)
        wd_bf[...] = wd_ref[0].astype(BF16)

    @pl.when(i < nv_ref[0])
    def _():
        f = wd_bf.shape[0]
        gu = _dot(x_ref[...], wgu_bf[...]) + bgu_ref[0]
        glu = jnp.minimum(gu[:, :f], SWIGLU_LIMIT)
        lin = jnp.clip(gu[:, f:], -SWIGLU_LIMIT, SWIGLU_LIMIT)
        act = glu * jax.nn.sigmoid(SWIGLU_ALPHA * glu) * (lin + 1.0)
        o_ref[...] = _dot(act.astype(BF16), wd_bf[...]) + bd_ref[0]

    @pl.when(i >= nv_ref[0])
    def _():
        o_ref[...] = jnp.zeros(o_ref.shape, F32)


def _experts(rows_sorted, blk_e, n_valid, w_gu, b_gu, w_down, b_down):
    ns, d = rows_sorted.shape
    e, _, f2 = w_gu.shape
    f = f2 // 2
    grid_spec = pltpu.PrefetchScalarGridSpec(
        num_scalar_prefetch=2,
        grid=(ns // EX_TM,),
        in_specs=[pl.BlockSpec((EX_TM, d), lambda i, be, nv: (i, 0)),
                  pl.BlockSpec((1, d, f2), lambda i, be, nv: (be[i], 0, 0)),
                  pl.BlockSpec((1, 1, f2), lambda i, be, nv: (be[i], 0, 0)),
                  pl.BlockSpec((1, f, d), lambda i, be, nv: (be[i], 0, 0)),
                  pl.BlockSpec((1, 1, d), lambda i, be, nv: (be[i], 0, 0))],
        out_specs=pl.BlockSpec((EX_TM, d), lambda i, be, nv: (i, 0)),
        scratch_shapes=[pltpu.VMEM((d, f2), BF16), pltpu.VMEM((f, d), BF16)])
    return pl.pallas_call(
        _expert_kernel,
        grid_spec=grid_spec,
        out_shape=jax.ShapeDtypeStruct((ns, d), F32),
        compiler_params=_cparams(("arbitrary",)),
    )(blk_e, n_valid, rows_sorted, w_gu, b_gu.reshape(e, 1, f2), w_down, b_down.reshape(e, 1, d))


def _combine_kernel(slot_ref, rows_ref, gate_ref, x1_ref, g2_ref, nf_ref, y_ref, buf, sem):
    tm = x1_ref.shape[0]

    def body(r, _):
        for kk in range(TOP_K):
            pltpu.make_async_copy(rows_ref.at[pl.ds(slot_ref[r * TOP_K + kk], 1)],
                                  buf.at[kk, pl.ds(r, 1)], sem).start()
        return 0

    lax.fori_loop(0, tm, body, 0)
    for kk in range(TOP_K):
        pltpu.make_async_copy(rows_ref.at[pl.ds(0, tm)], buf.at[kk], sem).wait()
    gates = gate_ref[...]
    moe = gates[:, 0:1] * buf[0]
    for kk in range(1, TOP_K):
        moe = moe + gates[:, kk:kk + 1] * buf[kk]
    x2 = x1_ref[...] + g2_ref[...] * moe
    y_ref[...] = x2 * lax.rsqrt(jnp.mean(x2 * x2, axis=-1, keepdims=True) + RMS_EPS) * nf_ref[...]


def _combine(out_rows, slot, gates, x1, g2, normf_g):
    t, d = x1.shape
    tm = min(RT_TM, t)
    per_row = g2.shape[0] != 1
    mod_spec = pl.BlockSpec((tm, d), lambda i: (i, 0)) if per_row else pl.BlockSpec((1, d), lambda i: (0, 0))
    return pl.pallas_call(
        _combine_kernel,
        grid=(t // tm,),
        in_specs=[pl.BlockSpec((tm * TOP_K,), lambda i: (i,), memory_space=pltpu.SMEM),
                  pl.BlockSpec(memory_space=pl.ANY),
                  pl.BlockSpec((tm, TOP_K), lambda i: (i, 0)),
                  pl.BlockSpec((tm, d), lambda i: (i, 0)), mod_spec,
                  pl.BlockSpec((1, d), lambda i: (0, 0))],
        out_specs=pl.BlockSpec((tm, d), lambda i: (i, 0)),
        out_shape=jax.ShapeDtypeStruct((t, d), F32),
        scratch_shapes=[pltpu.VMEM((TOP_K, tm, d), F32), pltpu.SemaphoreType.DMA(())],
        compiler_params=_cparams(("arbitrary",)),
    )(slot.reshape(-1), out_rows, gates, x1, g2, normf_g.reshape(1, d))


def _moe_plan(idx_p, rank_p, cnt_p, idx_s, rank_s, cnt_s):
    n_assign = idx_p.size + idx_s.size
    n_tiles = -(-n_assign // EX_TM) + N_EXPERTS
    cp = cnt_p[0, :N_EXPERTS].astype(I32)
    cs = cnt_s[0, :N_EXPERTS].astype(I32)
    padded = (cp + cs + EX_TM - 1) // EX_TM * EX_TM
    pad_end = jnp.cumsum(padded)
    pad_start = pad_end - padded
    slot_p = pad_start[idx_p] + rank_p
    slot_s = pad_start[idx_s] + cp[idx_s] + rank_s
    blk_e = jnp.minimum(jnp.searchsorted(pad_end, jnp.arange(n_tiles, dtype=I32) * EX_TM, side='right'),
                        N_EXPERTS - 1).astype(I32)
    n_valid = (pad_end[-1:] // EX_TM).astype(I32)
    return slot_p, slot_s, blk_e, n_valid, n_tiles * EX_TM


PG = 8
QROWS = N_HEADS * SUBLANES


def _page_specs(block, n_pages):
    def spec(u):
        return pl.BlockSpec(block, lambda b, j, pt: (pt[b * n_pages + j * PG + u],) + (0,) * (len(block) - 1))
    return [spec(u) for u in range(PG)]


def _head_sum(x):
    out = x[:SUBLANES]
    for h in range(1, N_HEADS):
        out = out + x[h * SUBLANES:(h + 1) * SUBLANES]
    return out


def _sample_index_kernel(pt_ref, qi_ref, wrep_ref, kin_ref, *rest, n_pages, t_new, topk):
    del pt_ref
    pages = rest[:PG]
    keys_ref, tau_ref = rest[PG:]
    j = pl.program_id(1)
    qi = qi_ref[0]
    wrep = wrep_ref[0]

    def scores(ki_bf):
        s = _dot_nt(qi, ki_bf)
        return _head_sum(jnp.maximum(s, 0.0) * wrep)

    for u in range(PG):
        keys_ref[0, j * PG + u] = _float_key(scores(pages[u][0].astype(BF16)))

    @pl.when(j == pl.num_programs(1) - 1)
    def _():
        qrow = lax.broadcasted_iota(I32, (SUBLANES, PAGE_SIZE), 0)
        col = lax.broadcasted_iota(I32, (SUBLANES, PAGE_SIZE), 1)
        ok = jnp.logical_and(col <= qrow, col < t_new)
        keys_ref[0, n_pages] = jnp.where(ok, _float_key(scores(kin_ref[0])), KEY_NEG_INF)

        def count_ge(mid):
            def cbody(t, c):
                return c + jnp.where(keys_ref[0, t] >= mid, 1.0, 0.0)

            c = lax.fori_loop(0, n_pages + 1, cbody, jnp.zeros((SUBLANES, PAGE_SIZE), F32))
            return jnp.broadcast_to(jnp.sum(c, axis=1, keepdims=True), (SUBLANES, LANES))

        tau = _kth_largest_key(count_ge, SUBLANES, topk)
        tau_ref[0] = jnp.maximum(tau, KEY_NEG_INF + 1)


def _sample_attend_kernel(pt_ref, q_ref, keys_ref, tau_ref, band_ref, kn_ref, vn_ref, *rest, n_pages):
    del pt_ref
    kpages = rest[:PG]
    vpages = rest[PG:2 * PG]
    o_ref, m_ref, l_ref, acc_ref = rest[2 * PG:]
    j = pl.program_id(1)
    last = j == pl.num_programs(1) - 1

    @pl.when(j == 0)
    def _():
        m_ref[...] = jnp.full(m_ref.shape, NEG_BIG, F32)
        l_ref[...] = jnp.zeros(l_ref.shape, F32)
        acc_ref[...] = jnp.zeros(acc_ref.shape, F32)

    q = q_ref[0]
    tau = _rep(tau_ref[0], N_HEADS, axis=0)
    width = acc_ref.shape[1]

    def attend(k_bf, v_bf, key_tile, bias):
        s = _dot_nt(q, k_bf)
        if bias is not None:
            s = s + bias
        s = jnp.where(_rep(key_tile, N_HEADS, axis=0) >= tau, s, NEG_BIG)
        m_old = m_ref[...]
        m_new = jnp.maximum(m_old, jnp.broadcast_to(jnp.max(s, axis=1, keepdims=True), m_old.shape))
        p = jnp.exp(s - m_new)
        alpha = jnp.exp(m_old - m_new)
        l_ref[...] = alpha * l_ref[...] + jnp.broadcast_to(jnp.sum(p, axis=1, keepdims=True), m_old.shape)
        m_ref[...] = m_new
        acc_ref[...] = acc_ref[...] * _rep(alpha, width // LANES, axis=1) + _dot(p.astype(BF16), v_bf)

    for u in range(PG):
        bias = None
        if u == PG - 1:
            bias = jnp.where(last, band_ref[:, :PAGE_SIZE], 0.0)
        attend(kpages[u][0].astype(BF16), vpages[u][0].astype(BF16), keys_ref[0, j * PG + u], bias)

    @pl.when(last)
    def _():
        attend(kn_ref[0], vn_ref[0], keys_ref[0, n_pages], band_ref[:, PAGE_SIZE:])
        accn = acc_ref[...] / _rep(l_ref[...], width // LANES, axis=1)
        lane_head = lax.broadcasted_iota(I32, (SUBLANES, width), 1) // HEAD_DIM
        out = jnp.zeros((SUBLANES, width), F32)
        for h in range(N_HEADS):
            out = out + jnp.where(lane_head == h, accn[h * SUBLANES:(h + 1) * SUBLANES], 0.0)
        o_ref[0] = out


def _sample_attention(qs, qib, wi, k_new_bf, v_new_bf, ki_new_bf, cache_k, cache_v, cache_kidx, page_table,
                      rel_bias, b, t_new):
    w = GROUP_WIDTH
    n_pages = page_table.shape[1]
    assert n_pages % PG == 0 and t_new <= SUBLANES and cache_k.shape[1] == PAGE_SIZE
    past = n_pages * PAGE_SIZE
    topk = min(TOPK_MAX, (past + t_new) // 4)
    pt = page_table.reshape(-1)
    padq = lambda a: jnp.pad(a, ((0, 0), (0, SUBLANES - t_new)) + ((0, 0),) * (a.ndim - 2))
    padk = lambda a: jnp.pad(a.reshape(b, t_new, -1), ((0, 0), (0, PAGE_SIZE - t_new), (0, 0)))
    qi_r = padq(qib.reshape(b, t_new, N_IDX_HEADS, IDX_DIM)).transpose(0, 2, 1, 3).reshape(b, QROWS, IDX_DIM)
    w_r = padq(wi.reshape(b, t_new, N_IDX_HEADS)).transpose(0, 2, 1).reshape(b, QROWS, 1)
    w_r = jnp.broadcast_to(w_r, (b, QROWS, LANES))
    q4 = padq(qs.reshape(b, t_new, N_HEADS, HEAD_DIM)).transpose(0, 2, 1, 3)
    q_bd = (q4[:, :, :, None, :] * jnp.eye(N_HEADS, dtype=qs.dtype)[None, :, None, :, None]).reshape(b, QROWS, w)
    kin, kn, vn = padk(ki_new_bf), padk(k_new_bf), padk(v_new_bf)
    ck = cache_k.reshape(cache_k.shape[0], PAGE_SIZE, w)
    cv = cache_v.reshape(cache_v.shape[0], PAGE_SIZE, w)
    band = _bias_band(rel_bias, SUBLANES, 2 * PAGE_SIZE, PAGE_SIZE).reshape(QROWS, 2 * PAGE_SIZE)

    per_b = lambda shape: pl.BlockSpec((1,) + shape, lambda bb, j, p_: (bb,) + (0,) * len(shape))
    steps = n_pages // PG
    keys, tau = pl.pallas_call(
        functools.partial(_sample_index_kernel, n_pages=n_pages, t_new=t_new, topk=topk),
        grid_spec=pltpu.PrefetchScalarGridSpec(
            num_scalar_prefetch=1, grid=(b, steps),
            in_specs=[per_b((QROWS, IDX_DIM)), per_b((QROWS, LANES)), per_b((PAGE_SIZE, IDX_DIM))]
            + _page_specs((1, PAGE_SIZE, IDX_DIM), n_pages),
            out_specs=[per_b((n_pages + 1, SUBLANES, PAGE_SIZE)), per_b((SUBLANES, LANES))]),
        out_shape=[jax.ShapeDtypeStruct((b, n_pages + 1, SUBLANES, PAGE_SIZE), I32),
                   jax.ShapeDtypeStruct((b, SUBLANES, LANES), I32)],
        compiler_params=_cparams(("arbitrary", "arbitrary")),
    )(pt, qi_r, w_r, kin, *([cache_kidx] * PG))
    out = pl.pallas_call(
        functools.partial(_sample_attend_kernel, n_pages=n_pages),
        grid_spec=pltpu.PrefetchScalarGridSpec(
            num_scalar_prefetch=1, grid=(b, steps),
            in_specs=[per_b((QROWS, w)), per_b((n_pages + 1, SUBLANES, PAGE_SIZE)), per_b((SUBLANES, LANES)),
                      pl.BlockSpec(band.shape, lambda bb, j, p_: (0, 0)), per_b((PAGE_SIZE, w)),
                      per_b((PAGE_SIZE, w))]
            + _page_specs((1, PAGE_SIZE, w), n_pages) + _page_specs((1, PAGE_SIZE, w), n_pages),
            out_specs=per_b((SUBLANES, w)),
            scratch_shapes=[pltpu.VMEM((QROWS, LANES), F32), pltpu.VMEM((QROWS, LANES), F32),
                            pltpu.VMEM((QROWS, w), F32)]),
        out_shape=jax.ShapeDtypeStruct((b, SUBLANES, w), F32),
        compiler_params=_cparams(("arbitrary", "arbitrary")),
    )(pt, q_bd, keys, tau, band, kn, vn, *([ck] * PG), *([cv] * PG))
    return out[:, :t_new].reshape(b * t_new, w)


def kernel(x_prompt, x_sample, c_prompt, c_sample, cache_k, cache_v, cache_kidx, page_table, state_wkv,
           state_shift, w_ada, b_ada, norm1_g, w_in, mu_shift, w0, w_decay2, a0, w_a2, w_g2, k_k, k_a, r_k,
           lnx_g, lnx_b, rel_bias, w_out, norm2_g, w_router, b_router, w_gu, b_gu, w_down, b_down, normf_g):
    depth = w_in.shape[0]
    assert depth == 1, "the merged prompt+sample expert pass is written for a single layer"
    bp, sp, d = x_prompt.shape
    bs, ts, _ = x_sample.shape
    assert bp == 1 and sp % RBLK == 0 and bs % RNB == 0 and ts <= RC
    l = 0
    lw = {'mu_shift': mu_shift[l], 'w0': w0[l], 'w_decay2': w_decay2[l], 'a0': a0[l], 'w_a2': w_a2[l],
          'w_g2': w_g2[l], 'k_k': k_k[l], 'k_a': k_a[l], 'r_k': r_k[l], 'lnx_g': lnx_g[l], 'lnx_b': lnx_b[l]}
    ns = bs * ts

    c_all = jnp.concatenate([c_prompt, c_sample], axis=0)
    n_c = c_all.shape[0]
    c_all = jnp.pad(c_all, ((0, -n_c % SUBLANES), (0, 0)))
    mod = _adaln(c_all, w_ada[l], b_ada[l])
    mods_p = [mod[0:1, i * d:(i + 1) * d] for i in range(6)]
    mods_s = [jnp.repeat(mod[bp:bp + bs, i * d:(i + 1) * d], ts, axis=0) for i in range(6)]

    xp = x_prompt.reshape(sp, d)
    xs = x_sample.reshape(ns, d)
    pr_p, k_p, v_p, ki_p, wi_p, qs_p, kb_p, vb_p, qib_p, kib_p = _in_proj(xp, norm1_g[l], mods_p[0], mods_p[1],
                                                                          w_in[l])
    pr_s, k_s, v_s, ki_s, wi_s, qs_s, kb_s, vb_s, qib_s, kib_s = _in_proj(xs, norm1_g[l], mods_s[0], mods_s[1],
                                                                          w_in[l])

    ro_p, wkv_p = _rwkv(pr_p, lw, chain=True)
    att_p = _prompt_attention(qs_p, qib_p, wi_p, kib_p, kb_p, vb_p, rel_bias)

    pr_s3 = pr_s.reshape(bs, ts, RWKV_COLS)
    prev_s3 = jnp.concatenate([state_shift[l][:, None, :], pr_s3[:, :-1]], axis=1)
    padc = lambda a: jnp.pad(a, ((0, 0), (0, RC - ts), (0, 0))).reshape(bs * RC, RWKV_COLS)
    ro_s, wkv_s = _rwkv(padc(pr_s3), lw, chain=False, pprev=padc(prev_s3), s0=state_wkv[l], t_valid=ts)
    ro_s = ro_s.reshape(bs, RC, GROUP_WIDTH)[:, :ts].reshape(ns, GROUP_WIDTH)
    att_s = _sample_attention(qs_s, qib_s, wi_s, kb_s, vb_s, kib_s, cache_k[l], cache_v[l], cache_kidx[l],
                              page_table, rel_bias, bs, ts)

    x1_p, h2_p, idx_p, gate_p, rank_p, cnt_p = _out_proj_route(xp, ro_p, att_p, w_out[l], mods_p[2], mods_p[3],
                                                               mods_p[4], norm2_g[l], w_router[l], b_router[l])
    x1_s, h2_s, idx_s, gate_s, rank_s, cnt_s = _out_proj_route(xs, ro_s, att_s, w_out[l], mods_s[2], mods_s[3],
                                                               mods_s[4], norm2_g[l], w_router[l], b_router[l])
    slot_p, slot_s, blk_e, n_valid, n_slots = _moe_plan(idx_p, rank_p, cnt_p, idx_s, rank_s, cnt_s)
    rows = jnp.zeros((n_slots,) + BF16_ROW, BF16)
    rows = _dispatch(h2_p, slot_p, rows)
    rows = _dispatch(h2_s, slot_s, rows)
    out_rows = _experts(rows.reshape(n_slots, d), blk_e, n_valid, w_gu[l], b_gu[l], w_down[l], b_down[l])
    y_p = _combine(out_rows, slot_p, gate_p, x1_p, mods_p[5], normf_g)
    y_s = _combine(out_rows, slot_s, gate_s, x1_s, mods_s[5], normf_g)

    hd = (N_HEADS, HEAD_DIM)
    return (y_p.reshape(bp, sp, d), y_s.reshape(bs, ts, d),
            k_p.reshape((1, bp, sp) + hd), v_p.reshape((1, bp, sp) + hd), ki_p.reshape(1, bp, sp, IDX_DIM),
            wkv_p.reshape((1, bp) + (N_HEADS, HEAD_DIM, HEAD_DIM)), pr_p[sp - 1:sp].reshape(1, bp, RWKV_COLS),
            k_s.reshape((1, bs, ts) + hd), v_s.reshape((1, bs, ts) + hd), ki_s.reshape(1, bs, ts, IDX_DIM),
            wkv_s.reshape((1, bs) + (N_HEADS, HEAD_DIM, HEAD_DIM)), pr_s3[:, ts - 1].reshape(1, bs, RWKV_COLS))
```

```python
import functools
import math

import jax
import jax.numpy as jnp
import numpy as np
from jax import lax
from jax.experimental import pallas as pl
from jax.experimental.pallas import tpu as pltpu

F32 = jnp.float32
BF16 = jnp.bfloat16
I32 = jnp.int32

HEAD_DIM = 64
N_HEADS = 8
GROUP_WIDTH = N_HEADS * HEAD_DIM
DECAY_LORA, AAA_LORA, GATE_LORA = 64, 64, 128
RWKV_COLS = 3 * GROUP_WIDTH + DECAY_LORA + AAA_LORA + GATE_LORA
IDX_DIM = 64
N_IDX_HEADS = 8
TOPK_MAX = 256
N_BUCKETS = 32
MAX_DISTANCE = 128
N_EXPERTS = 32
TOP_K = 4
SWIGLU_LIMIT = 7.0
SWIGLU_ALPHA = 1.702
RMS_EPS = 1e-6
GN_EPS = HEAD_DIM * 1e-5
PAGE_SIZE = 128

LANES = 128
SUBLANES = 8
VMEM_LIMIT = 56 * 1024 * 1024

NEG_BIG = -1e30
KEY_NEG_INF = -2139095041


def _cparams(sem):
    return pltpu.CompilerParams(dimension_semantics=sem, vmem_limit_bytes=VMEM_LIMIT)


def _dot(a, b):
    return jnp.dot(a, b, preferred_element_type=F32)


def _dot_nt(a, b):
    return lax.dot_general(a, b, (((1,), (1,)), ((), ())), preferred_element_type=F32)


def _split2(a):
    hi = a.astype(BF16)
    lo = (a - hi.astype(F32)).astype(BF16)
    return hi, lo


def _split3(a):
    hi = a.astype(BF16)
    r1 = a - hi.astype(F32)
    mid = r1.astype(BF16)
    lo = (r1 - mid.astype(F32)).astype(BF16)
    return hi, mid, lo


def _rep(x, n, axis):
    return jnp.concatenate([x] * n, axis=axis)


def _float_key(x):
    b = pltpu.bitcast(x, I32)
    return b ^ ((b >> 31) & 0x7FFFFFFF)


def _ada_kernel(c_ref, w_ref, b_ref, o_ref):
    c = c_ref[...]
    s = c * jax.nn.sigmoid(c)
    o_ref[...] = _dot(s.astype(BF16), w_ref[...].astype(BF16)) + b_ref[...]


def _adaln(c, w_ada, b_ada):
    r, d = c.shape
    n = w_ada.shape[1]
    tn = 1536
    return pl.pallas_call(
        _ada_kernel,
        grid=(n // tn,),
        in_specs=[pl.BlockSpec((r, d), lambda j: (0, 0)),
                  pl.BlockSpec((d, tn), lambda j: (0, j)),
                  pl.BlockSpec((1, tn), lambda j: (0, j))],
        out_specs=pl.BlockSpec((r, tn), lambda j: (0, j)),
        out_shape=jax.ShapeDtypeStruct((r, n), F32),
        compiler_params=_cparams(("arbitrary",)),
    )(c, w_ada, b_ada.reshape(1, n))


def _inproj_kernel(x_ref, g_ref, sh_ref, sc_ref, wr_ref, wa_ref, wk_ref,
                   pr_ref, k_ref, v_ref, ki_ref, wi_ref, qs_ref, kb_ref, vb_ref, qib_ref, kib_ref, vt_ref):
    x = x_ref[...]
    y = x * lax.rsqrt(jnp.mean(x * x, axis=-1, keepdims=True) + RMS_EPS) * g_ref[...]
    h = (y * (1.0 + sc_ref[...]) + sh_ref[...]).astype(BF16)
    pr_ref[...] = _dot(h, wr_ref[...])
    a = _dot(h, wa_ref[...])
    w = GROUP_WIDTH
    q, k, v, qi = a[:, :w], a[:, w:2 * w], a[:, 2 * w:3 * w], a[:, 3 * w:4 * w]
    k_ref[...] = k
    v_ref[...] = v
    qs_ref[...] = (q * HEAD_DIM ** -0.5).astype(BF16)
    kb_ref[...] = k.astype(BF16)
    vb_ref[...] = v.astype(BF16)
    vt_ref[...] = v.T.astype(BF16)
    qib_ref[...] = (qi * IDX_DIM ** -0.5).astype(BF16)
    kw = _dot(h, wk_ref[...])
    ki = kw[:, :IDX_DIM]
    ki_ref[...] = ki
    kib_ref[...] = ki.astype(BF16)
    wi_ref[...] = kw[:, IDX_DIM:IDX_DIM + N_IDX_HEADS] * N_IDX_HEADS ** -0.5


def _in_proj(x, norm_g, shift, scale, w_in):
    t, d = x.shape
    tm = min(512, t)
    per_row = shift.shape[0] != 1
    w = GROUP_WIDTH
    a0 = RWKV_COLS
    wr = w_in[:, :a0].astype(BF16)
    wa = w_in[:, a0:a0 + 4 * w].astype(BF16)
    wk = jnp.pad(w_in[:, a0 + 4 * w:], ((0, 0), (0, LANES - IDX_DIM - N_IDX_HEADS))).astype(BF16)
    mod_spec = pl.BlockSpec((tm, d), lambda i: (i, 0)) if per_row else pl.BlockSpec((1, d), lambda i: (0, 0))
    row = lambda n: pl.BlockSpec((tm, n), lambda i: (i, 0))
    full = lambda a: pl.BlockSpec(a.shape, lambda i: (0, 0))
    sds = lambda n, dt: jax.ShapeDtypeStruct((t, n), dt)
    return pl.pallas_call(
        _inproj_kernel,
        grid=(t // tm,),
        in_specs=[row(d), pl.BlockSpec((1, d), lambda i: (0, 0)), mod_spec, mod_spec, full(wr), full(wa), full(wk)],
        out_specs=[row(a0), row(w), row(w), row(IDX_DIM), row(N_IDX_HEADS), row(w), row(w), row(w), row(w),
                   row(IDX_DIM), pl.BlockSpec((w, tm), lambda i: (0, i))],
        out_shape=[sds(a0, F32), sds(w, F32), sds(w, F32), sds(IDX_DIM, F32), sds(N_IDX_HEADS, F32),
                   sds(w, BF16), sds(w, BF16), sds(w, BF16), sds(w, BF16), sds(IDX_DIM, BF16),
                   jax.ShapeDtypeStruct((w, t), BF16)],
        compiler_params=_cparams(("arbitrary",)),
    )(x, norm_g.reshape(1, d), shift, scale, wr, wa, wk)


def _bias_band_kernel(rb_ref, o_ref, *, sign, off):
    _, rows, cols = o_ref.shape
    r = lax.broadcasted_iota(I32, (rows, cols), 0)
    c = lax.broadcasted_iota(I32, (rows, cols), 1)
    n = jnp.maximum(sign * (r - c) + off, 0)
    max_exact = N_BUCKETS // 2
    nf = jnp.maximum(n, 1).astype(F32)
    large = max_exact + (jnp.log(nf / max_exact) / math.log(MAX_DISTANCE / max_exact)
                         * (N_BUCKETS - max_exact)).astype(I32)
    large = jnp.minimum(large, N_BUCKETS - 1)
    bucket = jnp.where(n < max_exact, n, large)
    for h in range(N_HEADS):
        far = rb_ref[N_BUCKETS - 1, h]
        acc = jnp.zeros((rows, cols), F32)
        for b in range(N_BUCKETS - 1):
            acc = jnp.where(bucket == b, rb_ref[b, h] - far, acc)
        o_ref[h] = acc


def _bias_band(rel_bias, rows, cols, sign, off):
    return pl.pallas_call(
        functools.partial(_bias_band_kernel, sign=sign, off=off),
        in_specs=[pl.BlockSpec(memory_space=pltpu.SMEM)],
        out_specs=pl.BlockSpec((N_HEADS, rows, cols), lambda: (0, 0, 0)),
        out_shape=jax.ShapeDtypeStruct((N_HEADS, rows, cols), F32),
    )(rel_bias)


def _key_float(k):
    return pltpu.bitcast(k ^ ((k >> 31) & 0x7FFFFFFF), F32)


def _kth_largest_key(count_ge, lo0, hi0, topk):
    def cond(carry):
        return carry[2] > 0

    def body(carry):
        lo, hi, _ = carry
        mid_f = _float_key(0.5 * _key_float(lo) + 0.5 * _key_float(hi))
        mid_i = (lo | hi) - ((lo ^ hi) >> 1)
        mid = jnp.where(jnp.logical_and(mid_f > lo, mid_f <= hi), mid_f, mid_i)
        cnt = count_ge(mid)
        active = lo < hi
        exact = jnp.logical_and(active, cnt == topk)
        ge = cnt >= topk
        lo_n = jnp.where(active, jnp.where(ge, mid, lo), lo)
        hi_n = jnp.where(active, jnp.where(exact, mid, jnp.where(ge, hi, mid - 1)), hi)
        return lo_n, hi_n, jnp.max(jnp.where(lo_n < hi_n, 1.0, 0.0))

    lo, _, _ = lax.while_loop(cond, body, (lo0, hi0, jnp.max(jnp.where(lo0 < hi0, 1.0, 0.0))))
    return lo


PQB = 256
PKT = 512
PKB = 1024
PSUB = 256


def _prompt_att_kernel(qs_ref, qib_ref, wit_ref, kib_ref, kb_ref, vt_ref, band_ref, o_ref,
                       keys_ref, tau_ref, m_ref, l_ref, acc_ref, *, topk, qb_rows):
    qb = pl.program_id(0)
    j = pl.program_id(1)
    q_lo = qb * qb_rows
    n_kt = (q_lo + qb_rows + PKT - 1) // PKT

    @pl.when(j == 0)
    def _index_phase():
        q_pos = q_lo + lax.broadcasted_iota(I32, (PKT, qb_rows), 1)

        def tile_body(kt, carry):
            k0 = pl.multiple_of(kt * PKT, PKT)
            ki = kib_ref[pl.ds(k0, PKT), :]
            acc = jnp.zeros((PKT, qb_rows), F32)
            for h in range(N_IDX_HEADS):
                s = _dot_nt(ki, qib_ref[:, h * IDX_DIM:(h + 1) * IDX_DIM])
                acc = acc + jnp.maximum(s, 0.0) * wit_ref[h:h + 1, :]
            k_pos = k0 + lax.broadcasted_iota(I32, (PKT, qb_rows), 0)
            causal = k_pos <= q_pos
            keys_ref[kt] = jnp.where(causal, _float_key(acc), KEY_NEG_INF)
            smax, smin = carry
            grp = lambda x: x.reshape(PKT // SUBLANES, SUBLANES, qb_rows)
            smax = jnp.maximum(smax, jnp.max(grp(jnp.where(causal, acc, -jnp.inf)), axis=0))
            smin = jnp.minimum(smin, jnp.min(grp(jnp.where(causal, acc, jnp.inf)), axis=0))
            return smax, smin

        smax, smin = lax.fori_loop(
            0, n_kt, tile_body,
            (jnp.full((SUBLANES, qb_rows), -jnp.inf, F32), jnp.full((SUBLANES, qb_rows), jnp.inf, F32)))
        fmax = jnp.max(smax, axis=0, keepdims=True)
        fmin = jnp.min(smin, axis=0, keepdims=True)
        hi0 = jnp.where(fmax == 0.0, 0, _float_key(fmax))
        lo0 = jnp.where(fmin == 0.0, -1, _float_key(fmin))
        n_causal = q_lo + 1 + lax.broadcasted_iota(I32, (1, qb_rows), 1)
        lo0 = jnp.where(n_causal < topk, KEY_NEG_INF + 1, lo0)
        hi0 = jnp.where(n_causal < topk, KEY_NEG_INF + 1, hi0)

        def count_ge(mid):
            def cbody(kt, c):
                ge = jnp.where(keys_ref[kt] >= mid, 1.0, 0.0)
                return c + jnp.sum(ge.reshape(PKT // SUBLANES, SUBLANES, qb_rows), axis=0)

            c = lax.fori_loop(0, n_kt, cbody, jnp.zeros((SUBLANES, qb_rows), F32))
            return jnp.sum(c, axis=0, keepdims=True)

        tau_ref[...] = _kth_largest_key(count_ge, lo0, hi0, topk)
        m_ref[...] = jnp.full(m_ref.shape, NEG_BIG, F32)
        l_ref[...] = jnp.zeros(l_ref.shape, F32)
        acc_ref[...] = jnp.zeros(acc_ref.shape, F32)

    kb = j - 1
    last_kb = (q_lo + qb_rows - 1) // PKB

    @pl.when(jnp.logical_and(j >= 1, kb <= last_kb))
    def _attend_phase():
        tau = tau_ref[...]
        for u in range(PKB // PSUB):
            s0 = kb * PKB + u * PSUB

            def sub_tile(near, u=u, s0=s0):
                kt = (kb * PKB + u * PSUB) // PKT
                c0 = (u * PSUB) % PKT
                sel = keys_ref[kt, c0:c0 + PSUB, :] >= tau
                if near:
                    which = jnp.clip((q_lo - s0) // PSUB, 0, 1)
                for h in range(N_HEADS):
                    hs = slice(h * HEAD_DIM, (h + 1) * HEAD_DIM)
                    s = _dot_nt(kb_ref[u * PSUB:(u + 1) * PSUB, hs], qs_ref[:, hs])
                    if near:
                        s = s + band_ref[h, 1 - which]
                    s = jnp.where(sel, s, NEG_BIG)
                    m_old = m_ref[h]
                    m_new = jnp.maximum(m_old, jnp.max(s, axis=0, keepdims=True))
                    p = jnp.exp(s - m_new)
                    alpha = jnp.exp(m_old - m_new)
                    l_ref[h] = alpha * l_ref[h] + jnp.sum(p, axis=0, keepdims=True)
                    m_ref[h] = m_new
                    pv = _dot(vt_ref[hs, u * PSUB:(u + 1) * PSUB], p.astype(BF16))
                    acc_ref[h] = acc_ref[h] * alpha + pv

            is_near = s0 + PSUB > q_lo - PSUB
            in_range = s0 < q_lo + qb_rows

            @pl.when(jnp.logical_and(in_range, is_near))
            def _():
                sub_tile(True)

            @pl.when(jnp.logical_and(in_range, jnp.logical_not(is_near)))
            def _():
                sub_tile(False)

    @pl.when(j == last_kb + 1)
    def _finish():
        for h in range(N_HEADS):
            hs = slice(h * HEAD_DIM, (h + 1) * HEAD_DIM)
            o_ref[:, hs] = (acc_ref[h] / l_ref[h]).T


def _prompt_attention(qs, qib, wi, kib, kb, vt, rel_bias):
    s, w = qs.shape
    topk = min(TOPK_MAX, s // 4)
    qb_rows = min(PQB, s)
    assert s % qb_rows == 0 and s % PKB == 0 and qb_rows == PSUB
    nqb = s // qb_rows
    nkb = s // PKB
    band = _bias_band(rel_bias, 2 * PSUB, qb_rows, -1, PSUB).reshape(N_HEADS, 2, PSUB, qb_rows)

    def last_kb(i):
        return (i * qb_rows + qb_rows - 1) // PKB

    qrow = lambda n: pl.BlockSpec((qb_rows, n), lambda i, j: (i, 0))
    return pl.pallas_call(
        functools.partial(_prompt_att_kernel, topk=topk, qb_rows=qb_rows),
        grid=(nqb, nkb + 1),
        in_specs=[qrow(w), qrow(w), pl.BlockSpec((N_IDX_HEADS, qb_rows), lambda i, j: (0, i)),
                  pl.BlockSpec((s, IDX_DIM), lambda i, j: (0, 0)),
                  pl.BlockSpec((PKB, w), lambda i, j: (jnp.minimum(jnp.maximum(j - 1, 0), last_kb(i)), 0)),
                  pl.BlockSpec((w, PKB), lambda i, j: (0, jnp.minimum(jnp.maximum(j - 1, 0), last_kb(i)))),
                  pl.BlockSpec(band.shape, lambda i, j: (0, 0, 0, 0))],
        out_specs=qrow(w),
        out_shape=jax.ShapeDtypeStruct((s, w), F32),
        scratch_shapes=[pltpu.VMEM((s // PKT, PKT, qb_rows), I32),
                        pltpu.VMEM((1, qb_rows), I32),
                        pltpu.VMEM((N_HEADS, 1, qb_rows), F32),
                        pltpu.VMEM((N_HEADS, 1, qb_rows), F32),
                        pltpu.VMEM((N_HEADS, HEAD_DIM, qb_rows), F32)],
        compiler_params=_cparams(("arbitrary", "arbitrary")),
    )(qs, qib, wi.T, kib, kb, vt, band)


RC = 16
RNB = 16
RBLK = RC * RNB


def _dot3(a, b):
    ah, al = _split2(a)
    bh, bl = _split2(b)
    return _dot(ah, bh) + (_dot(ah, bl) + _dot(al, bh))


def _dot3_nt(a, b):
    ah, al = _split2(a)
    bh, bl = _split2(b)
    return _dot_nt(ah, bh) + (_dot_nt(ah, bl) + _dot_nt(al, bh))


def _dot3_tn(a, b):
    dn = (((0,), (0,)), ((), ()))
    ah, al = _split2(a)
    bh, bl = _split2(b)
    f = lambda x, y: lax.dot_general(x, y, dn, preferred_element_type=F32)
    return f(ah, bh) + (f(ah, bl) + f(al, bh))


def _dot_exact_lhs(m_bf, x):
    hi, mid, lo = _split3(x)
    return _dot(m_bf, hi) + (_dot(m_bf, mid) + _dot(m_bf, lo))


def _dot_exact_rhs(x, m_bf):
    hi, mid, lo = _split3(x)
    return _dot(hi, m_bf) + (_dot(mid, m_bf) + _dot(lo, m_bf))


def _rwkv_kernel(*refs, chain, t_valid):
    if chain:
        p_ref, = refs[:1]
        rest = refs[1:]
    else:
        p_ref, pprev_ref, s0_ref = refs[:3]
        rest = refs[3:]
    (mu_ref, w0_ref, wd2_ref, a0_ref, wa2_ref, wg2_ref, kk_ref, ka_ref, rk_ref, lng_ref, lnb_ref,
     lt_ref, bo_ref, bd_ref, out_ref, sfin_ref,
     carry_ref, s_ref, wa_s, rq_s, uv_s, yv_s, bt_s, kt_s, v_s, gc_s, y_s) = rest
    i = pl.program_id(0)
    w = GROUP_WIDTH
    p = p_ref[...]
    row = lax.broadcasted_iota(I32, (RBLK, 1), 0)
    if chain:
        @pl.when(i == 0)
        def _():
            carry_ref[...] = jnp.zeros(carry_ref.shape, F32)
            s_ref[...] = jnp.zeros(s_ref.shape, F32)

        pprev = jnp.where(row == 0, carry_ref[...], pltpu.roll(p, 1, axis=0))
        carry_ref[...] = p[RBLK - 1:RBLK, :]
    else:
        pprev = pprev_ref[...]
    ps = p + (pprev - p) * mu_ref[...]
    r, k, v = ps[:, :w], ps[:, w:2 * w], ps[:, 2 * w:3 * w]
    o = 3 * w
    xw = ps[:, o:o + DECAY_LORA]
    xa = ps[:, o + DECAY_LORA:o + DECAY_LORA + AAA_LORA]
    xg = ps[:, o + DECAY_LORA + AAA_LORA:]
    dec = w0_ref[...] + _dot(jnp.tanh(xw).astype(BF16), wd2_ref[...])
    softplus = jnp.maximum(-dec, 0.0) + jnp.log(1.0 + jnp.exp(-jnp.abs(dec)))
    lw = -jnp.exp(-softplus - 0.5)
    a = jax.nn.sigmoid(a0_ref[...] + _dot(xa.astype(BF16), wa2_ref[...]))
    g = _dot(jax.nn.sigmoid(xg).astype(BF16), wg2_ref[...])
    kk = k * kk_ref[...]
    kk = kk * lax.rsqrt(jnp.maximum(_dot_exact_rhs(kk * kk, bd_ref[...]), 1e-24))
    k2 = k * (1.0 + (a - 1.0) * ka_ref[...])
    alpha = -kk
    beta = kk * a
    if t_valid < RC:
        valid = (row % RC) < t_valid
        zero = lambda x: jnp.where(valid, x, 0.0)
        lw, alpha, beta, k2, r, v = zero(lw), zero(alpha), zero(beta), zero(k2), zero(r), zero(v)
    cl = _dot_exact_lhs(lt_ref[...], lw)
    ct = _dot_exact_lhs(bo_ref[...], lw)
    g_in = jnp.exp(cl)
    g_ex = jnp.exp(cl - lw)
    g_inv = jnp.exp(-cl)
    g_end = jnp.exp(ct - cl)
    g_all = jnp.exp(ct)
    at, rt = alpha * g_ex, r * g_in
    bh, kh = beta * g_inv, k2 * g_inv
    bt, kt = beta * g_end, k2 * g_end

    ri = lax.broadcasted_iota(I32, (RBLK, RBLK), 0)
    ci = lax.broadcasted_iota(I32, (RBLK, RBLK), 1)
    same = (ri // RC) == (ci // RC)
    strict = jnp.logical_and(same, ci < ri)
    incl = jnp.logical_and(same, ci <= ri)
    eye = (ri == ci).astype(F32)

    for h in range(N_HEADS):
        hs = slice(h * HEAD_DIM, (h + 1) * HEAD_DIM)
        at_h, rt_h, v_h = at[:, hs], rt[:, hs], v[:, hs]
        gm = _dot3_nt(jnp.concatenate([at_h, rt_h], axis=0), jnp.concatenate([bh[:, hs], kh[:, hs]], axis=0))
        n1 = jnp.where(strict, gm[:RBLK, :RBLK], 0.0)
        a_ak = jnp.where(strict, gm[:RBLK, RBLK:], 0.0)
        a_rb = jnp.where(incl, gm[RBLK:, :RBLK], 0.0)
        a_rk = jnp.where(incl, gm[RBLK:, RBLK:], 0.0)
        tinv = eye + n1
        npow = n1
        for _ in range(int(math.log2(RC)) - 1):
            npow = _dot3(npow, npow)
            tinv = tinv + _dot3(npow, tinv)
        wcat = _dot3(tinv, jnp.concatenate([at_h, _dot3(a_ak, v_h)], axis=1))
        ry = _dot3(a_rb, wcat)
        wa_s[h] = wcat[:, :HEAD_DIM]
        uv_s[h] = wcat[:, HEAD_DIM:]
        rq_s[h] = rt_h + ry[:, :HEAD_DIM]
        yv_s[h] = ry[:, HEAD_DIM:] + _dot3(a_rk, v_h)
        bt_s[h] = bt[:, hs]
        kt_s[h] = kt[:, hs]
        v_s[h] = v_h
        gc_s[h] = g_all[:, hs]

    def chunk_body(c, _):
        c0 = pl.multiple_of(c * RC, RC)
        sl = pl.ds(c0, RC)
        for h in range(N_HEADS):
            s_old = s0_ref[c, h] if not chain else s_ref[h]
            res = _dot_nt(jnp.concatenate([wa_s[h, sl, :], rq_s[h, sl, :]], axis=0).astype(BF16),
                          s_old.astype(BF16))
            u = res[:RC] + uv_s[h, sl, :]
            y_s[h, sl, :] = res[RC:] + yv_s[h, sl, :]
            upd = lax.dot_general(jnp.concatenate([u, v_s[h, sl, :]], axis=0).astype(BF16),
                                  jnp.concatenate([bt_s[h, sl, :], kt_s[h, sl, :]], axis=0).astype(BF16),
                                  (((0,), (0,)), ((), ())), preferred_element_type=F32)
            s_new = s_old * gc_s[h, pl.ds(c0, 1), :] + upd
            if chain:
                s_ref[h] = s_new
            else:
                sfin_ref[c, h] = s_new
        return 0

    lax.fori_loop(0, RNB, chunk_body, 0)
    if chain:
        sfin_ref[...] = s_ref[...]

    for h in range(N_HEADS):
        hs = slice(h * HEAD_DIM, (h + 1) * HEAD_DIM)
        y = y_s[h]
        mean = jnp.mean(y, axis=-1, keepdims=True)
        var = jnp.mean(jnp.square(y - mean), axis=-1, keepdims=True)
        yn = (y - mean) * lax.rsqrt(var + GN_EPS) * lng_ref[:, hs] + lnb_ref[:, hs]
        bonus = jnp.sum(r[:, hs] * k2[:, hs] * rk_ref[:, hs], axis=-1, keepdims=True) * v[:, hs]
        out_ref[:, hs] = (yn + bonus) * g[:, hs]


def _rwkv_consts():
    idx = np.arange(RBLK)
    same = (idx[:, None] // RC) == (idx[None, :] // RC)
    lt = (same & (idx[None, :] <= idx[:, None])).astype(np.float32)
    bo = same.astype(np.float32)
    lane = np.arange(GROUP_WIDTH)
    bd = ((lane[:, None] // HEAD_DIM) == (lane[None, :] // HEAD_DIM)).astype(np.float32)
    return jnp.asarray(lt, BF16), jnp.asarray(bo, BF16), jnp.asarray(bd, BF16)


def _rwkv(p, lw, chain, pprev=None, s0=None, t_valid=RC):
    rows = p.shape[0]
    nblk = rows // RBLK
    w = GROUP_WIDTH
    vec = lambda a: a.reshape(1, -1)
    consts = [vec(lw['mu_shift']), vec(lw['w0']), lw['w_decay2'].astype(BF16), vec(lw['a0']),
              lw['w_a2'].astype(BF16), lw['w_g2'].astype(BF16), vec(lw['k_k']), vec(lw['k_a']), vec(lw['r_k']),
              vec(lw['lnx_g']), vec(lw['lnx_b'])] + list(_rwkv_consts())
    full = lambda a: pl.BlockSpec(a.shape, lambda i: (0,) * a.ndim)
    blk = pl.BlockSpec((RBLK, RWKV_COLS), lambda i: (i, 0))
    hshape = (N_HEADS, HEAD_DIM, HEAD_DIM)
    if chain:
        ins, in_specs = [p], [blk]
        sfin_spec = pl.BlockSpec(hshape, lambda i: (0, 0, 0))
        sfin_shape = jax.ShapeDtypeStruct(hshape, F32)
    else:
        sspec = pl.BlockSpec((RNB,) + hshape, lambda i: (i, 0, 0, 0))
        ins, in_specs = [p, pprev, s0], [blk, blk, sspec]
        sfin_spec = sspec
        sfin_shape = jax.ShapeDtypeStruct((nblk * RNB,) + hshape, F32)
    hm = lambda: pltpu.VMEM((N_HEADS, RBLK, HEAD_DIM), F32)
    return pl.pallas_call(
        functools.partial(_rwkv_kernel, chain=chain, t_valid=t_valid),
        grid=(nblk,),
        in_specs=in_specs + [full(c) for c in consts],
        out_specs=[pl.BlockSpec((RBLK, w), lambda i: (i, 0)), sfin_spec],
        out_shape=[jax.ShapeDtypeStruct((rows, w), F32), sfin_shape],
        scratch_shapes=[pltpu.VMEM((1, RWKV_COLS), F32), pltpu.VMEM(hshape, F32)] + [hm() for _ in range(9)],
        compiler_params=_cparams(("arbitrary",)),
    )(*ins, *consts)


RT_TM = 256


def _route_kernel(x_ref, ro_ref, ao_ref, wor_ref, woa_ref, g1_ref, sh_ref, sc_ref, n2_ref, wr_ref, br_ref,
                  ltri_ref, x1_ref, h2_ref, idx_ref, gate_ref, rank_ref, cnt_ref, run_ref):
    @pl.when(pl.program_id(0) == 0)
    def _():
        run_ref[...] = jnp.zeros(run_ref.shape, F32)

    tm = x_ref.shape[0]
    mix = _dot(ro_ref[...].astype(BF16), wor_ref[...]) + _dot(ao_ref[...].astype(BF16), woa_ref[...])
    x1 = x_ref[...] + g1_ref[...] * mix
    x1_ref[...] = x1
    y = x1 * lax.rsqrt(jnp.mean(x1 * x1, axis=-1, keepdims=True) + RMS_EPS) * n2_ref[...]
    h2 = (y * (1.0 + sc_ref[...]) + sh_ref[...]).astype(BF16)
    h2_ref[...] = h2
    logits = _dot(h2, wr_ref[...]) + br_ref[...]
    lane = lax.broadcasted_iota(I32, (tm, LANES), 1)
    lane_f = lane.astype(F32)
    lg = logits
    vals, idxs = [], []
    for _ in range(TOP_K):
        m = jnp.max(lg, axis=1, keepdims=True)
        idx = jnp.min(jnp.where(lg == m, lane_f, float(LANES)), axis=1, keepdims=True)
        vals.append(m)
        idxs.append(idx)
        lg = jnp.where(lane_f == idx, -3e38, lg)
    es = [jnp.exp(v - vals[0]) for v in vals]
    den = es[0] + es[1] + es[2] + es[3]
    ohs = [(lane_f == idx).astype(F32) for idx in idxs]
    oh_all = ohs[0] + ohs[1] + ohs[2] + ohs[3]
    base = run_ref[...] + _dot(ltri_ref[...], oh_all.astype(BF16))
    idx_out = jnp.zeros((tm, LANES), F32)
    gate_out = jnp.zeros((tm, LANES), F32)
    rank_out = jnp.zeros((tm, LANES), F32)
    for kk in range(TOP_K):
        rank = jnp.sum(ohs[kk] * base, axis=1, keepdims=True)
        idx_out = jnp.where(lane == kk, idxs[kk], idx_out)
        gate_out = jnp.where(lane == kk, es[kk] / den, gate_out)
        rank_out = jnp.where(lane == kk, rank, rank_out)
    idx_ref[...] = idx_out[:, :TOP_K].astype(I32)
    gate_ref[...] = gate_out[:, :TOP_K]
    rank_ref[...] = rank_out[:, :TOP_K].astype(I32)
    run_ref[...] = run_ref[...] + jnp.sum(oh_all, axis=0, keepdims=True)
    cnt_ref[...] = run_ref[...]


def _out_proj_route(x, ro, ao, w_out, g1, sh2, sc2, norm2_g, w_router, b_router):
    t, d = x.shape
    tm = min(RT_TM, t)
    w = GROUP_WIDTH
    per_row = g1.shape[0] != 1
    wor = w_out[:w].astype(BF16)
    woa = w_out[w:].astype(BF16)
    wr = jnp.pad(w_router, ((0, 0), (0, LANES - N_EXPERTS))).astype(BF16)
    br = jnp.pad(b_router.reshape(1, -1), ((0, 0), (0, LANES - N_EXPERTS)), constant_values=NEG_BIG)
    ltri = jnp.asarray(np.tril(np.ones((tm, tm), np.float32), -1), BF16)
    mod_spec = pl.BlockSpec((tm, d), lambda i: (i, 0)) if per_row else pl.BlockSpec((1, d), lambda i: (0, 0))
    row = lambda n: pl.BlockSpec((tm, n), lambda i: (i, 0))
    full = lambda a: pl.BlockSpec(a.shape, lambda i: (0, 0))
    return pl.pallas_call(
        _route_kernel,
        grid=(t // tm,),
        in_specs=[row(d), row(w), row(w), full(wor), full(woa), mod_spec, mod_spec, mod_spec,
                  pl.BlockSpec((1, d), lambda i: (0, 0)), full(wr), full(br), full(ltri)],
        out_specs=[row(d), row(d), row(TOP_K), row(TOP_K), row(TOP_K), pl.BlockSpec((1, LANES), lambda i: (0, 0))],
        out_shape=[jax.ShapeDtypeStruct((t, d), F32), jax.ShapeDtypeStruct((t, d), BF16),
                   jax.ShapeDtypeStruct((t, TOP_K), I32), jax.ShapeDtypeStruct((t, TOP_K), F32),
                   jax.ShapeDtypeStruct((t, TOP_K), I32), jax.ShapeDtypeStruct((1, LANES), F32)],
        scratch_shapes=[pltpu.VMEM((1, LANES), F32)],
        compiler_params=_cparams(("arbitrary",)),
    )(x, ro, ao, wor, woa, g1, sh2, sc2, norm2_g.reshape(1, d), wr, br, ltri)


EX_TM = 256
BF16_ROW = (SUBLANES, LANES)


def _dispatch_kernel(slot_ref, h_ref, init_ref, out_ref, sem):
    del init_ref
    tm = h_ref.shape[0]

    def body(r, _):
        for kk in range(TOP_K):
            pltpu.make_async_copy(h_ref.at[r], out_ref.at[slot_ref[r * TOP_K + kk]], sem).start()
        return 0

    lax.fori_loop(0, tm, body, 0)
    for _ in range(TOP_K):
        pltpu.make_async_copy(h_ref, out_ref.at[pl.ds(0, tm)], sem).wait()


def _dispatch(h2, slot, rows_sorted):
    t, d = h2.shape
    tm = min(RT_TM, t)
    assert d == SUBLANES * LANES
    h3 = h2.reshape((t,) + BF16_ROW)
    return pl.pallas_call(
        _dispatch_kernel,
        grid=(t // tm,),
        in_specs=[pl.BlockSpec((tm * TOP_K,), lambda i: (i,), memory_space=pltpu.SMEM),
                  pl.BlockSpec((tm,) + BF16_ROW, lambda i: (i, 0, 0)),
                  pl.BlockSpec(memory_space=pl.ANY)],
        out_specs=pl.BlockSpec(memory_space=pl.ANY),
        out_shape=jax.ShapeDtypeStruct(rows_sorted.shape, rows_sorted.dtype),
        scratch_shapes=[pltpu.SemaphoreType.DMA(())],
        input_output_aliases={2: 0},
        compiler_params=_cparams(("arbitrary",)),
    )(slot.reshape(-1), h3, rows_sorted)


def _expert_kernel(be_ref, nv_ref, x_ref, wgu_ref, bgu_ref, wd_ref, bd_ref, o_ref, wgu_bf, wd_bf):
    i = pl.program_id(0)
    changed = jnp.logical_or(i == 0, be_ref[i] != be_ref[jnp.maximum(i - 1, 0)])

    @pl.when(changed)
    def _():
        wgu_bf[...] = wgu_ref[0].astype(BF16)
        wd_bf[...] = wd_ref[0].astype(BF16)

    @pl.when(i < nv_ref[0])
    def _():
        f = wd_bf.shape[0]
        gu = _dot(x_ref[...], wgu_bf[...]) + bgu_ref[0]
        glu = jnp.minimum(gu[:, :f], SWIGLU_LIMIT)
        lin = jnp.clip(gu[:, f:], -SWIGLU_LIMIT, SWIGLU_LIMIT)
        act = glu * jax.nn.sigmoid(SWIGLU_ALPHA * glu) * (lin + 1.0)
        o_ref[...] = _dot(act.astype(BF16), wd_bf[...]) + bd_ref[0]

    @pl.when(i >= nv_ref[0])
    def _():
        o_ref[...] = jnp.zeros(o_ref.shape, F32)


def _experts(rows_sorted, blk_e, n_valid, w_gu, b_gu, w_down, b_down):
    ns, d = rows_sorted.shape
    e, _, f2 = w_gu.shape
    f = f2 // 2
    grid_spec = pltpu.PrefetchScalarGridSpec(
        num_scalar_prefetch=2,
        grid=(ns // EX_TM,),
        in_specs=[pl.BlockSpec((EX_TM, d), lambda i, be, nv: (i, 0)),
                  pl.BlockSpec((1, d, f2), lambda i, be, nv: (be[i], 0, 0)),
                  pl.BlockSpec((1, 1, f2), lambda i, be, nv: (be[i], 0, 0)),
                  pl.BlockSpec((1, f, d), lambda i, be, nv: (be[i], 0, 0)),
                  pl.BlockSpec((1, 1, d), lambda i, be, nv: (be[i], 0, 0))],
        out_specs=pl.BlockSpec((EX_TM, d), lambda i, be, nv: (i, 0)),
        scratch_shapes=[pltpu.VMEM((d, f2), BF16), pltpu.VMEM((f, d), BF16)])
    return pl.pallas_call(
        _expert_kernel,
        grid_spec=grid_spec,
        out_shape=jax.ShapeDtypeStruct((ns, d), F32),
        compiler_params=_cparams(("arbitrary",)),
    )(blk_e, n_valid, rows_sorted, w_gu, b_gu.reshape(e, 1, f2), w_down, b_down.reshape(e, 1, d))


def _combine_kernel(slot_ref, rows_ref, gate_ref, x1_ref, g2_ref, nf_ref, y_ref, buf, sem):
    tm = x1_ref.shape[0]

    def body(r, _):
        for kk in range(TOP_K):
            pltpu.make_async_copy(rows_ref.at[pl.ds(slot_ref[r * TOP_K + kk], 1)],
                                  buf.at[kk, pl.ds(r, 1)], sem).start()
        return 0

    lax.fori_loop(0, tm, body, 0)
    for kk in range(TOP_K):
        pltpu.make_async_copy(rows_ref.at[pl.ds(0, tm)], buf.at[kk], sem).wait()
    gates = gate_ref[...]
    moe = gates[:, 0:1] * buf[0]
    for kk in range(1, TOP_K):
        moe = moe + gates[:, kk:kk + 1] * buf[kk]
    x2 = x1_ref[...] + g2_ref[...] * moe
    y_ref[...] = x2 * lax.rsqrt(jnp.mean(x2 * x2, axis=-1, keepdims=True) + RMS_EPS) * nf_ref[...]


def _combine(out_rows, slot, gates, x1, g2, normf_g):
    t, d = x1.shape
    tm = min(RT_TM, t)
    per_row = g2.shape[0] != 1
    mod_spec = pl.BlockSpec((tm, d), lambda i: (i, 0)) if per_row else pl.BlockSpec((1, d), lambda i: (0, 0))
    return pl.pallas_call(
        _combine_kernel,
        grid=(t // tm,),
        in_specs=[pl.BlockSpec((tm * TOP_K,), lambda i: (i,), memory_space=pltpu.SMEM),
                  pl.BlockSpec(memory_space=pl.ANY),
                  pl.BlockSpec((tm, TOP_K), lambda i: (i, 0)),
                  pl.BlockSpec((tm, d), lambda i: (i, 0)), mod_spec,
                  pl.BlockSpec((1, d), lambda i: (0, 0))],
        out_specs=pl.BlockSpec((tm, d), lambda i: (i, 0)),
        out_shape=jax.ShapeDtypeStruct((t, d), F32),
        scratch_shapes=[pltpu.VMEM((TOP_K, tm, d), F32), pltpu.SemaphoreType.DMA(())],
        compiler_params=_cparams(("arbitrary",)),
    )(slot.reshape(-1), out_rows, gates, x1, g2, normf_g.reshape(1, d))


def _moe_plan(idx_p, rank_p, cnt_p, idx_s, rank_s, cnt_s):
    n_assign = idx_p.size + idx_s.size
    n_tiles = -(-n_assign // EX_TM) + N_EXPERTS
    cp = cnt_p[0, :N_EXPERTS].astype(I32)
    cs = cnt_s[0, :N_EXPERTS].astype(I32)
    padded = (cp + cs + EX_TM - 1) // EX_TM * EX_TM
    pad_end = jnp.cumsum(padded)
    pad_start = pad_end - padded
    slot_p = pad_start[idx_p] + rank_p
    slot_s = pad_start[idx_s] + cp[idx_s] + rank_s
    blk_e = jnp.minimum(jnp.searchsorted(pad_end, jnp.arange(n_tiles, dtype=I32) * EX_TM, side='right'),
                        N_EXPERTS - 1).astype(I32)
    n_valid = (pad_end[-1:] // EX_TM).astype(I32)
    return slot_p, slot_s, blk_e, n_valid, n_tiles * EX_TM


PG = 8
QROWS = N_HEADS * SUBLANES


def _page_specs(block, n_pages):
    def spec(u):
        return pl.BlockSpec(block, lambda b, j, pt: (pt[b * n_pages + j * PG + u],) + (0,) * (len(block) - 1))
    return [spec(u) for u in range(PG)]


def _head_sum(x):
    out = x[:SUBLANES]
    for h in range(1, N_HEADS):
        out = out + x[h * SUBLANES:(h + 1) * SUBLANES]
    return out


def _sample_index_kernel(pt_ref, qi_ref, wrep_ref, kin_ref, *rest, n_pages, t_new, topk):
    del pt_ref
    pages = rest[:PG]
    keys_ref, tau_ref = rest[PG:]
    j = pl.program_id(1)
    qi = qi_ref[0]
    wrep = wrep_ref[0]

    def scores(ki_bf):
        s = _dot_nt(qi, ki_bf)
        return _head_sum(jnp.maximum(s, 0.0) * wrep)

    for u in range(PG):
        keys_ref[0, j * PG + u] = _float_key(scores(pages[u][0].astype(BF16)))

    @pl.when(j == pl.num_programs(1) - 1)
    def _():
        qrow = lax.broadcasted_iota(I32, (SUBLANES, PAGE_SIZE), 0)
        col = lax.broadcasted_iota(I32, (SUBLANES, PAGE_SIZE), 1)
        ok = jnp.logical_and(col <= qrow, col < t_new)
        keys_ref[0, n_pages] = jnp.where(ok, _float_key(scores(kin_ref[0])), KEY_NEG_INF)

        def count_ge(mid):
            def cbody(t, c):
                return c + jnp.where(keys_ref[0, t] >= mid, 1.0, 0.0)

            c = lax.fori_loop(0, n_pages + 1, cbody, jnp.zeros((SUBLANES, PAGE_SIZE), F32))
            return jnp.broadcast_to(jnp.sum(c, axis=1, keepdims=True), (SUBLANES, LANES))

        tau_ref[0] = _kth_largest_key(count_ge, jnp.full((SUBLANES, LANES), KEY_NEG_INF + 1, I32),
                                      jnp.full((SUBLANES, LANES), 0x7F800000, I32), topk)


def _sample_attend_kernel(pt_ref, q_ref, keys_ref, tau_ref, band_ref, kn_ref, vn_ref, *rest, n_pages):
    del pt_ref
    kpages = rest[:PG]
    vpages = rest[PG:2 * PG]
    o_ref, m_ref, l_ref, acc_ref = rest[2 * PG:]
    j = pl.program_id(1)
    last = j == pl.num_programs(1) - 1

    @pl.when(j == 0)
    def _():
        m_ref[...] = jnp.full(m_ref.shape, NEG_BIG, F32)
        l_ref[...] = jnp.zeros(l_ref.shape, F32)
        acc_ref[...] = jnp.zeros(acc_ref.shape, F32)

    q = q_ref[0]
    tau = _rep(tau_ref[0], N_HEADS, axis=0)
    width = acc_ref.shape[1]

    def attend(k_bf, v_bf, key_tile, bias):
        s = _dot_nt(q, k_bf)
        if bias is not None:
            s = s + bias
        s = jnp.where(_rep(key_tile, N_HEADS, axis=0) >= tau, s, NEG_BIG)
        m_old = m_ref[...]
        m_new = jnp.maximum(m_old, jnp.broadcast_to(jnp.max(s, axis=1, keepdims=True), m_old.shape))
        p = jnp.exp(s - m_new)
        alpha = jnp.exp(m_old - m_new)
        l_ref[...] = alpha * l_ref[...] + jnp.broadcast_to(jnp.sum(p, axis=1, keepdims=True), m_old.shape)
        m_ref[...] = m_new
        acc_ref[...] = acc_ref[...] * _rep(alpha, width // LANES, axis=1) + _dot(p.astype(BF16), v_bf)

    for u in range(PG):
        bias = None
        if u == PG - 1:
            bias = jnp.where(last, band_ref[:, :PAGE_SIZE], 0.0)
        attend(kpages[u][0].astype(BF16), vpages[u][0].astype(BF16), keys_ref[0, j * PG + u], bias)

    @pl.when(last)
    def _():
        attend(kn_ref[0], vn_ref[0], keys_ref[0, n_pages], band_ref[:, PAGE_SIZE:])
        accn = acc_ref[...] / _rep(l_ref[...], width // LANES, axis=1)
        lane_head = lax.broadcasted_iota(I32, (SUBLANES, width), 1) // HEAD_DIM
        out = jnp.zeros((SUBLANES, width), F32)
        for h in range(N_HEADS):
            out = out + jnp.where(lane_head == h, accn[h * SUBLANES:(h + 1) * SUBLANES], 0.0)
        o_ref[0] = out


def _sample_attention(qs, qib, wi, k_new_bf, v_new_bf, ki_new_bf, cache_k, cache_v, cache_kidx, page_table,
                      rel_bias, b, t_new):
    w = GROUP_WIDTH
    n_pages = page_table.shape[1]
    assert n_pages % PG == 0 and t_new <= SUBLANES and cache_k.shape[1] == PAGE_SIZE
    past = n_pages * PAGE_SIZE
    topk = min(TOPK_MAX, (past + t_new) // 4)
    pt = page_table.reshape(-1)
    padq = lambda a: jnp.pad(a, ((0, 0), (0, SUBLANES - t_new)) + ((0, 0),) * (a.ndim - 2))
    padk = lambda a: jnp.pad(a.reshape(b, t_new, -1), ((0, 0), (0, PAGE_SIZE - t_new), (0, 0)))
    qi_r = padq(qib.reshape(b, t_new, N_IDX_HEADS, IDX_DIM)).transpose(0, 2, 1, 3).reshape(b, QROWS, IDX_DIM)
    w_r = padq(wi.reshape(b, t_new, N_IDX_HEADS)).transpose(0, 2, 1).reshape(b, QROWS, 1)
    w_r = jnp.broadcast_to(w_r, (b, QROWS, LANES))
    q4 = padq(qs.reshape(b, t_new, N_HEADS, HEAD_DIM)).transpose(0, 2, 1, 3)
    q_bd = (q4[:, :, :, None, :] * jnp.eye(N_HEADS, dtype=qs.dtype)[None, :, None, :, None]).reshape(b, QROWS, w)
    kin, kn, vn = padk(ki_new_bf), padk(k_new_bf), padk(v_new_bf)
    ck = cache_k.reshape(cache_k.shape[0], PAGE_SIZE, w)
    cv = cache_v.reshape(cache_v.shape[0], PAGE_SIZE, w)
    band = _bias_band(rel_bias, SUBLANES, 2 * PAGE_SIZE, 1, PAGE_SIZE).reshape(QROWS, 2 * PAGE_SIZE)

    per_b = lambda shape: pl.BlockSpec((1,) + shape, lambda bb, j, p_: (bb,) + (0,) * len(shape))
    steps = n_pages // PG
    keys, tau = pl.pallas_call(
        functools.partial(_sample_index_kernel, n_pages=n_pages, t_new=t_new, topk=topk),
        grid_spec=pltpu.PrefetchScalarGridSpec(
            num_scalar_prefetch=1, grid=(b, steps),
            in_specs=[per_b((QROWS, IDX_DIM)), per_b((QROWS, LANES)), per_b((PAGE_SIZE, IDX_DIM))]
            + _page_specs((1, PAGE_SIZE, IDX_DIM), n_pages),
            out_specs=[per_b((n_pages + 1, SUBLANES, PAGE_SIZE)), per_b((SUBLANES, LANES))]),
        out_shape=[jax.ShapeDtypeStruct((b, n_pages + 1, SUBLANES, PAGE_SIZE), I32),
                   jax.ShapeDtypeStruct((b, SUBLANES, LANES), I32)],
        compiler_params=_cparams(("arbitrary", "arbitrary")),
    )(pt, qi_r, w_r, kin, *([cache_kidx] * PG))
    out = pl.pallas_call(
        functools.partial(_sample_attend_kernel, n_pages=n_pages),
        grid_spec=pltpu.PrefetchScalarGridSpec(
            num_scalar_prefetch=1, grid=(b, steps),
            in_specs=[per_b((QROWS, w)), per_b((n_pages + 1, SUBLANES, PAGE_SIZE)), per_b((SUBLANES, LANES)),
                      pl.BlockSpec(band.shape, lambda bb, j, p_: (0, 0)), per_b((PAGE_SIZE, w)),
                      per_b((PAGE_SIZE, w))]
            + _page_specs((1, PAGE_SIZE, w), n_pages) + _page_specs((1, PAGE_SIZE, w), n_pages),
            out_specs=per_b((SUBLANES, w)),
            scratch_shapes=[pltpu.VMEM((QROWS, LANES), F32), pltpu.VMEM((QROWS, LANES), F32),
                            pltpu.VMEM((QROWS, w), F32)]),
        out_shape=jax.ShapeDtypeStruct((b, SUBLANES, w), F32),
        compiler_params=_cparams(("arbitrary", "arbitrary")),
    )(pt, q_bd, keys, tau, band, kn, vn, *([ck] * PG), *([cv] * PG))
    return out[:, :t_new].reshape(b * t_new, w)


def kernel(x_prompt, x_sample, c_prompt, c_sample, cache_k, cache_v, cache_kidx, page_table, state_wkv,
           state_shift, w_ada, b_ada, norm1_g, w_in, mu_shift, w0, w_decay2, a0, w_a2, w_g2, k_k, k_a, r_k,
           lnx_g, lnx_b, rel_bias, w_out, norm2_g, w_router, b_router, w_gu, b_gu, w_down, b_down, normf_g):
    depth = w_in.shape[0]
    assert depth == 1, "the merged prompt+sample expert pass is written for a single layer"
    bp, sp, d = x_prompt.shape
    bs, ts, _ = x_sample.shape
    assert bp == 1 and sp % RBLK == 0 and bs % RNB == 0 and ts <= RC
    l = 0
    lw = {'mu_shift': mu_shift[l], 'w0': w0[l], 'w_decay2': w_decay2[l], 'a0': a0[l], 'w_a2': w_a2[l],
          'w_g2': w_g2[l], 'k_k': k_k[l], 'k_a': k_a[l], 'r_k': r_k[l], 'lnx_g': lnx_g[l], 'lnx_b': lnx_b[l]}
    ns = bs * ts

    c_all = jnp.concatenate([c_prompt, c_sample], axis=0)
    n_c = c_all.shape[0]
    c_all = jnp.pad(c_all, ((0, -n_c % SUBLANES), (0, 0)))
    mod = _adaln(c_all, w_ada[l], b_ada[l])
    mods_p = [mod[0:1, i * d:(i + 1) * d] for i in range(6)]
    mods_s = [jnp.repeat(mod[bp:bp + bs, i * d:(i + 1) * d], ts, axis=0) for i in range(6)]

    xp = x_prompt.reshape(sp, d)
    xs = x_sample.reshape(ns, d)
    pr_p, k_p, v_p, ki_p, wi_p, qs_p, kb_p, _, qib_p, kib_p, vt_p = _in_proj(xp, norm1_g[l], mods_p[0],
                                                                             mods_p[1], w_in[l])
    pr_s, k_s, v_s, ki_s, wi_s, qs_s, kb_s, vb_s, qib_s, kib_s, _ = _in_proj(xs, norm1_g[l], mods_s[0],
                                                                             mods_s[1], w_in[l])

    ro_p, wkv_p = _rwkv(pr_p, lw, chain=True)
    att_p = _prompt_attention(qs_p, qib_p, wi_p, kib_p, kb_p, vt_p, rel_bias)

    pr_s3 = pr_s.reshape(bs, ts, RWKV_COLS)
    prev_s3 = jnp.concatenate([state_shift[l][:, None, :], pr_s3[:, :-1]], axis=1)
    padc = lambda a: jnp.pad(a, ((0, 0), (0, RC - ts), (0, 0))).reshape(bs * RC, RWKV_COLS)
    ro_s, wkv_s = _rwkv(padc(pr_s3), lw, chain=False, pprev=padc(prev_s3), s0=state_wkv[l], t_valid=ts)
    ro_s = ro_s.reshape(bs, RC, GROUP_WIDTH)[:, :ts].reshape(ns, GROUP_WIDTH)
    att_s = _sample_attention(qs_s, qib_s, wi_s, kb_s, vb_s, kib_s, cache_k[l], cache_v[l], cache_kidx[l],
                              page_table, rel_bias, bs, ts)

    x1_p, h2_p, idx_p, gate_p, rank_p, cnt_p = _out_proj_route(xp, ro_p, att_p, w_out[l], mods_p[2], mods_p[3],
                                                               mods_p[4], norm2_g[l], w_router[l], b_router[l])
    x1_s, h2_s, idx_s, gate_s, rank_s, cnt_s = _out_proj_route(xs, ro_s, att_s, w_out[l], mods_s[2], mods_s[3],
                                                               mods_s[4], norm2_g[l], w_router[l], b_router[l])
    slot_p, slot_s, blk_e, n_valid, n_slots = _moe_plan(idx_p, rank_p, cnt_p, idx_s, rank_s, cnt_s)
    rows = jnp.zeros((n_slots,) + BF16_ROW, BF16)
    rows = _dispatch(h2_p, slot_p, rows)
    rows = _dispatch(h2_s, slot_s, rows)
    out_rows = _experts(rows.reshape(n_slots, d), blk_e, n_valid, w_gu[l], b_gu[l], w_down[l], b_down[l])
    y_p = _combine(out_rows, slot_p, gate_p, x1_p, mods_p[5], normf_g)
    y_s = _combine(out_rows, slot_s, gate_s, x1_s, mods_s[5], normf_g)

    hd = (N_HEADS, HEAD_DIM)
    return (y_p.reshape(bp, sp, d), y_s.reshape(bs, ts, d),
            k_p.reshape((1, bp, sp) + hd), v_p.reshape((1, bp, sp) + hd), ki_p.reshape(1, bp, sp, IDX_DIM),
            wkv_p.reshape((1, bp) + (N_HEADS, HEAD_DIM, HEAD_DIM)), pr_p[sp - 1:sp].reshape(1, bp, RWKV_COLS),
            k_s.reshape((1, bs, ts) + hd), v_s.reshape((1, bs, ts) + hd), ki_s.reshape(1, bs, ts, IDX_DIM),
            wkv_s.reshape((1, bs) + (N_HEADS, HEAD_DIM, HEAD_DIM)), pr_s3[:, ts - 1].reshape(1, bs, RWKV_COLS))
```

```python
import functools
import math

import jax
import jax.numpy as jnp
import numpy as np
from jax import lax
from jax.experimental import pallas as pl
from jax.experimental.pallas import tpu as pltpu

F32 = jnp.float32
BF16 = jnp.bfloat16
I32 = jnp.int32

HEAD_DIM = 64
N_HEADS = 8
GROUP_WIDTH = N_HEADS * HEAD_DIM
DECAY_LORA, AAA_LORA, GATE_LORA = 64, 64, 128
RWKV_COLS = 3 * GROUP_WIDTH + DECAY_LORA + AAA_LORA + GATE_LORA
IDX_DIM = 64
N_IDX_HEADS = 8
TOPK_MAX = 256
N_BUCKETS = 32
MAX_DISTANCE = 128
N_EXPERTS = 32
TOP_K = 4
SWIGLU_LIMIT = 7.0
SWIGLU_ALPHA = 1.702
RMS_EPS = 1e-6
GN_EPS = HEAD_DIM * 1e-5
PAGE_SIZE = 128

LANES = 128
SUBLANES = 8
VMEM_LIMIT = 56 * 1024 * 1024

NEG_BIG = -1e30
KEY_NEG_INF = -2139095041


def _cparams(sem):
    return pltpu.CompilerParams(dimension_semantics=sem, vmem_limit_bytes=VMEM_LIMIT)


def _dot(a, b):
    return jnp.dot(a, b, preferred_element_type=F32)


def _dot_nt(a, b):
    return lax.dot_general(a, b, (((1,), (1,)), ((), ())), preferred_element_type=F32)


def _split2(a):
    hi = a.astype(BF16)
    lo = (a - hi.astype(F32)).astype(BF16)
    return hi, lo


def _split3(a):
    hi = a.astype(BF16)
    r1 = a - hi.astype(F32)
    mid = r1.astype(BF16)
    lo = (r1 - mid.astype(F32)).astype(BF16)
    return hi, mid, lo


def _rep(x, n, axis):
    return jnp.concatenate([x] * n, axis=axis)


def _float_key(x):
    b = pltpu.bitcast(x, I32)
    return b ^ ((b >> 31) & 0x7FFFFFFF)


def _ada_kernel(c_ref, w_ref, b_ref, o_ref):
    c = c_ref[...]
    s = c * jax.nn.sigmoid(c)
    o_ref[...] = _dot(s.astype(BF16), w_ref[...].astype(BF16)) + b_ref[...]


def _adaln(c, w_ada, b_ada):
    r, d = c.shape
    n = w_ada.shape[1]
    tn = 1536
    return pl.pallas_call(
        _ada_kernel,
        grid=(n // tn,),
        in_specs=[pl.BlockSpec((r, d), lambda j: (0, 0)),
                  pl.BlockSpec((d, tn), lambda j: (0, j)),
                  pl.BlockSpec((1, tn), lambda j: (0, j))],
        out_specs=pl.BlockSpec((r, tn), lambda j: (0, j)),
        out_shape=jax.ShapeDtypeStruct((r, n), F32),
        compiler_params=_cparams(("arbitrary",)),
    )(c, w_ada, b_ada.reshape(1, n))


def _inproj_kernel(x_ref, g_ref, sh_ref, sc_ref, wr_ref, wa_ref, wk_ref,
                   pr_ref, k_ref, v_ref, ki_ref, wi_ref, qs_ref, kb_ref, vb_ref, qib_ref, kib_ref, vt_ref):
    x = x_ref[...]
    y = x * lax.rsqrt(jnp.mean(x * x, axis=-1, keepdims=True) + RMS_EPS) * g_ref[...]
    h = (y * (1.0 + sc_ref[...]) + sh_ref[...]).astype(BF16)
    pr_ref[...] = _dot(h, wr_ref[...])
    a = _dot(h, wa_ref[...])
    w = GROUP_WIDTH
    q, k, v, qi = a[:, :w], a[:, w:2 * w], a[:, 2 * w:3 * w], a[:, 3 * w:4 * w]
    k_ref[...] = k
    v_ref[...] = v
    qs_ref[...] = (q * HEAD_DIM ** -0.5).astype(BF16)
    kb_ref[...] = k.astype(BF16)
    vb_ref[...] = v.astype(BF16)
    vt_ref[...] = v.T.astype(BF16)
    qib_ref[...] = (qi * IDX_DIM ** -0.5).astype(BF16)
    kw = _dot(h, wk_ref[...])
    ki = kw[:, :IDX_DIM]
    ki_ref[...] = ki
    kib_ref[...] = ki.astype(BF16)
    wi_ref[...] = kw[:, IDX_DIM:IDX_DIM + N_IDX_HEADS] * N_IDX_HEADS ** -0.5


def _in_proj(x, norm_g, shift, scale, w_in):
    t, d = x.shape
    tm = min(512, t)
    per_row = shift.shape[0] != 1
    w = GROUP_WIDTH
    a0 = RWKV_COLS
    wr = w_in[:, :a0].astype(BF16)
    wa = w_in[:, a0:a0 + 4 * w].astype(BF16)
    wk = jnp.pad(w_in[:, a0 + 4 * w:], ((0, 0), (0, LANES - IDX_DIM - N_IDX_HEADS))).astype(BF16)
    mod_spec = pl.BlockSpec((tm, d), lambda i: (i, 0)) if per_row else pl.BlockSpec((1, d), lambda i: (0, 0))
    row = lambda n: pl.BlockSpec((tm, n), lambda i: (i, 0))
    full = lambda a: pl.BlockSpec(a.shape, lambda i: (0, 0))
    sds = lambda n, dt: jax.ShapeDtypeStruct((t, n), dt)
    return pl.pallas_call(
        _inproj_kernel,
        grid=(t // tm,),
        in_specs=[row(d), pl.BlockSpec((1, d), lambda i: (0, 0)), mod_spec, mod_spec, full(wr), full(wa), full(wk)],
        out_specs=[row(a0), row(w), row(w), row(IDX_DIM), row(N_IDX_HEADS), row(w), row(w), row(w), row(w),
                   row(IDX_DIM), pl.BlockSpec((w, tm), lambda i: (0, i))],
        out_shape=[sds(a0, F32), sds(w, F32), sds(w, F32), sds(IDX_DIM, F32), sds(N_IDX_HEADS, F32),
                   sds(w, BF16), sds(w, BF16), sds(w, BF16), sds(w, BF16), sds(IDX_DIM, BF16),
                   jax.ShapeDtypeStruct((w, t), BF16)],
        compiler_params=_cparams(("arbitrary",)),
    )(x, norm_g.reshape(1, d), shift, scale, wr, wa, wk)


def _bias_band_kernel(rb_ref, o_ref, *, sign, off):
    _, rows, cols = o_ref.shape
    r = lax.broadcasted_iota(I32, (rows, cols), 0)
    c = lax.broadcasted_iota(I32, (rows, cols), 1)
    n = jnp.maximum(sign * (r - c) + off, 0)
    max_exact = N_BUCKETS // 2
    nf = jnp.maximum(n, 1).astype(F32)
    large = max_exact + (jnp.log(nf / max_exact) / math.log(MAX_DISTANCE / max_exact)
                         * (N_BUCKETS - max_exact)).astype(I32)
    large = jnp.minimum(large, N_BUCKETS - 1)
    bucket = jnp.where(n < max_exact, n, large)
    for h in range(N_HEADS):
        far = rb_ref[N_BUCKETS - 1, h]
        acc = jnp.zeros((rows, cols), F32)
        for b in range(N_BUCKETS - 1):
            acc = jnp.where(bucket == b, rb_ref[b, h] - far, acc)
        o_ref[h] = acc


def _bias_band(rel_bias, rows, cols, sign, off):
    return pl.pallas_call(
        functools.partial(_bias_band_kernel, sign=sign, off=off),
        in_specs=[pl.BlockSpec(memory_space=pltpu.SMEM)],
        out_specs=pl.BlockSpec((N_HEADS, rows, cols), lambda: (0, 0, 0)),
        out_shape=jax.ShapeDtypeStruct((N_HEADS, rows, cols), F32),
    )(rel_bias)


def _kth_largest_key(count_ge, lo0, hi0, topk):
    def cond(carry):
        return carry[2] > 0

    def body(carry):
        lo, hi, _ = carry
        mid = (lo | hi) - ((lo ^ hi) >> 1)
        cnt = count_ge(mid)
        active = lo < hi
        exact = jnp.logical_and(active, cnt == topk)
        ge = cnt >= topk
        lo_n = jnp.where(active, jnp.where(ge, mid, lo), lo)
        hi_n = jnp.where(active, jnp.where(exact, mid, jnp.where(ge, hi, mid - 1)), hi)
        return lo_n, hi_n, jnp.max(jnp.where(lo_n < hi_n, 1.0, 0.0))

    lo, _, _ = lax.while_loop(cond, body, (lo0, hi0, jnp.max(jnp.where(lo0 < hi0, 1.0, 0.0))))
    return lo


PQB = 256
PKT = 512
PKB = 1024
PSUB = 256


def _prompt_att_kernel(qs_ref, qib_ref, wit_ref, kib_ref, kb_ref, vt_ref, band_ref, o_ref,
                       keys_ref, tau_ref, m_ref, l_ref, acc_ref, s_ref, p_ref, a_ref, *, topk, qb_rows):
    qb = pl.program_id(0)
    j = pl.program_id(1)
    q_lo = qb * qb_rows
    n_kt = (q_lo + qb_rows + PKT - 1) // PKT

    @pl.when(j == 0)
    def _index_phase():
        q_pos = q_lo + lax.broadcasted_iota(I32, (PKT, qb_rows), 1)

        def tile_body(kt, carry):
            k0 = pl.multiple_of(kt * PKT, PKT)
            ki = kib_ref[pl.ds(k0, PKT), :]
            acc = jnp.zeros((PKT, qb_rows), F32)
            for h in range(N_IDX_HEADS):
                s = _dot_nt(ki, qib_ref[:, h * IDX_DIM:(h + 1) * IDX_DIM])
                acc = acc + jnp.maximum(s, 0.0) * wit_ref[h:h + 1, :]
            k_pos = k0 + lax.broadcasted_iota(I32, (PKT, qb_rows), 0)
            causal = k_pos <= q_pos
            keys_ref[kt] = jnp.where(causal, _float_key(acc), KEY_NEG_INF)
            smax, smin = carry
            grp = lambda x: x.reshape(PKT // SUBLANES, SUBLANES, qb_rows)
            smax = jnp.maximum(smax, jnp.max(grp(jnp.where(causal, acc, -jnp.inf)), axis=0))
            smin = jnp.minimum(smin, jnp.min(grp(jnp.where(causal, acc, jnp.inf)), axis=0))
            return smax, smin

        smax, smin = lax.fori_loop(
            0, n_kt, tile_body,
            (jnp.full((SUBLANES, qb_rows), -jnp.inf, F32), jnp.full((SUBLANES, qb_rows), jnp.inf, F32)))
        fmax = jnp.max(smax, axis=0, keepdims=True)
        fmin = jnp.min(smin, axis=0, keepdims=True)
        hi0 = jnp.where(fmax == 0.0, 0, _float_key(fmax))
        lo0 = jnp.where(fmin == 0.0, -1, _float_key(fmin))
        n_causal = q_lo + 1 + lax.broadcasted_iota(I32, (1, qb_rows), 1)
        lo0 = jnp.where(n_causal < topk, KEY_NEG_INF + 1, lo0)
        hi0 = jnp.where(n_causal < topk, KEY_NEG_INF + 1, hi0)

        def count_ge(mid):
            def cbody(kt, c):
                ge = jnp.where(keys_ref[kt] >= mid, 1.0, 0.0)
                return c + jnp.sum(ge.reshape(PKT // SUBLANES, SUBLANES, qb_rows), axis=0)

            c = lax.fori_loop(0, n_kt, cbody, jnp.zeros((SUBLANES, qb_rows), F32))
            return jnp.sum(c, axis=0, keepdims=True)

        tau_ref[...] = _kth_largest_key(count_ge, lo0, hi0, topk)
        m_ref[...] = jnp.full(m_ref.shape, NEG_BIG, F32)
        l_ref[...] = jnp.zeros(l_ref.shape, F32)
        acc_ref[...] = jnp.zeros(acc_ref.shape, F32)

    kb = j - 1
    last_kb = (q_lo + qb_rows - 1) // PKB

    @pl.when(jnp.logical_and(j >= 1, kb <= last_kb))
    def _attend_phase():
        tau = tau_ref[...]
        for u in range(PKB // PSUB):
            s0 = kb * PKB + u * PSUB

            def sub_tile(near, u=u, s0=s0):
                kt = (kb * PKB + u * PSUB) // PKT
                c0 = (u * PSUB) % PKT
                sel = keys_ref[kt, c0:c0 + PSUB, :] >= tau
                if near:
                    which = jnp.clip((q_lo - s0) // PSUB, 0, 1)
                for h in range(N_HEADS):
                    hs = slice(h * HEAD_DIM, (h + 1) * HEAD_DIM)
                    s_ref[h] = _dot_nt(kb_ref[u * PSUB:(u + 1) * PSUB, hs], qs_ref[:, hs])
                for h in range(N_HEADS):
                    s = s_ref[h]
                    if near:
                        s = s + band_ref[h, 1 - which]
                    s = jnp.where(sel, s, NEG_BIG)
                    m_old = m_ref[h]
                    m_new = jnp.maximum(m_old, jnp.max(s, axis=0, keepdims=True))
                    p = jnp.exp(s - m_new)
                    alpha = jnp.exp(m_old - m_new)
                    l_ref[h] = alpha * l_ref[h] + jnp.sum(p, axis=0, keepdims=True)
                    m_ref[h] = m_new
                    a_ref[h] = alpha
                    p_ref[h] = p.astype(BF16)
                for h in range(N_HEADS):
                    hs = slice(h * HEAD_DIM, (h + 1) * HEAD_DIM)
                    pv = _dot(vt_ref[hs, u * PSUB:(u + 1) * PSUB], p_ref[h])
                    acc_ref[h] = acc_ref[h] * a_ref[h] + pv

            is_near = s0 + PSUB > q_lo - PSUB
            in_range = s0 < q_lo + qb_rows

            @pl.when(jnp.logical_and(in_range, is_near))
            def _():
                sub_tile(True)

            @pl.when(jnp.logical_and(in_range, jnp.logical_not(is_near)))
            def _():
                sub_tile(False)

    @pl.when(j == last_kb + 1)
    def _finish():
        for h in range(N_HEADS):
            hs = slice(h * HEAD_DIM, (h + 1) * HEAD_DIM)
            o_ref[:, hs] = (acc_ref[h] / l_ref[h]).T


def _prompt_attention(qs, qib, wi, kib, kb, vt, rel_bias):
    s, w = qs.shape
    topk = min(TOPK_MAX, s // 4)
    qb_rows = min(PQB, s)
    assert s % qb_rows == 0 and s % PKB == 0 and qb_rows == PSUB
    nqb = s // qb_rows
    nkb = s // PKB
    band = _bias_band(rel_bias, 2 * PSUB, qb_rows, -1, PSUB).reshape(N_HEADS, 2, PSUB, qb_rows)

    def last_kb(i):
        return (i * qb_rows + qb_rows - 1) // PKB

    qrow = lambda n: pl.BlockSpec((qb_rows, n), lambda i, j: (i, 0))
    return pl.pallas_call(
        functools.partial(_prompt_att_kernel, topk=topk, qb_rows=qb_rows),
        grid=(nqb, nkb + 1),
        in_specs=[qrow(w), qrow(w), pl.BlockSpec((N_IDX_HEADS, qb_rows), lambda i, j: (0, i)),
                  pl.BlockSpec((s, IDX_DIM), lambda i, j: (0, 0)),
                  pl.BlockSpec((PKB, w), lambda i, j: (jnp.minimum(jnp.maximum(j - 1, 0), last_kb(i)), 0)),
                  pl.BlockSpec((w, PKB), lambda i, j: (0, jnp.minimum(jnp.maximum(j - 1, 0), last_kb(i)))),
                  pl.BlockSpec(band.shape, lambda i, j: (0, 0, 0, 0))],
        out_specs=qrow(w),
        out_shape=jax.ShapeDtypeStruct((s, w), F32),
        scratch_shapes=[pltpu.VMEM((s // PKT, PKT, qb_rows), I32),
                        pltpu.VMEM((1, qb_rows), I32),
                        pltpu.VMEM((N_HEADS, 1, qb_rows), F32),
                        pltpu.VMEM((N_HEADS, 1, qb_rows), F32),
                        pltpu.VMEM((N_HEADS, HEAD_DIM, qb_rows), F32),
                        pltpu.VMEM((N_HEADS, PSUB, qb_rows), F32),
                        pltpu.VMEM((N_HEADS, PSUB, qb_rows), BF16),
                        pltpu.VMEM((N_HEADS, 1, qb_rows), F32)],
        compiler_params=_cparams(("arbitrary", "arbitrary")),
    )(qs, qib, wi.T, kib, kb, vt, band)


RC = 16
RNB = 16
RBLK = RC * RNB


def _dot3(a, b):
    ah, al = _split2(a)
    bh, bl = _split2(b)
    return _dot(ah, bh) + (_dot(ah, bl) + _dot(al, bh))


def _dot3_nt(a, b):
    ah, al = _split2(a)
    bh, bl = _split2(b)
    return _dot_nt(ah, bh) + (_dot_nt(ah, bl) + _dot_nt(al, bh))


def _dot3_tn(a, b):
    dn = (((0,), (0,)), ((), ()))
    ah, al = _split2(a)
    bh, bl = _split2(b)
    f = lambda x, y: lax.dot_general(x, y, dn, preferred_element_type=F32)
    return f(ah, bh) + (f(ah, bl) + f(al, bh))


def _dot_exact_lhs(m_bf, x):
    hi, mid, lo = _split3(x)
    return _dot(m_bf, hi) + (_dot(m_bf, mid) + _dot(m_bf, lo))


def _dot_exact_rhs(x, m_bf):
    hi, mid, lo = _split3(x)
    return _dot(hi, m_bf) + (_dot(mid, m_bf) + _dot(lo, m_bf))


def _rwkv_kernel(*refs, chain, t_valid):
    if chain:
        p_ref, = refs[:1]
        rest = refs[1:]
    else:
        p_ref, pprev_ref, s0_ref = refs[:3]
        rest = refs[3:]
    (mu_ref, w0_ref, wd2_ref, a0_ref, wa2_ref, wg2_ref, kk_ref, ka_ref, rk_ref, lng_ref, lnb_ref,
     lt_ref, bo_ref, bd_ref, out_ref, sfin_ref,
     carry_ref, s_ref, wa_s, rq_s, uv_s, yv_s, bt_s, kt_s, v_s, gc_s, y_s,
     np_s, ti_s, ak_s, rb_s, rk_s) = rest
    i = pl.program_id(0)
    w = GROUP_WIDTH
    p = p_ref[...]
    row = lax.broadcasted_iota(I32, (RBLK, 1), 0)
    if chain:
        @pl.when(i == 0)
        def _():
            carry_ref[...] = jnp.zeros(carry_ref.shape, F32)
            s_ref[...] = jnp.zeros(s_ref.shape, F32)

        pprev = jnp.where(row == 0, carry_ref[...], pltpu.roll(p, 1, axis=0))
        carry_ref[...] = p[RBLK - 1:RBLK, :]
    else:
        pprev = pprev_ref[...]
    ps = p + (pprev - p) * mu_ref[...]
    r, k, v = ps[:, :w], ps[:, w:2 * w], ps[:, 2 * w:3 * w]
    o = 3 * w
    xw = ps[:, o:o + DECAY_LORA]
    xa = ps[:, o + DECAY_LORA:o + DECAY_LORA + AAA_LORA]
    xg = ps[:, o + DECAY_LORA + AAA_LORA:]
    dec = w0_ref[...] + _dot(jnp.tanh(xw).astype(BF16), wd2_ref[...])
    softplus = jnp.maximum(-dec, 0.0) + jnp.log(1.0 + jnp.exp(-jnp.abs(dec)))
    lw = -jnp.exp(-softplus - 0.5)
    a = jax.nn.sigmoid(a0_ref[...] + _dot(xa.astype(BF16), wa2_ref[...]))
    g = _dot(jax.nn.sigmoid(xg).astype(BF16), wg2_ref[...])
    kk = k * kk_ref[...]
    kk = kk * lax.rsqrt(jnp.maximum(_dot_exact_rhs(kk * kk, bd_ref[...]), 1e-24))
    k2 = k * (1.0 + (a - 1.0) * ka_ref[...])
    alpha = -kk
    beta = kk * a
    if t_valid < RC:
        valid = (row % RC) < t_valid
        zero = lambda x: jnp.where(valid, x, 0.0)
        lw, alpha, beta, k2, r, v = zero(lw), zero(alpha), zero(beta), zero(k2), zero(r), zero(v)
    cl = _dot_exact_lhs(lt_ref[...], lw)
    ct = _dot_exact_lhs(bo_ref[...], lw)
    g_in = jnp.exp(cl)
    g_ex = jnp.exp(cl - lw)
    g_inv = jnp.exp(-cl)
    g_end = jnp.exp(ct - cl)
    g_all = jnp.exp(ct)
    at, rt = alpha * g_ex, r * g_in
    bh, kh = beta * g_inv, k2 * g_inv
    bt, kt = beta * g_end, k2 * g_end

    ri = lax.broadcasted_iota(I32, (RBLK, RBLK), 0)
    ci = lax.broadcasted_iota(I32, (RBLK, RBLK), 1)
    same = (ri // RC) == (ci // RC)
    strict = jnp.logical_and(same, ci < ri)
    incl = jnp.logical_and(same, ci <= ri)
    eye = (ri == ci).astype(F32)

    heads = range(N_HEADS)
    hsl = [slice(h * HEAD_DIM, (h + 1) * HEAD_DIM) for h in heads]
    for h in heads:
        hs = hsl[h]
        gm = _dot3_nt(jnp.concatenate([at[:, hs], rt[:, hs]], axis=0),
                      jnp.concatenate([bh[:, hs], kh[:, hs]], axis=0))
        n1 = jnp.where(strict, gm[:RBLK, :RBLK], 0.0)
        np_s[h] = n1
        ti_s[h] = eye + n1
        ak_s[h] = jnp.where(strict, gm[:RBLK, RBLK:], 0.0)
        rb_s[h] = jnp.where(incl, gm[RBLK:, :RBLK], 0.0)
        rk_s[h] = jnp.where(incl, gm[RBLK:, RBLK:], 0.0)
        bt_s[h] = bt[:, hs]
        kt_s[h] = kt[:, hs]
        v_s[h] = v[:, hs]
        gc_s[h] = g_all[:, hs]
    for _ in range(int(math.log2(RC)) - 1):
        for h in heads:
            npow = np_s[h]
            np_s[h] = _dot3(npow, npow)
        for h in heads:
            tinv = ti_s[h]
            ti_s[h] = tinv + _dot3(np_s[h], tinv)
    for h in heads:
        uv_s[h] = _dot3(ak_s[h], v_s[h])
    for h in heads:
        wcat = _dot3(ti_s[h], jnp.concatenate([at[:, hsl[h]], uv_s[h]], axis=1))
        wa_s[h] = wcat[:, :HEAD_DIM]
        uv_s[h] = wcat[:, HEAD_DIM:]
    for h in heads:
        ry = _dot3(rb_s[h], jnp.concatenate([wa_s[h], uv_s[h]], axis=1))
        rq_s[h] = rt[:, hsl[h]] + ry[:, :HEAD_DIM]
        yv_s[h] = ry[:, HEAD_DIM:] + _dot3(rk_s[h], v_s[h])

    def chunk_body(c, _):
        c0 = pl.multiple_of(c * RC, RC)
        sl = pl.ds(c0, RC)
        s_old = [s0_ref[c, h] if not chain else s_ref[h] for h in heads]
        res = [_dot_nt(jnp.concatenate([wa_s[h, sl, :], rq_s[h, sl, :]], axis=0).astype(BF16),
                       s_old[h].astype(BF16)) for h in heads]
        for h in heads:
            y_s[h, sl, :] = res[h][RC:] + yv_s[h, sl, :]
        upd = [lax.dot_general(jnp.concatenate([res[h][:RC] + uv_s[h, sl, :], v_s[h, sl, :]], axis=0).astype(BF16),
                               jnp.concatenate([bt_s[h, sl, :], kt_s[h, sl, :]], axis=0).astype(BF16),
                               (((0,), (0,)), ((), ())), preferred_element_type=F32) for h in heads]
        for h in heads:
            s_new = s_old[h] * gc_s[h, pl.ds(c0, 1), :] + upd[h]
            if chain:
                s_ref[h] = s_new
            else:
                sfin_ref[c, h] = s_new
        return 0

    lax.fori_loop(0, RNB, chunk_body, 0)
    if chain:
        sfin_ref[...] = s_ref[...]

    for h in range(N_HEADS):
        hs = slice(h * HEAD_DIM, (h + 1) * HEAD_DIM)
        y = y_s[h]
        mean = jnp.mean(y, axis=-1, keepdims=True)
        var = jnp.mean(jnp.square(y - mean), axis=-1, keepdims=True)
        yn = (y - mean) * lax.rsqrt(var + GN_EPS) * lng_ref[:, hs] + lnb_ref[:, hs]
        bonus = jnp.sum(r[:, hs] * k2[:, hs] * rk_ref[:, hs], axis=-1, keepdims=True) * v[:, hs]
        out_ref[:, hs] = (yn + bonus) * g[:, hs]


def _rwkv_consts():
    idx = np.arange(RBLK)
    same = (idx[:, None] // RC) == (idx[None, :] // RC)
    lt = (same & (idx[None, :] <= idx[:, None])).astype(np.float32)
    bo = same.astype(np.float32)
    lane = np.arange(GROUP_WIDTH)
    bd = ((lane[:, None] // HEAD_DIM) == (lane[None, :] // HEAD_DIM)).astype(np.float32)
    return jnp.asarray(lt, BF16), jnp.asarray(bo, BF16), jnp.asarray(bd, BF16)


def _rwkv(p, lw, chain, pprev=None, s0=None, t_valid=RC):
    rows = p.shape[0]
    nblk = rows // RBLK
    w = GROUP_WIDTH
    vec = lambda a: a.reshape(1, -1)
    consts = [vec(lw['mu_shift']), vec(lw['w0']), lw['w_decay2'].astype(BF16), vec(lw['a0']),
              lw['w_a2'].astype(BF16), lw['w_g2'].astype(BF16), vec(lw['k_k']), vec(lw['k_a']), vec(lw['r_k']),
              vec(lw['lnx_g']), vec(lw['lnx_b'])] + list(_rwkv_consts())
    full = lambda a: pl.BlockSpec(a.shape, lambda i: (0,) * a.ndim)
    blk = pl.BlockSpec((RBLK, RWKV_COLS), lambda i: (i, 0))
    hshape = (N_HEADS, HEAD_DIM, HEAD_DIM)
    if chain:
        ins, in_specs = [p], [blk]
        sfin_spec = pl.BlockSpec(hshape, lambda i: (0, 0, 0))
        sfin_shape = jax.ShapeDtypeStruct(hshape, F32)
    else:
        sspec = pl.BlockSpec((RNB,) + hshape, lambda i: (i, 0, 0, 0))
        ins, in_specs = [p, pprev, s0], [blk, blk, sspec]
        sfin_spec = sspec
        sfin_shape = jax.ShapeDtypeStruct((nblk * RNB,) + hshape, F32)
    hm = lambda: pltpu.VMEM((N_HEADS, RBLK, HEAD_DIM), F32)
    return pl.pallas_call(
        functools.partial(_rwkv_kernel, chain=chain, t_valid=t_valid),
        grid=(nblk,),
        in_specs=in_specs + [full(c) for c in consts],
        out_specs=[pl.BlockSpec((RBLK, w), lambda i: (i, 0)), sfin_spec],
        out_shape=[jax.ShapeDtypeStruct((rows, w), F32), sfin_shape],
        scratch_shapes=[pltpu.VMEM((1, RWKV_COLS), F32), pltpu.VMEM(hshape, F32)] + [hm() for _ in range(9)]
        + [pltpu.VMEM((N_HEADS, RBLK, RBLK), F32) for _ in range(5)],
        compiler_params=_cparams(("arbitrary",)),
    )(*ins, *consts)


RT_TM = 256


def _route_kernel(x_ref, ro_ref, ao_ref, wor_ref, woa_ref, g1_ref, sh_ref, sc_ref, n2_ref, wr_ref, br_ref,
                  ltri_ref, x1_ref, h2_ref, idx_ref, gate_ref, rank_ref, cnt_ref, run_ref):
    @pl.when(pl.program_id(0) == 0)
    def _():
        run_ref[...] = jnp.zeros(run_ref.shape, F32)

    tm = x_ref.shape[0]
    mix = _dot(ro_ref[...].astype(BF16), wor_ref[...]) + _dot(ao_ref[...].astype(BF16), woa_ref[...])
    x1 = x_ref[...] + g1_ref[...] * mix
    x1_ref[...] = x1
    y = x1 * lax.rsqrt(jnp.mean(x1 * x1, axis=-1, keepdims=True) + RMS_EPS) * n2_ref[...]
    h2 = (y * (1.0 + sc_ref[...]) + sh_ref[...]).astype(BF16)
    h2_ref[...] = h2
    logits = _dot(h2, wr_ref[...]) + br_ref[...]
    lane = lax.broadcasted_iota(I32, (tm, LANES), 1)
    lane_f = lane.astype(F32)
    lg = logits
    vals, idxs = [], []
    for _ in range(TOP_K):
        m = jnp.max(lg, axis=1, keepdims=True)
        idx = jnp.min(jnp.where(lg == m, lane_f, float(LANES)), axis=1, keepdims=True)
        vals.append(m)
        idxs.append(idx)
        lg = jnp.where(lane_f == idx, -3e38, lg)
    es = [jnp.exp(v - vals[0]) for v in vals]
    den = es[0] + es[1] + es[2] + es[3]
    ohs = [(lane_f == idx).astype(F32) for idx in idxs]
    oh_all = ohs[0] + ohs[1] + ohs[2] + ohs[3]
    base = run_ref[...] + _dot(ltri_ref[...], oh_all.astype(BF16))
    idx_out = jnp.zeros((tm, LANES), F32)
    gate_out = jnp.zeros((tm, LANES), F32)
    rank_out = jnp.zeros((tm, LANES), F32)
    for kk in range(TOP_K):
        rank = jnp.sum(ohs[kk] * base, axis=1, keepdims=True)
        idx_out = jnp.where(lane == kk, idxs[kk], idx_out)
        gate_out = jnp.where(lane == kk, es[kk] / den, gate_out)
        rank_out = jnp.where(lane == kk, rank, rank_out)
    idx_ref[...] = idx_out[:, :TOP_K].astype(I32)
    gate_ref[...] = gate_out[:, :TOP_K]
    rank_ref[...] = rank_out[:, :TOP_K].astype(I32)
    run_ref[...] = run_ref[...] + jnp.sum(oh_all, axis=0, keepdims=True)
    cnt_ref[...] = run_ref[...]


def _out_proj_route(x, ro, ao, w_out, g1, sh2, sc2, norm2_g, w_router, b_router):
    t, d = x.shape
    tm = min(RT_TM, t)
    w = GROUP_WIDTH
    per_row = g1.shape[0] != 1
    wor = w_out[:w].astype(BF16)
    woa = w_out[w:].astype(BF16)
    wr = jnp.pad(w_router, ((0, 0), (0, LANES - N_EXPERTS))).astype(BF16)
    br = jnp.pad(b_router.reshape(1, -1), ((0, 0), (0, LANES - N_EXPERTS)), constant_values=NEG_BIG)
    ltri = jnp.asarray(np.tril(np.ones((tm, tm), np.float32), -1), BF16)
    mod_spec = pl.BlockSpec((tm, d), lambda i: (i, 0)) if per_row else pl.BlockSpec((1, d), lambda i: (0, 0))
    row = lambda n: pl.BlockSpec((tm, n), lambda i: (i, 0))
    full = lambda a: pl.BlockSpec(a.shape, lambda i: (0, 0))
    return pl.pallas_call(
        _route_kernel,
        grid=(t // tm,),
        in_specs=[row(d), row(w), row(w), full(wor), full(woa), mod_spec, mod_spec, mod_spec,
                  pl.BlockSpec((1, d), lambda i: (0, 0)), full(wr), full(br), full(ltri)],
        out_specs=[row(d), row(d), row(TOP_K), row(TOP_K), row(TOP_K), pl.BlockSpec((1, LANES), lambda i: (0, 0))],
        out_shape=[jax.ShapeDtypeStruct((t, d), F32), jax.ShapeDtypeStruct((t, d), BF16),
                   jax.ShapeDtypeStruct((t, TOP_K), I32), jax.ShapeDtypeStruct((t, TOP_K), F32),
                   jax.ShapeDtypeStruct((t, TOP_K), I32), jax.ShapeDtypeStruct((1, LANES), F32)],
        scratch_shapes=[pltpu.VMEM((1, LANES), F32)],
        compiler_params=_cparams(("arbitrary",)),
    )(x, ro, ao, wor, woa, g1, sh2, sc2, norm2_g.reshape(1, d), wr, br, ltri)


EX_TM = 256
BF16_ROW = (SUBLANES, LANES)


def _dispatch_kernel(slot_ref, h_ref, init_ref, out_ref, sem):
    del init_ref
    tm = h_ref.shape[0]

    def body(r, _):
        for kk in range(TOP_K):
            pltpu.make_async_copy(h_ref.at[r], out_ref.at[slot_ref[r * TOP_K + kk]], sem).start()
        return 0

    lax.fori_loop(0, tm, body, 0)
    for _ in range(TOP_K):
        pltpu.make_async_copy(h_ref, out_ref.at[pl.ds(0, tm)], sem).wait()


def _dispatch(h2, slot, rows_sorted):
    t, d = h2.shape
    tm = min(RT_TM, t)
    assert d == SUBLANES * LANES
    h3 = h2.reshape((t,) + BF16_ROW)
    return pl.pallas_call(
        _dispatch_kernel,
        grid=(t // tm,),
        in_specs=[pl.BlockSpec((tm * TOP_K,), lambda i: (i,), memory_space=pltpu.SMEM),
                  pl.BlockSpec((tm,) + BF16_ROW, lambda i: (i, 0, 0)),
                  pl.BlockSpec(memory_space=pl.ANY)],
        out_specs=pl.BlockSpec(memory_space=pl.ANY),
        out_shape=jax.ShapeDtypeStruct(rows_sorted.shape, rows_sorted.dtype),
        scratch_shapes=[pltpu.SemaphoreType.DMA(())],
        input_output_aliases={2: 0},
        compiler_params=_cparams(("arbitrary",)),
    )(slot.reshape(-1), h3, rows_sorted)


def _expert_kernel(be_ref, nv_ref, x_ref, wgu_ref, bgu_ref, wd_ref, bd_ref, o_ref, wgu_bf, wd_bf):
    i = pl.program_id(0)
    changed = jnp.logical_or(i == 0, be_ref[i] != be_ref[jnp.maximum(i - 1, 0)])

    @pl.when(changed)
    def _():
        wgu_bf[...] = wgu_ref[0].astype(BF16)
        wd_bf[...] = wd_ref[0].astype(BF16)

    @pl.when(i < nv_ref[0])
    def _():
        f = wd_bf.shape[0]
        gu = _dot(x_ref[...], wgu_bf[...]) + bgu_ref[0]
        glu = jnp.minimum(gu[:, :f], SWIGLU_LIMIT)
        lin = jnp.clip(gu[:, f:], -SWIGLU_LIMIT, SWIGLU_LIMIT)
        act = glu * jax.nn.sigmoid(SWIGLU_ALPHA * glu) * (lin + 1.0)
        o_ref[...] = _dot(act.astype(BF16), wd_bf[...]) + bd_ref[0]

    @pl.when(i >= nv_ref[0])
    def _():
        o_ref[...] = jnp.zeros(o_ref.shape, F32)


def _experts(rows_sorted, blk_e, n_valid, w_gu, b_gu, w_down, b_down):
    ns, d = rows_sorted.shape
    e, _, f2 = w_gu.shape
    f = f2 // 2
    grid_spec = pltpu.PrefetchScalarGridSpec(
        num_scalar_prefetch=2,
        grid=(ns // EX_TM,),
        in_specs=[pl.BlockSpec((EX_TM, d), lambda i, be, nv: (i, 0)),
                  pl.BlockSpec((1, d, f2), lambda i, be, nv: (be[i], 0, 0)),
                  pl.BlockSpec((1, 1, f2), lambda i, be, nv: (be[i], 0, 0)),
                  pl.BlockSpec((1, f, d), lambda i, be, nv: (be[i], 0, 0)),
                  pl.BlockSpec((1, 1, d), lambda i, be, nv: (be[i], 0, 0))],
        out_specs=pl.BlockSpec((EX_TM, d), lambda i, be, nv: (i, 0)),
        scratch_shapes=[pltpu.VMEM((d, f2), BF16), pltpu.VMEM((f, d), BF16)])
    return pl.pallas_call(
        _expert_kernel,
        grid_spec=grid_spec,
        out_shape=jax.ShapeDtypeStruct((ns, d), F32),
        compiler_params=_cparams(("arbitrary",)),
    )(blk_e, n_valid, rows_sorted, w_gu, b_gu.reshape(e, 1, f2), w_down, b_down.reshape(e, 1, d))


def _combine_kernel(slot_ref, rows_ref, gate_ref, x1_ref, g2_ref, nf_ref, y_ref, buf, sem):
    tm = x1_ref.shape[0]

    def body(r, _):
        for kk in range(TOP_K):
            pltpu.make_async_copy(rows_ref.at[pl.ds(slot_ref[r * TOP_K + kk], 1)],
                                  buf.at[kk, pl.ds(r, 1)], sem).start()
        return 0

    lax.fori_loop(0, tm, body, 0)
    for kk in range(TOP_K):
        pltpu.make_async_copy(rows_ref.at[pl.ds(0, tm)], buf.at[kk], sem).wait()
    gates = gate_ref[...]
    moe = gates[:, 0:1] * buf[0]
    for kk in range(1, TOP_K):
        moe = moe + gates[:, kk:kk + 1] * buf[kk]
    x2 = x1_ref[...] + g2_ref[...] * moe
    y_ref[...] = x2 * lax.rsqrt(jnp.mean(x2 * x2, axis=-1, keepdims=True) + RMS_EPS) * nf_ref[...]


def _combine(out_rows, slot, gates, x1, g2, normf_g):
    t, d = x1.shape
    tm = min(RT_TM, t)
    per_row = g2.shape[0] != 1
    mod_spec = pl.BlockSpec((tm, d), lambda i: (i, 0)) if per_row else pl.BlockSpec((1, d), lambda i: (0, 0))
    return pl.pallas_call(
        _combine_kernel,
        grid=(t // tm,),
        in_specs=[pl.BlockSpec((tm * TOP_K,), lambda i: (i,), memory_space=pltpu.SMEM),
                  pl.BlockSpec(memory_space=pl.ANY),
                  pl.BlockSpec((tm, TOP_K), lambda i: (i, 0)),
                  pl.BlockSpec((tm, d), lambda i: (i, 0)), mod_spec,
                  pl.BlockSpec((1, d), lambda i: (0, 0))],
        out_specs=pl.BlockSpec((tm, d), lambda i: (i, 0)),
        out_shape=jax.ShapeDtypeStruct((t, d), F32),
        scratch_shapes=[pltpu.VMEM((TOP_K, tm, d), F32), pltpu.SemaphoreType.DMA(())],
        compiler_params=_cparams(("arbitrary",)),
    )(slot.reshape(-1), out_rows, gates, x1, g2, normf_g.reshape(1, d))


def _moe_plan(idx_p, rank_p, cnt_p, idx_s, rank_s, cnt_s):
    n_assign = idx_p.size + idx_s.size
    n_tiles = -(-n_assign // EX_TM) + N_EXPERTS
    cp = cnt_p[0, :N_EXPERTS].astype(I32)
    cs = cnt_s[0, :N_EXPERTS].astype(I32)
    padded = (cp + cs + EX_TM - 1) // EX_TM * EX_TM
    pad_end = jnp.cumsum(padded)
    pad_start = pad_end - padded
    slot_p = pad_start[idx_p] + rank_p
    slot_s = pad_start[idx_s] + cp[idx_s] + rank_s
    blk_e = jnp.minimum(jnp.searchsorted(pad_end, jnp.arange(n_tiles, dtype=I32) * EX_TM, side='right'),
                        N_EXPERTS - 1).astype(I32)
    n_valid = (pad_end[-1:] // EX_TM).astype(I32)
    return slot_p, slot_s, blk_e, n_valid, n_tiles * EX_TM


PG = 8
QROWS = N_HEADS * SUBLANES


def _page_specs(block, n_pages):
    def spec(u):
        return pl.BlockSpec(block, lambda b, j, pt: (pt[b * n_pages + j * PG + u],) + (0,) * (len(block) - 1))
    return [spec(u) for u in range(PG)]


def _head_sum(x):
    out = x[:SUBLANES]
    for h in range(1, N_HEADS):
        out = out + x[h * SUBLANES:(h + 1) * SUBLANES]
    return out


def _sample_index_kernel(pt_ref, qi_ref, wrep_ref, kin_ref, *rest, n_pages, t_new, topk):
    del pt_ref
    pages = rest[:PG]
    keys_ref, tau_ref = rest[PG:]
    j = pl.program_id(1)
    qi = qi_ref[0]
    wrep = wrep_ref[0]

    def scores(ki_bf):
        s = _dot_nt(qi, ki_bf)
        return _head_sum(jnp.maximum(s, 0.0) * wrep)

    for u in range(PG):
        keys_ref[0, j * PG + u] = _float_key(scores(pages[u][0].astype(BF16)))

    @pl.when(j == pl.num_programs(1) - 1)
    def _():
        qrow = lax.broadcasted_iota(I32, (SUBLANES, PAGE_SIZE), 0)
        col = lax.broadcasted_iota(I32, (SUBLANES, PAGE_SIZE), 1)
        ok = jnp.logical_and(col <= qrow, col < t_new)
        keys_ref[0, n_pages] = jnp.where(ok, _float_key(scores(kin_ref[0])), KEY_NEG_INF)

        def count_ge(mid):
            def cbody(t, c):
                return c + jnp.where(keys_ref[0, t] >= mid, 1.0, 0.0)

            c = lax.fori_loop(0, n_pages + 1, cbody, jnp.zeros((SUBLANES, PAGE_SIZE), F32))
            return jnp.broadcast_to(jnp.sum(c, axis=1, keepdims=True), (SUBLANES, LANES))

        tau_ref[0] = _kth_largest_key(count_ge, jnp.full((SUBLANES, LANES), KEY_NEG_INF + 1, I32),
                                      jnp.full((SUBLANES, LANES), 0x7F800000, I32), topk)


def _sample_attend_kernel(pt_ref, q_ref, keys_ref, tau_ref, band_ref, kn_ref, vn_ref, *rest, n_pages):
    del pt_ref
    kpages = rest[:PG]
    vpages = rest[PG:2 * PG]
    o_ref, m_ref, l_ref, acc_ref = rest[2 * PG:]
    j = pl.program_id(1)
    last = j == pl.num_programs(1) - 1

    @pl.when(j == 0)
    def _():
        m_ref[...] = jnp.full(m_ref.shape, NEG_BIG, F32)
        l_ref[...] = jnp.zeros(l_ref.shape, F32)
        acc_ref[...] = jnp.zeros(acc_ref.shape, F32)

    q = q_ref[0]
    tau = _rep(tau_ref[0], N_HEADS, axis=0)
    width = acc_ref.shape[1]

    def attend(k_bf, v_bf, key_tile, bias):
        s = _dot_nt(q, k_bf)
        if bias is not None:
            s = s + bias
        s = jnp.where(_rep(key_tile, N_HEADS, axis=0) >= tau, s, NEG_BIG)
        m_old = m_ref[...]
        m_new = jnp.maximum(m_old, jnp.broadcast_to(jnp.max(s, axis=1, keepdims=True), m_old.shape))
        p = jnp.exp(s - m_new)
        alpha = jnp.exp(m_old - m_new)
        l_ref[...] = alpha * l_ref[...] + jnp.broadcast_to(jnp.sum(p, axis=1, keepdims=True), m_old.shape)
        m_ref[...] = m_new
        acc_ref[...] = acc_ref[...] * _rep(alpha, width // LANES, axis=1) + _dot(p.astype(BF16), v_bf)

    for u in range(PG):
        bias = None
        if u == PG - 1:
            bias = jnp.where(last, band_ref[:, :PAGE_SIZE], 0.0)
        attend(kpages[u][0].astype(BF16), vpages[u][0].astype(BF16), keys_ref[0, j * PG + u], bias)

    @pl.when(last)
    def _():
        attend(kn_ref[0], vn_ref[0], keys_ref[0, n_pages], band_ref[:, PAGE_SIZE:])
        accn = acc_ref[...] / _rep(l_ref[...], width // LANES, axis=1)
        lane_head = lax.broadcasted_iota(I32, (SUBLANES, width), 1) // HEAD_DIM
        out = jnp.zeros((SUBLANES, width), F32)
        for h in range(N_HEADS):
            out = out + jnp.where(lane_head == h, accn[h * SUBLANES:(h + 1) * SUBLANES], 0.0)
        o_ref[0] = out


def _sample_attention(qs, qib, wi, k_new_bf, v_new_bf, ki_new_bf, cache_k, cache_v, cache_kidx, page_table,
                      rel_bias, b, t_new):
    w = GROUP_WIDTH
    n_pages = page_table.shape[1]
    assert n_pages % PG == 0 and t_new <= SUBLANES and cache_k.shape[1] == PAGE_SIZE
    past = n_pages * PAGE_SIZE
    topk = min(TOPK_MAX, (past + t_new) // 4)
    pt = page_table.reshape(-1)
    padq = lambda a: jnp.pad(a, ((0, 0), (0, SUBLANES - t_new)) + ((0, 0),) * (a.ndim - 2))
    padk = lambda a: jnp.pad(a.reshape(b, t_new, -1), ((0, 0), (0, PAGE_SIZE - t_new), (0, 0)))
    qi_r = padq(qib.reshape(b, t_new, N_IDX_HEADS, IDX_DIM)).transpose(0, 2, 1, 3).reshape(b, QROWS, IDX_DIM)
    w_r = padq(wi.reshape(b, t_new, N_IDX_HEADS)).transpose(0, 2, 1).reshape(b, QROWS, 1)
    w_r = jnp.broadcast_to(w_r, (b, QROWS, LANES))
    q4 = padq(qs.reshape(b, t_new, N_HEADS, HEAD_DIM)).transpose(0, 2, 1, 3)
    q_bd = (q4[:, :, :, None, :] * jnp.eye(N_HEADS, dtype=qs.dtype)[None, :, None, :, None]).reshape(b, QROWS, w)
    kin, kn, vn = padk(ki_new_bf), padk(k_new_bf), padk(v_new_bf)
    ck = cache_k.reshape(cache_k.shape[0], PAGE_SIZE, w)
    cv = cache_v.reshape(cache_v.shape[0], PAGE_SIZE, w)
    band = _bias_band(rel_bias, SUBLANES, 2 * PAGE_SIZE, 1, PAGE_SIZE).reshape(QROWS, 2 * PAGE_SIZE)

    per_b = lambda shape: pl.BlockSpec((1,) + shape, lambda bb, j, p_: (bb,) + (0,) * len(shape))
    steps = n_pages // PG
    keys, tau = pl.pallas_call(
        functools.partial(_sample_index_kernel, n_pages=n_pages, t_new=t_new, topk=topk),
        grid_spec=pltpu.PrefetchScalarGridSpec(
            num_scalar_prefetch=1, grid=(b, steps),
            in_specs=[per_b((QROWS, IDX_DIM)), per_b((QROWS, LANES)), per_b((PAGE_SIZE, IDX_DIM))]
            + _page_specs((1, PAGE_SIZE, IDX_DIM), n_pages),
            out_specs=[per_b((n_pages + 1, SUBLANES, PAGE_SIZE)), per_b((SUBLANES, LANES))]),
        out_shape=[jax.ShapeDtypeStruct((b, n_pages + 1, SUBLANES, PAGE_SIZE), I32),
                   jax.ShapeDtypeStruct((b, SUBLANES, LANES), I32)],
        compiler_params=_cparams(("arbitrary", "arbitrary")),
    )(pt, qi_r, w_r, kin, *([cache_kidx] * PG))
    out = pl.pallas_call(
        functools.partial(_sample_attend_kernel, n_pages=n_pages),
        grid_spec=pltpu.PrefetchScalarGridSpec(
            num_scalar_prefetch=1, grid=(b, steps),
            in_specs=[per_b((QROWS, w)), per_b((n_pages + 1, SUBLANES, PAGE_SIZE)), per_b((SUBLANES, LANES)),
                      pl.BlockSpec(band.shape, lambda bb, j, p_: (0, 0)), per_b((PAGE_SIZE, w)),
                      per_b((PAGE_SIZE, w))]
            + _page_specs((1, PAGE_SIZE, w), n_pages) + _page_specs((1, PAGE_SIZE, w), n_pages),
            out_specs=per_b((SUBLANES, w)),
            scratch_shapes=[pltpu.VMEM((QROWS, LANES), F32), pltpu.VMEM((QROWS, LANES), F32),
                            pltpu.VMEM((QROWS, w), F32)]),
        out_shape=jax.ShapeDtypeStruct((b, SUBLANES, w), F32),
        compiler_params=_cparams(("arbitrary", "arbitrary")),
    )(pt, q_bd, keys, tau, band, kn, vn, *([ck] * PG), *([cv] * PG))
    return out[:, :t_new].reshape(b * t_new, w)


def kernel(x_prompt, x_sample, c_prompt, c_sample, cache_k, cache_v, cache_kidx, page_table, state_wkv,
           state_shift, w_ada, b_ada, norm1_g, w_in, mu_shift, w0, w_decay2, a0, w_a2, w_g2, k_k, k_a, r_k,
           lnx_g, lnx_b, rel_bias, w_out, norm2_g, w_router, b_router, w_gu, b_gu, w_down, b_down, normf_g):
    depth = w_in.shape[0]
    assert depth == 1, "the merged prompt+sample expert pass is written for a single layer"
    bp, sp, d = x_prompt.shape
    bs, ts, _ = x_sample.shape
    assert bp == 1 and sp % RBLK == 0 and bs % RNB == 0 and ts <= RC
    l = 0
    lw = {'mu_shift': mu_shift[l], 'w0': w0[l], 'w_decay2': w_decay2[l], 'a0': a0[l], 'w_a2': w_a2[l],
          'w_g2': w_g2[l], 'k_k': k_k[l], 'k_a': k_a[l], 'r_k': r_k[l], 'lnx_g': lnx_g[l], 'lnx_b': lnx_b[l]}
    ns = bs * ts

    c_all = jnp.concatenate([c_prompt, c_sample], axis=0)
    n_c = c_all.shape[0]
    c_all = jnp.pad(c_all, ((0, -n_c % SUBLANES), (0, 0)))
    mod = _adaln(c_all, w_ada[l], b_ada[l])
    mods_p = [mod[0:1, i * d:(i + 1) * d] for i in range(6)]
    mods_s = [jnp.repeat(mod[bp:bp + bs, i * d:(i + 1) * d], ts, axis=0) for i in range(6)]

    xp = x_prompt.reshape(sp, d)
    xs = x_sample.reshape(ns, d)
    pr_p, k_p, v_p, ki_p, wi_p, qs_p, kb_p, _, qib_p, kib_p, vt_p = _in_proj(xp, norm1_g[l], mods_p[0],
                                                                             mods_p[1], w_in[l])
    pr_s, k_s, v_s, ki_s, wi_s, qs_s, kb_s, vb_s, qib_s, kib_s, _ = _in_proj(xs, norm1_g[l], mods_s[0],
                                                                             mods_s[1], w_in[l])

    ro_p, wkv_p = _rwkv(pr_p, lw, chain=True)
    att_p = _prompt_attention(qs_p, qib_p, wi_p, kib_p, kb_p, vt_p, rel_bias)

    pr_s3 = pr_s.reshape(bs, ts, RWKV_COLS)
    prev_s3 = jnp.concatenate([state_shift[l][:, None, :], pr_s3[:, :-1]], axis=1)
    padc = lambda a: jnp.pad(a, ((0, 0), (0, RC - ts), (0, 0))).reshape(bs * RC, RWKV_COLS)
    ro_s, wkv_s = _rwkv(padc(pr_s3), lw, chain=False, pprev=padc(prev_s3), s0=state_wkv[l], t_valid=ts)
    ro_s = ro_s.reshape(bs, RC, GROUP_WIDTH)[:, :ts].reshape(ns, GROUP_WIDTH)
    att_s = _sample_attention(qs_s, qib_s, wi_s, kb_s, vb_s, kib_s, cache_k[l], cache_v[l], cache_kidx[l],
                              page_table, rel_bias, bs, ts)

    x1_p, h2_p, idx_p, gate_p, rank_p, cnt_p = _out_proj_route(xp, ro_p, att_p, w_out[l], mods_p[2], mods_p[3],
                                                               mods_p[4], norm2_g[l], w_router[l], b_router[l])
    x1_s, h2_s, idx_s, gate_s, rank_s, cnt_s = _out_proj_route(xs, ro_s, att_s, w_out[l], mods_s[2], mods_s[3],
                                                               mods_s[4], norm2_g[l], w_router[l], b_router[l])
    slot_p, slot_s, blk_e, n_valid, n_slots = _moe_plan(idx_p, rank_p, cnt_p, idx_s, rank_s, cnt_s)
    rows = jnp.zeros((n_slots,) + BF16_ROW, BF16)
    rows = _dispatch(h2_p, slot_p, rows)
    rows = _dispatch(h2_s, slot_s, rows)
    out_rows = _experts(rows.reshape(n_slots, d), blk_e, n_valid, w_gu[l], b_gu[l], w_down[l], b_down[l])
    y_p = _combine(out_rows, slot_p, gate_p, x1_p, mods_p[5], normf_g)
    y_s = _combine(out_rows, slot_s, gate_s, x1_s, mods_s[5], normf_g)

    hd = (N_HEADS, HEAD_DIM)
    return (y_p.reshape(bp, sp, d), y_s.reshape(bs, ts, d),
            k_p.reshape((1, bp, sp) + hd), v_p.reshape((1, bp, sp) + hd), ki_p.reshape(1, bp, sp, IDX_DIM),
            wkv_p.reshape((1, bp) + (N_HEADS, HEAD_DIM, HEAD_DIM)), pr_p[sp - 1:sp].reshape(1, bp, RWKV_COLS),
            k_s.reshape((1, bs, ts) + hd), v_s.reshape((1, bs, ts) + hd), ki_s.reshape(1, bs, ts, IDX_DIM),
            wkv_s.reshape((1, bs) + (N_HEADS, HEAD_DIM, HEAD_DIM)), pr_s3[:, ts - 1].reshape(1, bs, RWKV_COLS))
```

```python
import functools
import math

import jax
import jax.numpy as jnp
import numpy as np
from jax import lax
from jax.experimental import pallas as pl
from jax.experimental.pallas import tpu as pltpu

F32 = jnp.float32
BF16 = jnp.bfloat16
I32 = jnp.int32

HEAD_DIM = 64
N_HEADS = 8
GROUP_WIDTH = N_HEADS * HEAD_DIM
DECAY_LORA, AAA_LORA, GATE_LORA = 64, 64, 128
RWKV_COLS = 3 * GROUP_WIDTH + DECAY_LORA + AAA_LORA + GATE_LORA
IDX_DIM = 64
N_IDX_HEADS = 8
TOPK_MAX = 256
N_BUCKETS = 32
MAX_DISTANCE = 128
N_EXPERTS = 32
TOP_K = 4
SWIGLU_LIMIT = 7.0
SWIGLU_ALPHA = 1.702
RMS_EPS = 1e-6
GN_EPS = HEAD_DIM * 1e-5
PAGE_SIZE = 128

LANES = 128
SUBLANES = 8
VMEM_LIMIT = 56 * 1024 * 1024

NEG_BIG = -1e30
KEY_NEG_INF = -2139095041


def _cparams(sem):
    return pltpu.CompilerParams(dimension_semantics=sem, vmem_limit_bytes=VMEM_LIMIT)


def _dot(a, b):
    return jnp.dot(a, b, preferred_element_type=F32)


def _dot_nt(a, b):
    return lax.dot_general(a, b, (((1,), (1,)), ((), ())), preferred_element_type=F32)


def _split2(a):
    hi = a.astype(BF16)
    lo = (a - hi.astype(F32)).astype(BF16)
    return hi, lo


def _split3(a):
    hi = a.astype(BF16)
    r1 = a - hi.astype(F32)
    mid = r1.astype(BF16)
    lo = (r1 - mid.astype(F32)).astype(BF16)
    return hi, mid, lo


def _rep(x, n, axis):
    return jnp.concatenate([x] * n, axis=axis)


def _float_key(x):
    b = pltpu.bitcast(x, I32)
    return b ^ ((b >> 31) & 0x7FFFFFFF)


def _ada_kernel(c_ref, w_ref, b_ref, o_ref):
    c = c_ref[...]
    s = c * jax.nn.sigmoid(c)
    o_ref[...] = _dot(s.astype(BF16), w_ref[...].astype(BF16)) + b_ref[...]


def _adaln(c, w_ada, b_ada):
    r, d = c.shape
    n = w_ada.shape[1]
    tn = 1536
    return pl.pallas_call(
        _ada_kernel,
        grid=(n // tn,),
        in_specs=[pl.BlockSpec((r, d), lambda j: (0, 0)),
                  pl.BlockSpec((d, tn), lambda j: (0, j)),
                  pl.BlockSpec((1, tn), lambda j: (0, j))],
        out_specs=pl.BlockSpec((r, tn), lambda j: (0, j)),
        out_shape=jax.ShapeDtypeStruct((r, n), F32),
        compiler_params=_cparams(("arbitrary",)),
    )(c, w_ada, b_ada.reshape(1, n))


def _inproj_kernel(x_ref, g_ref, sh_ref, sc_ref, wr_ref, wa_ref, wk_ref,
                   pr_ref, k_ref, v_ref, ki_ref, wi_ref, qs_ref, kb_ref, vb_ref, qib_ref, kib_ref, vt_ref):
    x = x_ref[...]
    y = x * lax.rsqrt(jnp.mean(x * x, axis=-1, keepdims=True) + RMS_EPS) * g_ref[...]
    h = (y * (1.0 + sc_ref[...]) + sh_ref[...]).astype(BF16)
    pr_ref[...] = _dot(h, wr_ref[...])
    a = _dot(h, wa_ref[...])
    w = GROUP_WIDTH
    q, k, v, qi = a[:, :w], a[:, w:2 * w], a[:, 2 * w:3 * w], a[:, 3 * w:4 * w]
    k_ref[...] = k
    v_ref[...] = v
    qs_ref[...] = (q * HEAD_DIM ** -0.5).astype(BF16)
    kb_ref[...] = k.astype(BF16)
    vb_ref[...] = v.astype(BF16)
    vt_ref[...] = v.T.astype(BF16)
    qib_ref[...] = (qi * IDX_DIM ** -0.5).astype(BF16)
    kw = _dot(h, wk_ref[...])
    ki = kw[:, :IDX_DIM]
    ki_ref[...] = ki
    kib_ref[...] = ki.astype(BF16)
    wi_ref[...] = kw[:, IDX_DIM:IDX_DIM + N_IDX_HEADS] * N_IDX_HEADS ** -0.5


def _in_proj(x, norm_g, shift, scale, w_in):
    t, d = x.shape
    tm = min(512, t)
    per_row = shift.shape[0] != 1
    w = GROUP_WIDTH
    a0 = RWKV_COLS
    wr = w_in[:, :a0].astype(BF16)
    wa = w_in[:, a0:a0 + 4 * w].astype(BF16)
    wk = jnp.pad(w_in[:, a0 + 4 * w:], ((0, 0), (0, LANES - IDX_DIM - N_IDX_HEADS))).astype(BF16)
    mod_spec = pl.BlockSpec((tm, d), lambda i: (i, 0)) if per_row else pl.BlockSpec((1, d), lambda i: (0, 0))
    row = lambda n: pl.BlockSpec((tm, n), lambda i: (i, 0))
    full = lambda a: pl.BlockSpec(a.shape, lambda i: (0, 0))
    sds = lambda n, dt: jax.ShapeDtypeStruct((t, n), dt)
    return pl.pallas_call(
        _inproj_kernel,
        grid=(t // tm,),
        in_specs=[row(d), pl.BlockSpec((1, d), lambda i: (0, 0)), mod_spec, mod_spec, full(wr), full(wa), full(wk)],
        out_specs=[row(a0), row(w), row(w), row(IDX_DIM), row(N_IDX_HEADS), row(w), row(w), row(w), row(w),
                   row(IDX_DIM), pl.BlockSpec((w, tm), lambda i: (0, i))],
        out_shape=[sds(a0, F32), sds(w, F32), sds(w, F32), sds(IDX_DIM, F32), sds(N_IDX_HEADS, F32),
                   sds(w, BF16), sds(w, BF16), sds(w, BF16), sds(w, BF16), sds(IDX_DIM, BF16),
                   jax.ShapeDtypeStruct((w, t), BF16)],
        compiler_params=_cparams(("arbitrary",)),
    )(x, norm_g.reshape(1, d), shift, scale, wr, wa, wk)


def _bias_band_kernel(rb_ref, o_ref, *, sign, off):
    _, rows, cols = o_ref.shape
    r = lax.broadcasted_iota(I32, (rows, cols), 0)
    c = lax.broadcasted_iota(I32, (rows, cols), 1)
    n = jnp.maximum(sign * (r - c) + off, 0)
    max_exact = N_BUCKETS // 2
    nf = jnp.maximum(n, 1).astype(F32)
    large = max_exact + (jnp.log(nf / max_exact) / math.log(MAX_DISTANCE / max_exact)
                         * (N_BUCKETS - max_exact)).astype(I32)
    large = jnp.minimum(large, N_BUCKETS - 1)
    bucket = jnp.where(n < max_exact, n, large)
    for h in range(N_HEADS):
        far = rb_ref[N_BUCKETS - 1, h]
        acc = jnp.zeros((rows, cols), F32)
        for b in range(N_BUCKETS - 1):
            acc = jnp.where(bucket == b, rb_ref[b, h] - far, acc)
        o_ref[h] = acc


def _bias_band(rel_bias, rows, cols, sign, off):
    return pl.pallas_call(
        functools.partial(_bias_band_kernel, sign=sign, off=off),
        in_specs=[pl.BlockSpec(memory_space=pltpu.SMEM)],
        out_specs=pl.BlockSpec((N_HEADS, rows, cols), lambda: (0, 0, 0)),
        out_shape=jax.ShapeDtypeStruct((N_HEADS, rows, cols), F32),
    )(rel_bias)


def _kth_largest_key(count_ge, lo0, hi0, topk):
    def cond(carry):
        return carry[2] > 0

    def body(carry):
        lo, hi, _ = carry
        mid = (lo | hi) - ((lo ^ hi) >> 1)
        cnt = count_ge(mid)
        active = lo < hi
        exact = jnp.logical_and(active, cnt == topk)
        ge = cnt >= topk
        lo_n = jnp.where(active, jnp.where(ge, mid, lo), lo)
        hi_n = jnp.where(active, jnp.where(exact, mid, jnp.where(ge, hi, mid - 1)), hi)
        return lo_n, hi_n, jnp.max(jnp.where(lo_n < hi_n, 1.0, 0.0))

    lo, _, _ = lax.while_loop(cond, body, (lo0, hi0, jnp.max(jnp.where(lo0 < hi0, 1.0, 0.0))))
    return lo


PQB = 256
PKT = 512
PKB = 1024
PSUB = 256
CNT_ROWS = 64


def _prompt_att_kernel(qs_ref, qib_ref, wit_ref, kib_ref, kb_ref, vt_ref, band_ref, o_ref,
                       keys_ref, tau_ref, m_ref, l_ref, acc_ref, s_ref, p_ref, a_ref, *, topk, qb_rows):
    qb = pl.program_id(0)
    j = pl.program_id(1)
    q_lo = qb * qb_rows
    n_kt = (q_lo + qb_rows + PKT - 1) // PKT

    @pl.when(j == 0)
    def _index_phase():
        q_pos = q_lo + lax.broadcasted_iota(I32, (PKT, qb_rows), 1)

        def tile_body(kt, carry):
            k0 = pl.multiple_of(kt * PKT, PKT)
            ki = kib_ref[pl.ds(k0, PKT), :]
            acc = jnp.zeros((PKT, qb_rows), F32)
            for h in range(N_IDX_HEADS):
                s = _dot_nt(ki, qib_ref[:, h * IDX_DIM:(h + 1) * IDX_DIM])
                acc = acc + jnp.maximum(s, 0.0) * wit_ref[h:h + 1, :]
            k_pos = k0 + lax.broadcasted_iota(I32, (PKT, qb_rows), 0)
            causal = k_pos <= q_pos
            keys_ref[kt] = jnp.where(causal, _float_key(acc), KEY_NEG_INF)
            smax, smin = carry
            grp = lambda x: x.reshape(PKT // SUBLANES, SUBLANES, qb_rows)
            smax = jnp.maximum(smax, jnp.max(grp(jnp.where(causal, acc, -jnp.inf)), axis=0))
            smin = jnp.minimum(smin, jnp.min(grp(jnp.where(causal, acc, jnp.inf)), axis=0))
            return smax, smin

        smax, smin = lax.fori_loop(
            0, n_kt, tile_body,
            (jnp.full((SUBLANES, qb_rows), -jnp.inf, F32), jnp.full((SUBLANES, qb_rows), jnp.inf, F32)))
        fmax = jnp.max(smax, axis=0, keepdims=True)
        fmin = jnp.min(smin, axis=0, keepdims=True)
        hi0 = jnp.where(fmax == 0.0, 0, _float_key(fmax))
        lo0 = jnp.where(fmin == 0.0, -1, _float_key(fmin))
        n_causal = q_lo + 1 + lax.broadcasted_iota(I32, (1, qb_rows), 1)
        lo0 = jnp.where(n_causal < topk, KEY_NEG_INF + 1, lo0)
        hi0 = jnp.where(n_causal < topk, KEY_NEG_INF + 1, hi0)

        def count_ge(mid):
            def cbody(kt, c):
                for r0 in range(0, PKT, CNT_ROWS):
                    ge = jnp.where(keys_ref[kt, r0:r0 + CNT_ROWS, :] >= mid, 1.0, 0.0)
                    c = c + jnp.sum(ge.reshape(CNT_ROWS // SUBLANES, SUBLANES, qb_rows), axis=0)
                return c

            c = lax.fori_loop(0, n_kt, cbody, jnp.zeros((SUBLANES, qb_rows), F32))
            return jnp.sum(c, axis=0, keepdims=True)

        tau_ref[...] = _kth_largest_key(count_ge, lo0, hi0, topk)
        m_ref[...] = jnp.full(m_ref.shape, NEG_BIG, F32)
        l_ref[...] = jnp.zeros(l_ref.shape, F32)
        acc_ref[...] = jnp.zeros(acc_ref.shape, F32)

    kb = j - 1
    last_kb = (q_lo + qb_rows - 1) // PKB

    @pl.when(jnp.logical_and(j >= 1, kb <= last_kb))
    def _attend_phase():
        tau = tau_ref[...]
        for u in range(PKB // PSUB):
            s0 = kb * PKB + u * PSUB

            def sub_tile(near, u=u, s0=s0):
                kt = (kb * PKB + u * PSUB) // PKT
                c0 = (u * PSUB) % PKT
                sel = keys_ref[kt, c0:c0 + PSUB, :] >= tau
                if near:
                    which = jnp.clip((q_lo - s0) // PSUB, 0, 1)
                for h in range(N_HEADS):
                    hs = slice(h * HEAD_DIM, (h + 1) * HEAD_DIM)
                    s_ref[h] = _dot_nt(kb_ref[u * PSUB:(u + 1) * PSUB, hs], qs_ref[:, hs])
                for h in range(N_HEADS):
                    s = s_ref[h]
                    if near:
                        s = s + band_ref[h, 1 - which]
                    s = jnp.where(sel, s, NEG_BIG)
                    m_old = m_ref[h]
                    m_new = jnp.maximum(m_old, jnp.max(s, axis=0, keepdims=True))
                    p = jnp.exp(s - m_new)
                    alpha = jnp.exp(m_old - m_new)
                    l_ref[h] = alpha * l_ref[h] + jnp.sum(p, axis=0, keepdims=True)
                    m_ref[h] = m_new
                    a_ref[h] = alpha
                    p_ref[h] = p.astype(BF16)
                for h in range(N_HEADS):
                    hs = slice(h * HEAD_DIM, (h + 1) * HEAD_DIM)
                    pv = _dot(vt_ref[hs, u * PSUB:(u + 1) * PSUB], p_ref[h])
                    acc_ref[h] = acc_ref[h] * a_ref[h] + pv

            is_near = s0 + PSUB > q_lo - PSUB
            in_range = s0 < q_lo + qb_rows

            @pl.when(jnp.logical_and(in_range, is_near))
            def _():
                sub_tile(True)

            @pl.when(jnp.logical_and(in_range, jnp.logical_not(is_near)))
            def _():
                sub_tile(False)

    @pl.when(j == last_kb + 1)
    def _finish():
        for h in range(N_HEADS):
            hs = slice(h * HEAD_DIM, (h + 1) * HEAD_DIM)
            o_ref[:, hs] = (acc_ref[h] / l_ref[h]).T


def _prompt_attention(qs, qib, wi, kib, kb, vt, rel_bias):
    s, w = qs.shape
    topk = min(TOPK_MAX, s // 4)
    qb_rows = min(PQB, s)
    assert s % qb_rows == 0 and s % PKB == 0 and qb_rows == PSUB
    nqb = s // qb_rows
    nkb = s // PKB
    band = _bias_band(rel_bias, 2 * PSUB, qb_rows, -1, PSUB).reshape(N_HEADS, 2, PSUB, qb_rows)

    def last_kb(i):
        return (i * qb_rows + qb_rows - 1) // PKB

    qrow = lambda n: pl.BlockSpec((qb_rows, n), lambda i, j: (i, 0))
    return pl.pallas_call(
        functools.partial(_prompt_att_kernel, topk=topk, qb_rows=qb_rows),
        grid=(nqb, nkb + 1),
        in_specs=[qrow(w), qrow(w), pl.BlockSpec((N_IDX_HEADS, qb_rows), lambda i, j: (0, i)),
                  pl.BlockSpec((s, IDX_DIM), lambda i, j: (0, 0)),
                  pl.BlockSpec((PKB, w), lambda i, j: (jnp.minimum(jnp.maximum(j - 1, 0), last_kb(i)), 0)),
                  pl.BlockSpec((w, PKB), lambda i, j: (0, jnp.minimum(jnp.maximum(j - 1, 0), last_kb(i)))),
                  pl.BlockSpec(band.shape, lambda i, j: (0, 0, 0, 0))],
        out_specs=qrow(w),
        out_shape=jax.ShapeDtypeStruct((s, w), F32),
        scratch_shapes=[pltpu.VMEM((s // PKT, PKT, qb_rows), I32),
                        pltpu.VMEM((1, qb_rows), I32),
                        pltpu.VMEM((N_HEADS, 1, qb_rows), F32),
                        pltpu.VMEM((N_HEADS, 1, qb_rows), F32),
                        pltpu.VMEM((N_HEADS, HEAD_DIM, qb_rows), F32),
                        pltpu.VMEM((N_HEADS, PSUB, qb_rows), F32),
                        pltpu.VMEM((N_HEADS, PSUB, qb_rows), BF16),
                        pltpu.VMEM((N_HEADS, 1, qb_rows), F32)],
        compiler_params=_cparams(("arbitrary", "arbitrary")),
    )(qs, qib, wi.T, kib, kb, vt, band)


RC = 16
RNB = 16
RBLK = RC * RNB


def _dot3(a, b):
    ah, al = _split2(a)
    bh, bl = _split2(b)
    return _dot(ah, bh) + (_dot(ah, bl) + _dot(al, bh))


def _dot3_nt(a, b):
    ah, al = _split2(a)
    bh, bl = _split2(b)
    return _dot_nt(ah, bh) + (_dot_nt(ah, bl) + _dot_nt(al, bh))


def _dot3_tn(a, b):
    dn = (((0,), (0,)), ((), ()))
    ah, al = _split2(a)
    bh, bl = _split2(b)
    f = lambda x, y: lax.dot_general(x, y, dn, preferred_element_type=F32)
    return f(ah, bh) + (f(ah, bl) + f(al, bh))


def _dot_exact_lhs(m_bf, x):
    hi, mid, lo = _split3(x)
    return _dot(m_bf, hi) + (_dot(m_bf, mid) + _dot(m_bf, lo))


def _dot_exact_rhs(x, m_bf):
    hi, mid, lo = _split3(x)
    return _dot(hi, m_bf) + (_dot(mid, m_bf) + _dot(lo, m_bf))


def _rwkv_kernel(*refs, chain, t_valid):
    if chain:
        p_ref, = refs[:1]
        rest = refs[1:]
    else:
        p_ref, pprev_ref, s0_ref = refs[:3]
        rest = refs[3:]
    (mu_ref, w0_ref, wd2_ref, a0_ref, wa2_ref, wg2_ref, kk_ref, ka_ref, rk_ref, lng_ref, lnb_ref,
     lt_ref, bo_ref, bd_ref, out_ref, sfin_ref,
     carry_ref, s_ref, wa_s, rq_s, uv_s, yv_s, bt_s, kt_s, v_s, gc_s, y_s,
     np_s, ti_s, ak_s, rb_s, rk_s) = rest
    i = pl.program_id(0)
    w = GROUP_WIDTH
    p = p_ref[...]
    row = lax.broadcasted_iota(I32, (RBLK, 1), 0)
    if chain:
        @pl.when(i == 0)
        def _():
            carry_ref[...] = jnp.zeros(carry_ref.shape, F32)
            s_ref[...] = jnp.zeros(s_ref.shape, F32)

        pprev = jnp.where(row == 0, carry_ref[...], pltpu.roll(p, 1, axis=0))
        carry_ref[...] = p[RBLK - 1:RBLK, :]
    else:
        pprev = pprev_ref[...]
    ps = p + (pprev - p) * mu_ref[...]
    r, k, v = ps[:, :w], ps[:, w:2 * w], ps[:, 2 * w:3 * w]
    o = 3 * w
    xw = ps[:, o:o + DECAY_LORA]
    xa = ps[:, o + DECAY_LORA:o + DECAY_LORA + AAA_LORA]
    xg = ps[:, o + DECAY_LORA + AAA_LORA:]
    dec = w0_ref[...] + _dot(jnp.tanh(xw).astype(BF16), wd2_ref[...])
    softplus = jnp.maximum(-dec, 0.0) + jnp.log(1.0 + jnp.exp(-jnp.abs(dec)))
    lw = -jnp.exp(-softplus - 0.5)
    a = jax.nn.sigmoid(a0_ref[...] + _dot(xa.astype(BF16), wa2_ref[...]))
    g = _dot(jax.nn.sigmoid(xg).astype(BF16), wg2_ref[...])
    kk = k * kk_ref[...]
    kk = kk * lax.rsqrt(jnp.maximum(_dot_exact_rhs(kk * kk, bd_ref[...]), 1e-24))
    k2 = k * (1.0 + (a - 1.0) * ka_ref[...])
    alpha = -kk
    beta = kk * a
    if t_valid < RC:
        valid = (row % RC) < t_valid
        zero = lambda x: jnp.where(valid, x, 0.0)
        lw, alpha, beta, k2, r, v = zero(lw), zero(alpha), zero(beta), zero(k2), zero(r), zero(v)
    cl = _dot_exact_lhs(lt_ref[...], lw)
    ct = _dot_exact_lhs(bo_ref[...], lw)
    g_in = jnp.exp(cl)
    g_ex = jnp.exp(cl - lw)
    g_inv = jnp.exp(-cl)
    g_end = jnp.exp(ct - cl)
    g_all = jnp.exp(ct)
    at, rt = alpha * g_ex, r * g_in
    bh, kh = beta * g_inv, k2 * g_inv
    bt, kt = beta * g_end, k2 * g_end

    ri = lax.broadcasted_iota(I32, (RBLK, RBLK), 0)
    ci = lax.broadcasted_iota(I32, (RBLK, RBLK), 1)
    same = (ri // RC) == (ci // RC)
    strict = jnp.logical_and(same, ci < ri)
    incl = jnp.logical_and(same, ci <= ri)
    eye = (ri == ci).astype(F32)

    heads = range(N_HEADS)
    hsl = [slice(h * HEAD_DIM, (h + 1) * HEAD_DIM) for h in heads]
    for h in heads:
        hs = hsl[h]
        gm = _dot3_nt(jnp.concatenate([at[:, hs], rt[:, hs]], axis=0),
                      jnp.concatenate([bh[:, hs], kh[:, hs]], axis=0))
        n1 = jnp.where(strict, gm[:RBLK, :RBLK], 0.0)
        np_s[h] = n1
        ti_s[h] = eye + n1
        ak_s[h] = jnp.where(strict, gm[:RBLK, RBLK:], 0.0)
        rb_s[h] = jnp.where(incl, gm[RBLK:, :RBLK], 0.0)
        rk_s[h] = jnp.where(incl, gm[RBLK:, RBLK:], 0.0)
        bt_s[h] = bt[:, hs]
        kt_s[h] = kt[:, hs]
        v_s[h] = v[:, hs]
        gc_s[h] = g_all[:, hs]
    for _ in range(int(math.log2(RC)) - 1):
        for h in heads:
            npow = np_s[h]
            np_s[h] = _dot3(npow, npow)
        for h in heads:
            tinv = ti_s[h]
            ti_s[h] = tinv + _dot3(np_s[h], tinv)
    for h in heads:
        uv_s[h] = _dot3(ak_s[h], v_s[h])
    for h in heads:
        wcat = _dot3(ti_s[h], jnp.concatenate([at[:, hsl[h]], uv_s[h]], axis=1))
        wa_s[h] = wcat[:, :HEAD_DIM]
        uv_s[h] = wcat[:, HEAD_DIM:]
    for h in heads:
        ry = _dot3(rb_s[h], jnp.concatenate([wa_s[h], uv_s[h]], axis=1))
        rq_s[h] = rt[:, hsl[h]] + ry[:, :HEAD_DIM]
        yv_s[h] = ry[:, HEAD_DIM:] + _dot3(rk_s[h], v_s[h])

    def chunk_body(c, _):
        c0 = pl.multiple_of(c * RC, RC)
        sl = pl.ds(c0, RC)
        s_old = [s0_ref[c, h] if not chain else s_ref[h] for h in heads]
        res = [_dot_nt(jnp.concatenate([wa_s[h, sl, :], rq_s[h, sl, :]], axis=0).astype(BF16),
                       s_old[h].astype(BF16)) for h in heads]
        for h in heads:
            y_s[h, sl, :] = res[h][RC:] + yv_s[h, sl, :]
        upd = [lax.dot_general(jnp.concatenate([res[h][:RC] + uv_s[h, sl, :], v_s[h, sl, :]], axis=0).astype(BF16),
                               jnp.concatenate([bt_s[h, sl, :], kt_s[h, sl, :]], axis=0).astype(BF16),
                               (((0,), (0,)), ((), ())), preferred_element_type=F32) for h in heads]
        for h in heads:
            s_new = s_old[h] * gc_s[h, pl.ds(c0, 1), :] + upd[h]
            if chain:
                s_ref[h] = s_new
            else:
                sfin_ref[c, h] = s_new
        return 0

    lax.fori_loop(0, RNB, chunk_body, 0)
    if chain:
        sfin_ref[...] = s_ref[...]

    for h in range(N_HEADS):
        hs = slice(h * HEAD_DIM, (h + 1) * HEAD_DIM)
        y = y_s[h]
        mean = jnp.mean(y, axis=-1, keepdims=True)
        var = jnp.mean(jnp.square(y - mean), axis=-1, keepdims=True)
        yn = (y - mean) * lax.rsqrt(var + GN_EPS) * lng_ref[:, hs] + lnb_ref[:, hs]
        bonus = jnp.sum(r[:, hs] * k2[:, hs] * rk_ref[:, hs], axis=-1, keepdims=True) * v[:, hs]
        out_ref[:, hs] = (yn + bonus) * g[:, hs]


def _rwkv_consts():
    idx = np.arange(RBLK)
    same = (idx[:, None] // RC) == (idx[None, :] // RC)
    lt = (same & (idx[None, :] <= idx[:, None])).astype(np.float32)
    bo = same.astype(np.float32)
    lane = np.arange(GROUP_WIDTH)
    bd = ((lane[:, None] // HEAD_DIM) == (lane[None, :] // HEAD_DIM)).astype(np.float32)
    return jnp.asarray(lt, BF16), jnp.asarray(bo, BF16), jnp.asarray(bd, BF16)


def _rwkv(p, lw, chain, pprev=None, s0=None, t_valid=RC):
    rows = p.shape[0]
    nblk = rows // RBLK
    w = GROUP_WIDTH
    vec = lambda a: a.reshape(1, -1)
    consts = [vec(lw['mu_shift']), vec(lw['w0']), lw['w_decay2'].astype(BF16), vec(lw['a0']),
              lw['w_a2'].astype(BF16), lw['w_g2'].astype(BF16), vec(lw['k_k']), vec(lw['k_a']), vec(lw['r_k']),
              vec(lw['lnx_g']), vec(lw['lnx_b'])] + list(_rwkv_consts())
    full = lambda a: pl.BlockSpec(a.shape, lambda i: (0,) * a.ndim)
    blk = pl.BlockSpec((RBLK, RWKV_COLS), lambda i: (i, 0))
    hshape = (N_HEADS, HEAD_DIM, HEAD_DIM)
    if chain:
        ins, in_specs = [p], [blk]
        sfin_spec = pl.BlockSpec(hshape, lambda i: (0, 0, 0))
        sfin_shape = jax.ShapeDtypeStruct(hshape, F32)
    else:
        sspec = pl.BlockSpec((RNB,) + hshape, lambda i: (i, 0, 0, 0))
        ins, in_specs = [p, pprev, s0], [blk, blk, sspec]
        sfin_spec = sspec
        sfin_shape = jax.ShapeDtypeStruct((nblk * RNB,) + hshape, F32)
    hm = lambda: pltpu.VMEM((N_HEADS, RBLK, HEAD_DIM), F32)
    return pl.pallas_call(
        functools.partial(_rwkv_kernel, chain=chain, t_valid=t_valid),
        grid=(nblk,),
        in_specs=in_specs + [full(c) for c in consts],
        out_specs=[pl.BlockSpec((RBLK, w), lambda i: (i, 0)), sfin_spec],
        out_shape=[jax.ShapeDtypeStruct((rows, w), F32), sfin_shape],
        scratch_shapes=[pltpu.VMEM((1, RWKV_COLS), F32), pltpu.VMEM(hshape, F32)] + [hm() for _ in range(9)]
        + [pltpu.VMEM((N_HEADS, RBLK, RBLK), F32) for _ in range(5)],
        compiler_params=_cparams(("arbitrary",)),
    )(*ins, *consts)


RT_TM = 256


def _route_kernel(x_ref, ro_ref, ao_ref, wor_ref, woa_ref, g1_ref, sh_ref, sc_ref, n2_ref, wr_ref, br_ref,
                  ltri_ref, x1_ref, h2_ref, idx_ref, gate_ref, rank_ref, cnt_ref, run_ref):
    @pl.when(pl.program_id(0) == 0)
    def _():
        run_ref[...] = jnp.zeros(run_ref.shape, F32)

    tm = x_ref.shape[0]
    mix = _dot(ro_ref[...].astype(BF16), wor_ref[...]) + _dot(ao_ref[...].astype(BF16), woa_ref[...])
    x1 = x_ref[...] + g1_ref[...] * mix
    x1_ref[...] = x1
    y = x1 * lax.rsqrt(jnp.mean(x1 * x1, axis=-1, keepdims=True) + RMS_EPS) * n2_ref[...]
    h2 = (y * (1.0 + sc_ref[...]) + sh_ref[...]).astype(BF16)
    h2_ref[...] = h2
    logits = _dot(h2, wr_ref[...]) + br_ref[...]
    lane = lax.broadcasted_iota(I32, (tm, LANES), 1)
    lane_f = lane.astype(F32)
    lg = logits
    vals, idxs = [], []
    for _ in range(TOP_K):
        m = jnp.max(lg, axis=1, keepdims=True)
        idx = jnp.min(jnp.where(lg == m, lane_f, float(LANES)), axis=1, keepdims=True)
        vals.append(m)
        idxs.append(idx)
        lg = jnp.where(lane_f == idx, -3e38, lg)
    es = [jnp.exp(v - vals[0]) for v in vals]
    den = es[0] + es[1] + es[2] + es[3]
    ohs = [(lane_f == idx).astype(F32) for idx in idxs]
    oh_all = ohs[0] + ohs[1] + ohs[2] + ohs[3]
    base = run_ref[...] + _dot(ltri_ref[...], oh_all.astype(BF16))
    idx_out = jnp.zeros((tm, LANES), F32)
    gate_out = jnp.zeros((tm, LANES), F32)
    rank_out = jnp.zeros((tm, LANES), F32)
    for kk in range(TOP_K):
        rank = jnp.sum(ohs[kk] * base, axis=1, keepdims=True)
        idx_out = jnp.where(lane == kk, idxs[kk], idx_out)
        gate_out = jnp.where(lane == kk, es[kk] / den, gate_out)
        rank_out = jnp.where(lane == kk, rank, rank_out)
    idx_ref[...] = idx_out[:, :TOP_K].astype(I32)
    gate_ref[...] = gate_out[:, :TOP_K]
    rank_ref[...] = rank_out[:, :TOP_K].astype(I32)
    run_ref[...] = run_ref[...] + jnp.sum(oh_all, axis=0, keepdims=True)
    cnt_ref[...] = run_ref[...]


def _out_proj_route(x, ro, ao, w_out, g1, sh2, sc2, norm2_g, w_router, b_router):
    t, d = x.shape
    tm = min(RT_TM, t)
    w = GROUP_WIDTH
    per_row = g1.shape[0] != 1
    wor = w_out[:w].astype(BF16)
    woa = w_out[w:].astype(BF16)
    wr = jnp.pad(w_router, ((0, 0), (0, LANES - N_EXPERTS))).astype(BF16)
    br = jnp.pad(b_router.reshape(1, -1), ((0, 0), (0, LANES - N_EXPERTS)), constant_values=NEG_BIG)
    ltri = jnp.asarray(np.tril(np.ones((tm, tm), np.float32), -1), BF16)
    mod_spec = pl.BlockSpec((tm, d), lambda i: (i, 0)) if per_row else pl.BlockSpec((1, d), lambda i: (0, 0))
    row = lambda n: pl.BlockSpec((tm, n), lambda i: (i, 0))
    full = lambda a: pl.BlockSpec(a.shape, lambda i: (0, 0))
    return pl.pallas_call(
        _route_kernel,
        grid=(t // tm,),
        in_specs=[row(d), row(w), row(w), full(wor), full(woa), mod_spec, mod_spec, mod_spec,
                  pl.BlockSpec((1, d), lambda i: (0, 0)), full(wr), full(br), full(ltri)],
        out_specs=[row(d), row(d), row(TOP_K), row(TOP_K), row(TOP_K), pl.BlockSpec((1, LANES), lambda i: (0, 0))],
        out_shape=[jax.ShapeDtypeStruct((t, d), F32), jax.ShapeDtypeStruct((t, d), BF16),
                   jax.ShapeDtypeStruct((t, TOP_K), I32), jax.ShapeDtypeStruct((t, TOP_K), F32),
                   jax.ShapeDtypeStruct((t, TOP_K), I32), jax.ShapeDtypeStruct((1, LANES), F32)],
        scratch_shapes=[pltpu.VMEM((1, LANES), F32)],
        compiler_params=_cparams(("arbitrary",)),
    )(x, ro, ao, wor, woa, g1, sh2, sc2, norm2_g.reshape(1, d), wr, br, ltri)


EX_TM = 256
BF16_ROW = (SUBLANES, LANES)


def _dispatch_kernel(slot_ref, h_ref, init_ref, out_ref, sem):
    del init_ref
    tm = h_ref.shape[0]

    def body(r, _):
        for kk in range(TOP_K):
            pltpu.make_async_copy(h_ref.at[r], out_ref.at[slot_ref[r * TOP_K + kk]], sem).start()
        return 0

    lax.fori_loop(0, tm, body, 0)
    for _ in range(TOP_K):
        pltpu.make_async_copy(h_ref, out_ref.at[pl.ds(0, tm)], sem).wait()


def _dispatch(h2, slot, rows_sorted):
    t, d = h2.shape
    tm = min(RT_TM, t)
    assert d == SUBLANES * LANES
    h3 = h2.reshape((t,) + BF16_ROW)
    return pl.pallas_call(
        _dispatch_kernel,
        grid=(t // tm,),
        in_specs=[pl.BlockSpec((tm * TOP_K,), lambda i: (i,), memory_space=pltpu.SMEM),
                  pl.BlockSpec((tm,) + BF16_ROW, lambda i: (i, 0, 0)),
                  pl.BlockSpec(memory_space=pl.ANY)],
        out_specs=pl.BlockSpec(memory_space=pl.ANY),
        out_shape=jax.ShapeDtypeStruct(rows_sorted.shape, rows_sorted.dtype),
        scratch_shapes=[pltpu.SemaphoreType.DMA(())],
        input_output_aliases={2: 0},
        compiler_params=_cparams(("arbitrary",)),
    )(slot.reshape(-1), h3, rows_sorted)


def _expert_kernel(be_ref, nv_ref, x_ref, wgu_ref, bgu_ref, wd_ref, bd_ref, o_ref, wgu_bf, wd_bf):
    i = pl.program_id(0)
    changed = jnp.logical_or(i == 0, be_ref[i] != be_ref[jnp.maximum(i - 1, 0)])

    @pl.when(changed)
    def _():
        wgu_bf[...] = wgu_ref[0].astype(BF16)
        wd_bf[...] = wd_ref[0].astype(BF16)

    @pl.when(i < nv_ref[0])
    def _():
        f = wd_bf.shape[0]
        gu = _dot(x_ref[...], wgu_bf[...]) + bgu_ref[0]
        glu = jnp.minimum(gu[:, :f], SWIGLU_LIMIT)
        lin = jnp.clip(gu[:, f:], -SWIGLU_LIMIT, SWIGLU_LIMIT)
        act = glu * jax.nn.sigmoid(SWIGLU_ALPHA * glu) * (lin + 1.0)
        o_ref[...] = _dot(act.astype(BF16), wd_bf[...]) + bd_ref[0]

    @pl.when(i >= nv_ref[0])
    def _():
        o_ref[...] = jnp.zeros(o_ref.shape, F32)


def _experts(rows_sorted, blk_e, n_valid, w_gu, b_gu, w_down, b_down):
    ns, d = rows_sorted.shape
    e, _, f2 = w_gu.shape
    f = f2 // 2
    grid_spec = pltpu.PrefetchScalarGridSpec(
        num_scalar_prefetch=2,
        grid=(ns // EX_TM,),
        in_specs=[pl.BlockSpec((EX_TM, d), lambda i, be, nv: (i, 0)),
                  pl.BlockSpec((1, d, f2), lambda i, be, nv: (be[i], 0, 0)),
                  pl.BlockSpec((1, 1, f2), lambda i, be, nv: (be[i], 0, 0)),
                  pl.BlockSpec((1, f, d), lambda i, be, nv: (be[i], 0, 0)),
                  pl.BlockSpec((1, 1, d), lambda i, be, nv: (be[i], 0, 0))],
        out_specs=pl.BlockSpec((EX_TM, d), lambda i, be, nv: (i, 0)),
        scratch_shapes=[pltpu.VMEM((d, f2), BF16), pltpu.VMEM((f, d), BF16)])
    return pl.pallas_call(
        _expert_kernel,
        grid_spec=grid_spec,
        out_shape=jax.ShapeDtypeStruct((ns, d), F32),
        compiler_params=_cparams(("arbitrary",)),
    )(blk_e, n_valid, rows_sorted, w_gu, b_gu.reshape(e, 1, f2), w_down, b_down.reshape(e, 1, d))


def _combine_kernel(slot_ref, rows_ref, gate_ref, x1_ref, g2_ref, nf_ref, y_ref, buf, sem):
    tm = x1_ref.shape[0]

    def body(r, _):
        for kk in range(TOP_K):
            pltpu.make_async_copy(rows_ref.at[pl.ds(slot_ref[r * TOP_K + kk], 1)],
                                  buf.at[kk, pl.ds(r, 1)], sem).start()
        return 0

    lax.fori_loop(0, tm, body, 0)
    for kk in range(TOP_K):
        pltpu.make_async_copy(rows_ref.at[pl.ds(0, tm)], buf.at[kk], sem).wait()
    gates = gate_ref[...]
    moe = gates[:, 0:1] * buf[0]
    for kk in range(1, TOP_K):
        moe = moe + gates[:, kk:kk + 1] * buf[kk]
    x2 = x1_ref[...] + g2_ref[...] * moe
    y_ref[...] = x2 * lax.rsqrt(jnp.mean(x2 * x2, axis=-1, keepdims=True) + RMS_EPS) * nf_ref[...]


def _combine(out_rows, slot, gates, x1, g2, normf_g):
    t, d = x1.shape
    tm = min(RT_TM, t)
    per_row = g2.shape[0] != 1
    mod_spec = pl.BlockSpec((tm, d), lambda i: (i, 0)) if per_row else pl.BlockSpec((1, d), lambda i: (0, 0))
    return pl.pallas_call(
        _combine_kernel,
        grid=(t // tm,),
        in_specs=[pl.BlockSpec((tm * TOP_K,), lambda i: (i,), memory_space=pltpu.SMEM),
                  pl.BlockSpec(memory_space=pl.ANY),
                  pl.BlockSpec((tm, TOP_K), lambda i: (i, 0)),
                  pl.BlockSpec((tm, d), lambda i: (i, 0)), mod_spec,
                  pl.BlockSpec((1, d), lambda i: (0, 0))],
        out_specs=pl.BlockSpec((tm, d), lambda i: (i, 0)),
        out_shape=jax.ShapeDtypeStruct((t, d), F32),
        scratch_shapes=[pltpu.VMEM((TOP_K, tm, d), F32), pltpu.SemaphoreType.DMA(())],
        compiler_params=_cparams(("arbitrary",)),
    )(slot.reshape(-1), out_rows, gates, x1, g2, normf_g.reshape(1, d))


def _moe_plan(idx_p, rank_p, cnt_p, idx_s, rank_s, cnt_s):
    n_assign = idx_p.size + idx_s.size
    n_tiles = -(-n_assign // EX_TM) + N_EXPERTS
    cp = cnt_p[0, :N_EXPERTS].astype(I32)
    cs = cnt_s[0, :N_EXPERTS].astype(I32)
    padded = (cp + cs + EX_TM - 1) // EX_TM * EX_TM
    pad_end = jnp.cumsum(padded)
    pad_start = pad_end - padded
    slot_p = pad_start[idx_p] + rank_p
    slot_s = pad_start[idx_s] + cp[idx_s] + rank_s
    blk_e = jnp.minimum(jnp.searchsorted(pad_end, jnp.arange(n_tiles, dtype=I32) * EX_TM, side='right'),
                        N_EXPERTS - 1).astype(I32)
    n_valid = (pad_end[-1:] // EX_TM).astype(I32)
    return slot_p, slot_s, blk_e, n_valid, n_tiles * EX_TM


PG = 8
QROWS = N_HEADS * SUBLANES


def _page_specs(block, n_pages):
    def spec(u):
        return pl.BlockSpec(block, lambda b, j, pt: (pt[b * n_pages + j * PG + u],) + (0,) * (len(block) - 1))
    return [spec(u) for u in range(PG)]


def _head_sum(x):
    out = x[:SUBLANES]
    for h in range(1, N_HEADS):
        out = out + x[h * SUBLANES:(h + 1) * SUBLANES]
    return out


def _sample_index_kernel(pt_ref, qi_ref, wrep_ref, kin_ref, *rest, n_pages, t_new, topk):
    del pt_ref
    pages = rest[:PG]
    keys_ref, tau_ref = rest[PG:]
    j = pl.program_id(1)
    qi = qi_ref[0]
    wrep = wrep_ref[0]

    def scores(ki_t_bf):
        s = _dot(qi, ki_t_bf)
        return _head_sum(jnp.maximum(s, 0.0) * wrep)

    for u in range(PG):
        keys_ref[0, j * PG + u] = _float_key(scores(pages[u][0].astype(BF16)))

    @pl.when(j == pl.num_programs(1) - 1)
    def _():
        qrow = lax.broadcasted_iota(I32, (SUBLANES, PAGE_SIZE), 0)
        col = lax.broadcasted_iota(I32, (SUBLANES, PAGE_SIZE), 1)
        ok = jnp.logical_and(col <= qrow, col < t_new)
        keys_ref[0, n_pages] = jnp.where(ok, _float_key(scores(kin_ref[0])), KEY_NEG_INF)

        def count_ge(mid):
            c = jnp.sum(jnp.where(keys_ref[0] >= mid[None], 1.0, 0.0), axis=0)
            return jnp.broadcast_to(jnp.sum(c, axis=1, keepdims=True), (SUBLANES, LANES))

        tau_ref[0] = _kth_largest_key(count_ge, jnp.full((SUBLANES, LANES), KEY_NEG_INF + 1, I32),
                                      jnp.full((SUBLANES, LANES), 0x7F800000, I32), topk)


def _sample_attend_kernel(pt_ref, q_ref, keys_ref, tau_ref, band_ref, kn_ref, vn_ref, *rest, n_pages):
    del pt_ref
    kpages = rest[:PG]
    vpages = rest[PG:2 * PG]
    o_ref, m_ref, l_ref, acc_ref = rest[2 * PG:]
    j = pl.program_id(1)
    last = j == pl.num_programs(1) - 1

    @pl.when(j == 0)
    def _():
        m_ref[...] = jnp.full(m_ref.shape, NEG_BIG, F32)
        l_ref[...] = jnp.zeros(l_ref.shape, F32)
        acc_ref[...] = jnp.zeros(acc_ref.shape, F32)

    tau = tau_ref[0]
    heads = range(N_HEADS)

    def attend(k_of, v_of, key_tiles, bias_of):
        pages = range(len(key_tiles))
        sel = [kt >= tau for kt in key_tiles]
        s = [[_dot(q_ref[0, h], k_of(u, h)) for u in pages] for h in heads]
        ps, alphas = [], []
        for h in heads:
            sh = []
            for u in pages:
                b_uh = bias_of(u, h)
                sh.append(jnp.where(sel[u], s[h][u] if b_uh is None else s[h][u] + b_uh, NEG_BIG))
            tile_max = functools.reduce(jnp.maximum, sh)
            m_old = m_ref[h]
            m_new = jnp.maximum(m_old, jnp.broadcast_to(jnp.max(tile_max, axis=1, keepdims=True), m_old.shape))
            p = [jnp.exp(x - m_new) for x in sh]
            alpha = jnp.exp(m_old - m_new)
            p_sum = functools.reduce(jnp.add, p)
            l_ref[h] = alpha * l_ref[h] + jnp.broadcast_to(jnp.sum(p_sum, axis=1, keepdims=True), m_old.shape)
            m_ref[h] = m_new
            ps.append([x.astype(BF16) for x in p])
            alphas.append(alpha[:, :HEAD_DIM])
        pv = [[_dot_nt(ps[h][u], v_of(u, h)) for u in pages] for h in heads]
        for h in heads:
            acc_ref[h] = acc_ref[h] * alphas[h] + functools.reduce(jnp.add, pv[h])

    attend(lambda u, h: kpages[u][0, h].astype(BF16), lambda u, h: vpages[u][0, h].astype(BF16),
           [keys_ref[0, j * PG + u] for u in range(PG)],
           lambda u, h: jnp.where(last, band_ref[h, :, :PAGE_SIZE], 0.0) if u == PG - 1 else None)

    @pl.when(last)
    def _():
        hsl = lambda h: slice(h * HEAD_DIM, (h + 1) * HEAD_DIM)
        attend(lambda u, h: kn_ref[0, hsl(h), :], lambda u, h: vn_ref[0, hsl(h), :], [keys_ref[0, n_pages]],
               lambda u, h: band_ref[h, :, PAGE_SIZE:])
        for h in heads:
            o_ref[0, :, hsl(h)] = acc_ref[h] / l_ref[h][:, :HEAD_DIM]


def _sample_attention(qs, qib, wi, k_new_bf, v_new_bf, ki_new_bf, cache_k, cache_v, cache_kidx, page_table,
                      rel_bias, b, t_new):
    w = GROUP_WIDTH
    n_pages = page_table.shape[1]
    assert n_pages % PG == 0 and t_new <= SUBLANES and cache_k.shape[1] == PAGE_SIZE
    past = n_pages * PAGE_SIZE
    topk = min(TOPK_MAX, (past + t_new) // 4)
    pt = page_table.reshape(-1)
    padq = lambda a: jnp.pad(a, ((0, 0), (0, SUBLANES - t_new)) + ((0, 0),) * (a.ndim - 2))
    padk = lambda a: jnp.pad(a.reshape(b, t_new, -1), ((0, 0), (0, PAGE_SIZE - t_new), (0, 0))).transpose(0, 2, 1)
    qi_r = padq(qib.reshape(b, t_new, N_IDX_HEADS, IDX_DIM)).transpose(0, 2, 1, 3).reshape(b, QROWS, IDX_DIM)
    w_r = padq(wi.reshape(b, t_new, N_IDX_HEADS)).transpose(0, 2, 1).reshape(b, QROWS, 1)
    w_r = jnp.broadcast_to(w_r, (b, QROWS, LANES))
    q4 = padq(qs.reshape(b, t_new, N_HEADS, HEAD_DIM)).transpose(0, 2, 1, 3)
    kin, kn, vn = padk(ki_new_bf), padk(k_new_bf), padk(v_new_bf)
    band = _bias_band(rel_bias, SUBLANES, 2 * PAGE_SIZE, 1, PAGE_SIZE)
    ck_t = cache_k.transpose(0, 2, 3, 1)
    cv_t = cache_v.transpose(0, 2, 3, 1)
    cki_t = cache_kidx.transpose(0, 2, 1)
    page_block = (1, N_HEADS, HEAD_DIM, PAGE_SIZE)

    per_b = lambda shape: pl.BlockSpec((1,) + shape, lambda bb, j, p_: (bb,) + (0,) * len(shape))
    steps = n_pages // PG
    keys, tau = pl.pallas_call(
        functools.partial(_sample_index_kernel, n_pages=n_pages, t_new=t_new, topk=topk),
        grid_spec=pltpu.PrefetchScalarGridSpec(
            num_scalar_prefetch=1, grid=(b, steps),
            in_specs=[per_b((QROWS, IDX_DIM)), per_b((QROWS, LANES)), per_b((IDX_DIM, PAGE_SIZE))]
            + _page_specs((1, IDX_DIM, PAGE_SIZE), n_pages),
            out_specs=[per_b((n_pages + 1, SUBLANES, PAGE_SIZE)), per_b((SUBLANES, LANES))]),
        out_shape=[jax.ShapeDtypeStruct((b, n_pages + 1, SUBLANES, PAGE_SIZE), I32),
                   jax.ShapeDtypeStruct((b, SUBLANES, LANES), I32)],
        compiler_params=_cparams(("arbitrary", "arbitrary")),
    )(pt, qi_r, w_r, kin, *([cki_t] * PG))
    out = pl.pallas_call(
        functools.partial(_sample_attend_kernel, n_pages=n_pages),
        grid_spec=pltpu.PrefetchScalarGridSpec(
            num_scalar_prefetch=1, grid=(b, steps),
            in_specs=[per_b((N_HEADS, SUBLANES, HEAD_DIM)), per_b((n_pages + 1, SUBLANES, PAGE_SIZE)),
                      per_b((SUBLANES, LANES)), pl.BlockSpec(band.shape, lambda bb, j, p_: (0, 0, 0)),
                      per_b((w, PAGE_SIZE)), per_b((w, PAGE_SIZE))]
            + _page_specs(page_block, n_pages) + _page_specs(page_block, n_pages),
            out_specs=per_b((SUBLANES, w)),
            scratch_shapes=[pltpu.VMEM((N_HEADS, SUBLANES, LANES), F32), pltpu.VMEM((N_HEADS, SUBLANES, LANES), F32),
                            pltpu.VMEM((N_HEADS, SUBLANES, HEAD_DIM), F32)]),
        out_shape=jax.ShapeDtypeStruct((b, SUBLANES, w), F32),
        compiler_params=_cparams(("arbitrary", "arbitrary")),
    )(pt, q4, keys, tau, band, kn, vn, *([ck_t] * PG), *([cv_t] * PG))
    return out[:, :t_new].reshape(b * t_new, w)


def kernel(x_prompt, x_sample, c_prompt, c_sample, cache_k, cache_v, cache_kidx, page_table, state_wkv,
           state_shift, w_ada, b_ada, norm1_g, w_in, mu_shift, w0, w_decay2, a0, w_a2, w_g2, k_k, k_a, r_k,
           lnx_g, lnx_b, rel_bias, w_out, norm2_g, w_router, b_router, w_gu, b_gu, w_down, b_down, normf_g):
    depth = w_in.shape[0]
    assert depth == 1, "the merged prompt+sample expert pass is written for a single layer"
    bp, sp, d = x_prompt.shape
    bs, ts, _ = x_sample.shape
    assert bp == 1 and sp % RBLK == 0 and bs % RNB == 0 and ts <= RC
    l = 0
    lw = {'mu_shift': mu_shift[l], 'w0': w0[l], 'w_decay2': w_decay2[l], 'a0': a0[l], 'w_a2': w_a2[l],
          'w_g2': w_g2[l], 'k_k': k_k[l], 'k_a': k_a[l], 'r_k': r_k[l], 'lnx_g': lnx_g[l], 'lnx_b': lnx_b[l]}
    ns = bs * ts

    c_all = jnp.concatenate([c_prompt, c_sample], axis=0)
    n_c = c_all.shape[0]
    c_all = jnp.pad(c_all, ((0, -n_c % SUBLANES), (0, 0)))
    mod = _adaln(c_all, w_ada[l], b_ada[l])
    mods_p = [mod[0:1, i * d:(i + 1) * d] for i in range(6)]
    mods_s = [jnp.repeat(mod[bp:bp + bs, i * d:(i + 1) * d], ts, axis=0) for i in range(6)]

    xp = x_prompt.reshape(sp, d)
    xs = x_sample.reshape(ns, d)
    pr_p, k_p, v_p, ki_p, wi_p, qs_p, kb_p, _, qib_p, kib_p, vt_p = _in_proj(xp, norm1_g[l], mods_p[0],
                                                                             mods_p[1], w_in[l])
    pr_s, k_s, v_s, ki_s, wi_s, qs_s, kb_s, vb_s, qib_s, kib_s, _ = _in_proj(xs, norm1_g[l], mods_s[0],
                                                                             mods_s[1], w_in[l])

    ro_p, wkv_p = _rwkv(pr_p, lw, chain=True)
    att_p = _prompt_attention(qs_p, qib_p, wi_p, kib_p, kb_p, vt_p, rel_bias)

    pr_s3 = pr_s.reshape(bs, ts, RWKV_COLS)
    prev_s3 = jnp.concatenate([state_shift[l][:, None, :], pr_s3[:, :-1]], axis=1)
    padc = lambda a: jnp.pad(a, ((0, 0), (0, RC - ts), (0, 0))).reshape(bs * RC, RWKV_COLS)
    ro_s, wkv_s = _rwkv(padc(pr_s3), lw, chain=False, pprev=padc(prev_s3), s0=state_wkv[l], t_valid=ts)
    ro_s = ro_s.reshape(bs, RC, GROUP_WIDTH)[:, :ts].reshape(ns, GROUP_WIDTH)
    att_s = _sample_attention(qs_s, qib_s, wi_s, kb_s, vb_s, kib_s, cache_k[l], cache_v[l], cache_kidx[l],
                              page_table, rel_bias, bs, ts)

    x1_p, h2_p, idx_p, gate_p, rank_p, cnt_p = _out_proj_route(xp, ro_p, att_p, w_out[l], mods_p[2], mods_p[3],
                                                               mods_p[4], norm2_g[l], w_router[l], b_router[l])
    x1_s, h2_s, idx_s, gate_s, rank_s, cnt_s = _out_proj_route(xs, ro_s, att_s, w_out[l], mods_s[2], mods_s[3],
                                                               mods_s[4], norm2_g[l], w_router[l], b_router[l])
    slot_p, slot_s, blk_e, n_valid, n_slots = _moe_plan(idx_p, rank_p, cnt_p, idx_s, rank_s, cnt_s)
    rows = jnp.zeros((n_slots,) + BF16_ROW, BF16)
    rows = _dispatch(h2_p, slot_p, rows)
    rows = _dispatch(h2_s, slot_s, rows)
    out_rows = _experts(rows.reshape(n_slots, d), blk_e, n_valid, w_gu[l], b_gu[l], w_down[l], b_down[l])
    y_p = _combine(out_rows, slot_p, gate_p, x1_p, mods_p[5], normf_g)
    y_s = _combine(out_rows, slot_s, gate_s, x1_s, mods_s[5], normf_g)

    hd = (N_HEADS, HEAD_DIM)
    return (y_p.reshape(bp, sp, d), y_s.reshape(bs, ts, d),
            k_p.reshape((1, bp, sp) + hd), v_p.reshape((1, bp, sp) + hd), ki_p.reshape(1, bp, sp, IDX_DIM),
            wkv_p.reshape((1, bp) + (N_HEADS, HEAD_DIM, HEAD_DIM)), pr_p[sp - 1:sp].reshape(1, bp, RWKV_COLS),
            k_s.reshape((1, bs, ts) + hd), v_s.reshape((1, bs, ts) + hd), ki_s.reshape(1, bs, ts, IDX_DIM),
            wkv_s.reshape((1, bs) + (N_HEADS, HEAD_DIM, HEAD_DIM)), pr_s3[:, ts - 1].reshape(1, bs, RWKV_COLS))
```

```python
import functools
import math

import jax
import jax.numpy as jnp
import numpy as np
from jax import lax
from jax.experimental import pallas as pl
from jax.experimental.pallas import tpu as pltpu

F32 = jnp.float32
BF16 = jnp.bfloat16
I32 = jnp.int32

HEAD_DIM = 64
N_HEADS = 8
GROUP_WIDTH = N_HEADS * HEAD_DIM
DECAY_LORA, AAA_LORA, GATE_LORA = 64, 64, 128
RWKV_COLS = 3 * GROUP_WIDTH + DECAY_LORA + AAA_LORA + GATE_LORA
IDX_DIM = 64
N_IDX_HEADS = 8
TOPK_MAX = 256
N_BUCKETS = 32
MAX_DISTANCE = 128
N_EXPERTS = 32
TOP_K = 4
SWIGLU_LIMIT = 7.0
SWIGLU_ALPHA = 1.702
RMS_EPS = 1e-6
GN_EPS = HEAD_DIM * 1e-5
PAGE_SIZE = 128

LANES = 128
SUBLANES = 8
VMEM_LIMIT = 56 * 1024 * 1024

NEG_BIG = -1e30
KEY_NEG_INF = -2139095041
TIE_CODE = 0x7F800001


def _cparams(sem):
    return pltpu.CompilerParams(dimension_semantics=sem, vmem_limit_bytes=VMEM_LIMIT)


def _dot(a, b):
    return jnp.dot(a, b, preferred_element_type=F32)


def _dot_nt(a, b):
    return lax.dot_general(a, b, (((1,), (1,)), ((), ())), preferred_element_type=F32)


def _split2(a):
    hi = a.astype(BF16)
    lo = (a - hi.astype(F32)).astype(BF16)
    return hi, lo


def _split3(a):
    hi = a.astype(BF16)
    r1 = a - hi.astype(F32)
    mid = r1.astype(BF16)
    lo = (r1 - mid.astype(F32)).astype(BF16)
    return hi, mid, lo


def _rep(x, n, axis):
    return jnp.concatenate([x] * n, axis=axis)


def _float_key(x):
    b = pltpu.bitcast(x, I32)
    return b ^ ((b >> 31) & 0x7FFFFFFF)


def _ada_kernel(c_ref, w_ref, b_ref, o_ref):
    c = c_ref[...]
    s = c * jax.nn.sigmoid(c)
    o_ref[...] = _dot(s.astype(BF16), w_ref[...].astype(BF16)) + b_ref[...]


def _adaln(c, w_ada, b_ada):
    r, d = c.shape
    n = w_ada.shape[1]
    tn = 1536
    return pl.pallas_call(
        _ada_kernel,
        grid=(n // tn,),
        in_specs=[pl.BlockSpec((r, d), lambda j: (0, 0)),
                  pl.BlockSpec((d, tn), lambda j: (0, j)),
                  pl.BlockSpec((1, tn), lambda j: (0, j))],
        out_specs=pl.BlockSpec((r, tn), lambda j: (0, j)),
        out_shape=jax.ShapeDtypeStruct((r, n), F32),
        compiler_params=_cparams(("arbitrary",)),
    )(c, w_ada, b_ada.reshape(1, n))


def _inproj_kernel(x_ref, g_ref, sh_ref, sc_ref, wr_ref, wa_ref, wk_ref,
                   pr_ref, k_ref, v_ref, ki_ref, wi_ref, qs_ref, kb_ref, vb_ref, qib_ref, kib_ref, vt_ref):
    x = x_ref[...]
    y = x * lax.rsqrt(jnp.mean(x * x, axis=-1, keepdims=True) + RMS_EPS) * g_ref[...]
    h = (y * (1.0 + sc_ref[...]) + sh_ref[...]).astype(BF16)
    pr_ref[...] = _dot(h, wr_ref[...])
    a = _dot(h, wa_ref[...])
    w = GROUP_WIDTH
    q, k, v, qi = a[:, :w], a[:, w:2 * w], a[:, 2 * w:3 * w], a[:, 3 * w:4 * w]
    k_ref[...] = k
    v_ref[...] = v
    qs_ref[...] = (q * HEAD_DIM ** -0.5).astype(BF16)
    kb_ref[...] = k.astype(BF16)
    vb_ref[...] = v.astype(BF16)
    vt_ref[...] = v.T.astype(BF16)
    qib_ref[...] = (qi * IDX_DIM ** -0.5).astype(BF16)
    kw = _dot(h, wk_ref[...])
    ki = kw[:, :IDX_DIM]
    ki_ref[...] = ki
    kib_ref[...] = ki.astype(BF16)
    wi_ref[...] = kw[:, IDX_DIM:IDX_DIM + N_IDX_HEADS] * N_IDX_HEADS ** -0.5


def _mod_spec(mod, col, tm, d):
    if mod.shape[0] == 1:
        return pl.BlockSpec((1, d), lambda i: (0, col))
    return pl.BlockSpec((tm, d), lambda i: (i, col))


def _in_proj(x, norm_g, mod, w_in):
    t, d = x.shape
    tm = min(512, t)
    w = GROUP_WIDTH
    a0 = RWKV_COLS
    wr = w_in[:, :a0].astype(BF16)
    wa = w_in[:, a0:a0 + 4 * w].astype(BF16)
    wk = jnp.pad(w_in[:, a0 + 4 * w:], ((0, 0), (0, LANES - IDX_DIM - N_IDX_HEADS))).astype(BF16)
    row = lambda n: pl.BlockSpec((tm, n), lambda i: (i, 0))
    full = lambda a: pl.BlockSpec(a.shape, lambda i: (0, 0))
    sds = lambda n, dt: jax.ShapeDtypeStruct((t, n), dt)
    return pl.pallas_call(
        _inproj_kernel,
        grid=(t // tm,),
        in_specs=[row(d), pl.BlockSpec((1, d), lambda i: (0, 0)), _mod_spec(mod, 0, tm, d), _mod_spec(mod, 1, tm, d),
                  full(wr), full(wa), full(wk)],
        out_specs=[row(a0), row(w), row(w), row(IDX_DIM), row(N_IDX_HEADS), row(w), row(w), row(w), row(w),
                   row(IDX_DIM), pl.BlockSpec((w, tm), lambda i: (0, i))],
        out_shape=[sds(a0, F32), sds(w, F32), sds(w, F32), sds(IDX_DIM, F32), sds(N_IDX_HEADS, F32),
                   sds(w, BF16), sds(w, BF16), sds(w, BF16), sds(w, BF16), sds(IDX_DIM, BF16),
                   jax.ShapeDtypeStruct((w, t), BF16)],
        compiler_params=_cparams(("arbitrary",)),
    )(x, norm_g.reshape(1, d), mod, mod, wr, wa, wk)


def _bias_band_kernel(rb_ref, o_ref, *, sign, off):
    _, rows, cols = o_ref.shape
    r = lax.broadcasted_iota(I32, (rows, cols), 0)
    c = lax.broadcasted_iota(I32, (rows, cols), 1)
    n = jnp.maximum(sign * (r - c) + off, 0)
    max_exact = N_BUCKETS // 2
    nf = jnp.maximum(n, 1).astype(F32)
    large = max_exact + (jnp.log(nf / max_exact) / math.log(MAX_DISTANCE / max_exact)
                         * (N_BUCKETS - max_exact)).astype(I32)
    large = jnp.minimum(large, N_BUCKETS - 1)
    bucket = jnp.where(n < max_exact, n, large)
    for h in range(N_HEADS):
        far = rb_ref[N_BUCKETS - 1, h]
        acc = jnp.zeros((rows, cols), F32)
        for b in range(N_BUCKETS - 1):
            acc = jnp.where(bucket == b, rb_ref[b, h] - far, acc)
        o_ref[h] = acc


def _bias_band(rel_bias, rows, cols, sign, off):
    return pl.pallas_call(
        functools.partial(_bias_band_kernel, sign=sign, off=off),
        in_specs=[pl.BlockSpec(memory_space=pltpu.SMEM)],
        out_specs=pl.BlockSpec((N_HEADS, rows, cols), lambda: (0, 0, 0)),
        out_shape=jax.ShapeDtypeStruct((N_HEADS, rows, cols), F32),
    )(rel_bias)


def _kth_largest_key(count_ge, lo0, hi0, topk):
    def cond(carry):
        return carry[2] > 0

    def body(carry):
        lo, hi, _ = carry
        mid = (lo | hi) - ((lo ^ hi) >> 1)
        cnt = count_ge(mid)
        active = lo < hi
        exact = jnp.logical_and(active, cnt == topk)
        ge = cnt >= topk
        lo_n = jnp.where(active, jnp.where(ge, mid, lo), lo)
        hi_n = jnp.where(active, jnp.where(exact, mid, jnp.where(ge, hi, mid - 1)), hi)
        return lo_n, hi_n, jnp.max(jnp.where(lo_n < hi_n, 1.0, 0.0))

    lo, _, _ = lax.while_loop(cond, body, (lo0, hi0, jnp.max(jnp.where(lo0 < hi0, 1.0, 0.0))))
    return lo


PQB = 256
PKT = 512
PKB = 1024
PSUB = 256
CNT_ROWS = 64


def _prompt_att_kernel(qs_ref, qib_ref, wit_ref, kib_ref, kb_ref, vt_ref, band_ref, o_ref,
                       keys_ref, tau_ref, m_ref, l_ref, acc_ref, s_ref, p_ref, a_ref, *, topk, qb_rows):
    qb = pl.program_id(0)
    j = pl.program_id(1)
    q_lo = qb * qb_rows
    n_kt = (q_lo + qb_rows + PKT - 1) // PKT

    @pl.when(j == 0)
    def _index_phase():
        q_pos = q_lo + lax.broadcasted_iota(I32, (PKT, qb_rows), 1)

        def tile_body(kt, carry):
            k0 = pl.multiple_of(kt * PKT, PKT)
            ki = kib_ref[pl.ds(k0, PKT), :]
            acc = jnp.zeros((PKT, qb_rows), F32)
            for h in range(N_IDX_HEADS):
                s = _dot_nt(ki, qib_ref[:, h * IDX_DIM:(h + 1) * IDX_DIM])
                acc = acc + jnp.maximum(s, 0.0) * wit_ref[h:h + 1, :]
            k_pos = k0 + lax.broadcasted_iota(I32, (PKT, qb_rows), 0)
            causal = k_pos <= q_pos
            keys_ref[kt] = jnp.where(causal, _float_key(acc), KEY_NEG_INF)
            smax, smin = carry
            grp = lambda x: x.reshape(PKT // SUBLANES, SUBLANES, qb_rows)
            smax = jnp.maximum(smax, jnp.max(grp(jnp.where(causal, acc, -jnp.inf)), axis=0))
            smin = jnp.minimum(smin, jnp.min(grp(jnp.where(causal, acc, jnp.inf)), axis=0))
            return smax, smin

        smax, smin = lax.fori_loop(
            0, n_kt, tile_body,
            (jnp.full((SUBLANES, qb_rows), -jnp.inf, F32), jnp.full((SUBLANES, qb_rows), jnp.inf, F32)))
        fmax = jnp.max(smax, axis=0, keepdims=True)
        fmin = jnp.min(smin, axis=0, keepdims=True)
        hi0 = jnp.where(fmax == 0.0, 0, _float_key(fmax))
        lo0 = jnp.where(fmin == 0.0, -1, _float_key(fmin))
        n_causal = q_lo + 1 + lax.broadcasted_iota(I32, (1, qb_rows), 1)
        lo0 = jnp.where(n_causal < topk, KEY_NEG_INF + 1, lo0)
        hi0 = jnp.where(n_causal < topk, KEY_NEG_INF + 1, hi0)

        def count_ge(mid):
            def cbody(kt, c):
                for r0 in range(0, PKT, CNT_ROWS):
                    ge = jnp.where(keys_ref[kt, r0:r0 + CNT_ROWS, :] >= mid, 1.0, 0.0)
                    c = c + jnp.sum(ge.reshape(CNT_ROWS // SUBLANES, SUBLANES, qb_rows), axis=0)
                return c

            c = lax.fori_loop(0, n_kt, cbody, jnp.zeros((SUBLANES, qb_rows), F32))
            return jnp.sum(c, axis=0, keepdims=True)

        tau = _kth_largest_key(count_ge, lo0, hi0, topk)
        tau_ref[...] = tau
        over_f = jnp.where(count_ge(tau) > topk, 1.0, 0.0)

        @pl.when(jnp.max(over_f) > 0.0)
        def _break_ties():
            need = topk - count_ge(tau + 1)
            k_row = lax.broadcasted_iota(I32, (PKT, qb_rows), 0)
            top_code = TIE_CODE + (q_lo + qb_rows - 1)

            def tag_body(kt, _):
                k = keys_ref[kt]
                keys_ref[kt] = jnp.where(k == tau, (top_code - kt * PKT) - k_row, k)
                return 0

            lax.fori_loop(0, n_kt, tag_body, 0)
            row = jnp.zeros((1, qb_rows), I32)
            cut = _kth_largest_key(count_ge, row + TIE_CODE, row + top_code, need)
            drop_to = jnp.where(over_f > 0.5, KEY_NEG_INF, tau)

            def untag_body(kt, _):
                k = keys_ref[kt]
                keys_ref[kt] = jnp.where(k >= TIE_CODE, jnp.where(k >= cut, tau, drop_to), k)
                return 0

            lax.fori_loop(0, n_kt, untag_body, 0)

        m_ref[...] = jnp.full(m_ref.shape, NEG_BIG, F32)
        l_ref[...] = jnp.zeros(l_ref.shape, F32)
        acc_ref[...] = jnp.zeros(acc_ref.shape, F32)

    kb = j - 1
    last_kb = (q_lo + qb_rows - 1) // PKB

    n_sub = PKB // PSUB
    hsl = [slice(h * HEAD_DIM, (h + 1) * HEAD_DIM) for h in range(N_HEADS)]

    def logits_stage(u, b):
        for h in range(N_HEADS):
            s_ref[b, h] = _dot_nt(kb_ref[u * PSUB:(u + 1) * PSUB, hsl[h]], qs_ref[:, hsl[h]])

    def softmax_stage(u, b, near):
        s0 = kb * PKB + u * PSUB
        kt = s0 // PKT
        c0 = (u * PSUB) % PKT
        sel = keys_ref[kt, c0:c0 + PSUB, :] >= tau_ref[...]
        if near:
            which = jnp.clip((q_lo - s0) // PSUB, 0, 1)
        for h in range(N_HEADS):
            s = s_ref[b, h]
            if near:
                s = s + band_ref[h, 1 - which]
            s = jnp.where(sel, s, NEG_BIG)
            m_old = m_ref[h]
            m_new = jnp.maximum(m_old, jnp.max(s, axis=0, keepdims=True))
            p = jnp.exp(s - m_new)
            alpha = jnp.exp(m_old - m_new)
            l_ref[h] = alpha * l_ref[h] + jnp.sum(p, axis=0, keepdims=True)
            m_ref[h] = m_new
            a_ref[b, h] = alpha
            p_ref[b, h] = p.astype(BF16)

    def values_stage(u, b):
        for h in range(N_HEADS):
            pv = _dot(vt_ref[hsl[h], u * PSUB:(u + 1) * PSUB], p_ref[b, h])
            acc_ref[h] = acc_ref[h] * a_ref[b, h] + pv

    attend = jnp.logical_and(j >= 1, kb <= last_kb)
    block_far = kb * PKB + PKB <= q_lo - PSUB

    @pl.when(jnp.logical_and(attend, block_far))
    def _attend_far_block():
        logits_stage(0, 0)
        for u in range(n_sub):
            if u + 1 < n_sub:
                logits_stage(u + 1, (u + 1) % 2)
            softmax_stage(u, u % 2, False)
            values_stage(u, u % 2)

    @pl.when(jnp.logical_and(attend, jnp.logical_not(block_far)))
    def _attend_near_block():
        for u in range(n_sub):
            s0 = kb * PKB + u * PSUB
            is_near = s0 + PSUB > q_lo - PSUB
            in_range = s0 < q_lo + qb_rows

            def sub_tile(near, u=u):
                logits_stage(u, 0)
                softmax_stage(u, 0, near)
                values_stage(u, 0)

            @pl.when(jnp.logical_and(in_range, is_near))
            def _():
                sub_tile(True)

            @pl.when(jnp.logical_and(in_range, jnp.logical_not(is_near)))
            def _():
                sub_tile(False)

    @pl.when(j == last_kb + 1)
    def _finish():
        for h in range(N_HEADS):
            hs = slice(h * HEAD_DIM, (h + 1) * HEAD_DIM)
            o_ref[:, hs] = (acc_ref[h] / l_ref[h]).T


def _prompt_attention(qs, qib, wi, kib, kb, vt, rel_bias):
    s, w = qs.shape
    topk = min(TOPK_MAX, s // 4)
    qb_rows = min(PQB, s)
    assert s % qb_rows == 0 and s % PKB == 0 and qb_rows == PSUB
    nqb = s // qb_rows
    nkb = s // PKB
    band = _bias_band(rel_bias, 2 * PSUB, qb_rows, -1, PSUB).reshape(N_HEADS, 2, PSUB, qb_rows)

    def last_kb(i):
        return (i * qb_rows + qb_rows - 1) // PKB

    qrow = lambda n: pl.BlockSpec((qb_rows, n), lambda i, j: (i, 0))
    return pl.pallas_call(
        functools.partial(_prompt_att_kernel, topk=topk, qb_rows=qb_rows),
        grid=(nqb, nkb + 1),
        in_specs=[qrow(w), qrow(w), pl.BlockSpec((N_IDX_HEADS, qb_rows), lambda i, j: (0, i)),
                  pl.BlockSpec((s, IDX_DIM), lambda i, j: (0, 0)),
                  pl.BlockSpec((PKB, w), lambda i, j: (jnp.minimum(jnp.maximum(j - 1, 0), last_kb(i)), 0)),
                  pl.BlockSpec((w, PKB), lambda i, j: (0, jnp.minimum(jnp.maximum(j - 1, 0), last_kb(i)))),
                  pl.BlockSpec(band.shape, lambda i, j: (0, 0, 0, 0))],
        out_specs=qrow(w),
        out_shape=jax.ShapeDtypeStruct((s, w), F32),
        scratch_shapes=[pltpu.VMEM((s // PKT, PKT, qb_rows), I32),
                        pltpu.VMEM((1, qb_rows), I32),
                        pltpu.VMEM((N_HEADS, 1, qb_rows), F32),
                        pltpu.VMEM((N_HEADS, 1, qb_rows), F32),
                        pltpu.VMEM((N_HEADS, HEAD_DIM, qb_rows), F32),
                        pltpu.VMEM((2, N_HEADS, PSUB, qb_rows), F32),
                        pltpu.VMEM((2, N_HEADS, PSUB, qb_rows), BF16),
                        pltpu.VMEM((2, N_HEADS, 1, qb_rows), F32)],
        compiler_params=_cparams(("arbitrary", "arbitrary")),
    )(qs, qib, wi.T, kib, kb, vt, band)


RC = 16
RNB = 16
RBLK = RC * RNB


def _dot3(a, b):
    ah, al = _split2(a)
    bh, bl = _split2(b)
    return _dot(ah, bh) + (_dot(ah, bl) + _dot(al, bh))


def _dot3_nt(a, b):
    ah, al = _split2(a)
    bh, bl = _split2(b)
    return _dot_nt(ah, bh) + (_dot_nt(ah, bl) + _dot_nt(al, bh))


def _dot3_tn(a, b):
    dn = (((0,), (0,)), ((), ()))
    ah, al = _split2(a)
    bh, bl = _split2(b)
    f = lambda x, y: lax.dot_general(x, y, dn, preferred_element_type=F32)
    return f(ah, bh) + (f(ah, bl) + f(al, bh))


def _dot_exact_lhs(m_bf, x):
    hi, mid, lo = _split3(x)
    return _dot(m_bf, hi) + (_dot(m_bf, mid) + _dot(m_bf, lo))


def _dot_exact_rhs(x, m_bf):
    hi, mid, lo = _split3(x)
    return _dot(hi, m_bf) + (_dot(mid, m_bf) + _dot(lo, m_bf))


def _rwkv_kernel(*refs, chain, t_valid):
    if chain:
        p_ref, = refs[:1]
        rest = refs[1:]
    else:
        p_ref, pprev_ref, s0_ref = refs[:3]
        rest = refs[3:]
    (mu_ref, w0_ref, wd2_ref, a0_ref, wa2_ref, wg2_ref, kk_ref, ka_ref, rk_ref, lng_ref, lnb_ref,
     lt_ref, bo_ref, bd_ref, out_ref, sfin_ref,
     carry_ref, s_ref, wa_s, rq_s, uv_s, yv_s, bt_s, kt_s, v_s, gc_s, y_s,
     np_s, ti_s, ak_s, rb_s, rk_s) = rest
    i = pl.program_id(0)
    w = GROUP_WIDTH
    p = p_ref[...]
    row = lax.broadcasted_iota(I32, (RBLK, 1), 0)
    if chain:
        @pl.when(i == 0)
        def _():
            carry_ref[...] = jnp.zeros(carry_ref.shape, F32)
            s_ref[...] = jnp.zeros(s_ref.shape, F32)

        pprev = jnp.where(row == 0, carry_ref[...], pltpu.roll(p, 1, axis=0))
        carry_ref[...] = p[RBLK - 1:RBLK, :]
    else:
        pprev = pprev_ref[...]
    ps = p + (pprev - p) * mu_ref[...]
    r, k, v = ps[:, :w], ps[:, w:2 * w], ps[:, 2 * w:3 * w]
    o = 3 * w
    xw = ps[:, o:o + DECAY_LORA]
    xa = ps[:, o + DECAY_LORA:o + DECAY_LORA + AAA_LORA]
    xg = ps[:, o + DECAY_LORA + AAA_LORA:]
    dec = w0_ref[...] + _dot(jnp.tanh(xw).astype(BF16), wd2_ref[...])
    softplus = jnp.maximum(-dec, 0.0) + jnp.log(1.0 + jnp.exp(-jnp.abs(dec)))
    lw = -jnp.exp(-softplus - 0.5)
    a = jax.nn.sigmoid(a0_ref[...] + _dot(xa.astype(BF16), wa2_ref[...]))
    g = _dot(jax.nn.sigmoid(xg).astype(BF16), wg2_ref[...])
    kk = k * kk_ref[...]
    kk = kk * lax.rsqrt(jnp.maximum(_dot_exact_rhs(kk * kk, bd_ref[...]), 1e-24))
    k2 = k * (1.0 + (a - 1.0) * ka_ref[...])
    alpha = -kk
    beta = kk * a
    if t_valid < RC:
        valid = (row % RC) < t_valid
        zero = lambda x: jnp.where(valid, x, 0.0)
        lw, alpha, beta, k2, r, v = zero(lw), zero(alpha), zero(beta), zero(k2), zero(r), zero(v)
    cl = _dot_exact_lhs(lt_ref[...], lw)
    ct = _dot_exact_lhs(bo_ref[...], lw)
    g_in = jnp.exp(cl)
    g_ex = jnp.exp(cl - lw)
    g_inv = jnp.exp(-cl)
    g_end = jnp.exp(ct - cl)
    g_all = jnp.exp(ct)
    at, rt = alpha * g_ex, r * g_in
    bh, kh = beta * g_inv, k2 * g_inv
    bt, kt = beta * g_end, k2 * g_end

    ri = lax.broadcasted_iota(I32, (RBLK, RBLK), 0)
    ci = lax.broadcasted_iota(I32, (RBLK, RBLK), 1)
    same = (ri // RC) == (ci // RC)
    strict = jnp.logical_and(same, ci < ri)
    incl = jnp.logical_and(same, ci <= ri)
    eye = (ri == ci).astype(F32)

    heads = range(N_HEADS)
    hsl = [slice(h * HEAD_DIM, (h + 1) * HEAD_DIM) for h in heads]
    for h in heads:
        hs = hsl[h]
        gm = _dot3_nt(jnp.concatenate([at[:, hs], rt[:, hs]], axis=0),
                      jnp.concatenate([bh[:, hs], kh[:, hs]], axis=0))
        n1 = jnp.where(strict, gm[:RBLK, :RBLK], 0.0)
        np_s[h] = n1
        ti_s[h] = eye + n1
        ak_s[h] = jnp.where(strict, gm[:RBLK, RBLK:], 0.0)
        rb_s[h] = jnp.where(incl, gm[RBLK:, :RBLK], 0.0)
        rk_s[h] = jnp.where(incl, gm[RBLK:, RBLK:], 0.0)
        bt_s[h] = bt[:, hs]
        kt_s[h] = kt[:, hs]
        v_s[h] = v[:, hs]
        gc_s[h] = g_all[:, hs]
    for _ in range(int(math.log2(RC)) - 1):
        for h in heads:
            npow = np_s[h]
            np_s[h] = _dot3(npow, npow)
        for h in heads:
            tinv = ti_s[h]
            ti_s[h] = tinv + _dot3(np_s[h], tinv)
    for h in heads:
        uv_s[h] = _dot3(ak_s[h], v_s[h])
    for h in heads:
        wcat = _dot3(ti_s[h], jnp.concatenate([at[:, hsl[h]], uv_s[h]], axis=1))
        wa_s[h] = wcat[:, :HEAD_DIM]
        uv_s[h] = wcat[:, HEAD_DIM:]
    for h in heads:
        ry = _dot3(rb_s[h], jnp.concatenate([wa_s[h], uv_s[h]], axis=1))
        rq_s[h] = rt[:, hsl[h]] + ry[:, :HEAD_DIM]
        yv_s[h] = ry[:, HEAD_DIM:] + _dot3(rk_s[h], v_s[h])

    def chunk_body(c, _):
        c0 = pl.multiple_of(c * RC, RC)
        sl = pl.ds(c0, RC)
        s_old = [s0_ref[c, h] if not chain else s_ref[h] for h in heads]
        res = [_dot_nt(jnp.concatenate([wa_s[h, sl, :], rq_s[h, sl, :]], axis=0).astype(BF16),
                       s_old[h].astype(BF16)) for h in heads]
        for h in heads:
            y_s[h, sl, :] = res[h][RC:] + yv_s[h, sl, :]
        upd = [lax.dot_general(jnp.concatenate([res[h][:RC] + uv_s[h, sl, :], v_s[h, sl, :]], axis=0).astype(BF16),
                               jnp.concatenate([bt_s[h, sl, :], kt_s[h, sl, :]], axis=0).astype(BF16),
                               (((0,), (0,)), ((), ())), preferred_element_type=F32) for h in heads]
        for h in heads:
            s_new = s_old[h] * gc_s[h, pl.ds(c0, 1), :] + upd[h]
            if chain:
                s_ref[h] = s_new
            else:
                sfin_ref[c, h] = s_new
        return 0

    lax.fori_loop(0, RNB, chunk_body, 0)
    if chain:
        sfin_ref[...] = s_ref[...]

    for h in range(N_HEADS):
        hs = slice(h * HEAD_DIM, (h + 1) * HEAD_DIM)
        y = y_s[h]
        mean = jnp.mean(y, axis=-1, keepdims=True)
        var = jnp.mean(jnp.square(y - mean), axis=-1, keepdims=True)
        yn = (y - mean) * lax.rsqrt(var + GN_EPS) * lng_ref[:, hs] + lnb_ref[:, hs]
        bonus = jnp.sum(r[:, hs] * k2[:, hs] * rk_ref[:, hs], axis=-1, keepdims=True) * v[:, hs]
        out_ref[:, hs] = (yn + bonus) * g[:, hs]


def _rwkv_consts():
    idx = np.arange(RBLK)
    same = (idx[:, None] // RC) == (idx[None, :] // RC)
    lt = (same & (idx[None, :] <= idx[:, None])).astype(np.float32)
    bo = same.astype(np.float32)
    lane = np.arange(GROUP_WIDTH)
    bd = ((lane[:, None] // HEAD_DIM) == (lane[None, :] // HEAD_DIM)).astype(np.float32)
    return jnp.asarray(lt, BF16), jnp.asarray(bo, BF16), jnp.asarray(bd, BF16)


def _rwkv(p, lw, chain, pprev=None, s0=None, t_valid=RC):
    rows = p.shape[0]
    nblk = rows // RBLK
    w = GROUP_WIDTH
    vec = lambda a: a.reshape(1, -1)
    consts = [vec(lw['mu_shift']), vec(lw['w0']), lw['w_decay2'].astype(BF16), vec(lw['a0']),
              lw['w_a2'].astype(BF16), lw['w_g2'].astype(BF16), vec(lw['k_k']), vec(lw['k_a']), vec(lw['r_k']),
              vec(lw['lnx_g']), vec(lw['lnx_b'])] + list(_rwkv_consts())
    full = lambda a: pl.BlockSpec(a.shape, lambda i: (0,) * a.ndim)
    blk = pl.BlockSpec((RBLK, RWKV_COLS), lambda i: (i, 0))
    hshape = (N_HEADS, HEAD_DIM, HEAD_DIM)
    if chain:
        ins, in_specs = [p], [blk]
        sfin_spec = pl.BlockSpec(hshape, lambda i: (0, 0, 0))
        sfin_shape = jax.ShapeDtypeStruct(hshape, F32)
    else:
        sspec = pl.BlockSpec((RNB,) + hshape, lambda i: (i, 0, 0, 0))
        ins, in_specs = [p, pprev, s0], [blk, blk, sspec]
        sfin_spec = sspec
        sfin_shape = jax.ShapeDtypeStruct((nblk * RNB,) + hshape, F32)
    hm = lambda: pltpu.VMEM((N_HEADS, RBLK, HEAD_DIM), F32)
    return pl.pallas_call(
        functools.partial(_rwkv_kernel, chain=chain, t_valid=t_valid),
        grid=(nblk,),
        in_specs=in_specs + [full(c) for c in consts],
        out_specs=[pl.BlockSpec((RBLK, w), lambda i: (i, 0)), sfin_spec],
        out_shape=[jax.ShapeDtypeStruct((rows, w), F32), sfin_shape],
        scratch_shapes=[pltpu.VMEM((1, RWKV_COLS), F32), pltpu.VMEM(hshape, F32)] + [hm() for _ in range(9)]
        + [pltpu.VMEM((N_HEADS, RBLK, RBLK), F32) for _ in range(5)],
        compiler_params=_cparams(("arbitrary",)),
    )(*ins, *consts)


RT_TM = 256


def _route_kernel(x_ref, ro_ref, ao_ref, wor_ref, woa_ref, g1_ref, sh_ref, sc_ref, n2_ref, wr_ref, br_ref,
                  ltri_ref, x1_ref, h2_ref, idx_ref, gate_ref, rank_ref, cnt_ref, run_ref):
    @pl.when(pl.program_id(0) == 0)
    def _():
        run_ref[...] = jnp.zeros(run_ref.shape, F32)

    tm = x_ref.shape[0]
    mix = _dot(ro_ref[...].astype(BF16), wor_ref[...]) + _dot(ao_ref[...].astype(BF16), woa_ref[...])
    x1 = x_ref[...] + g1_ref[...] * mix
    x1_ref[...] = x1
    y = x1 * lax.rsqrt(jnp.mean(x1 * x1, axis=-1, keepdims=True) + RMS_EPS) * n2_ref[...]
    h2 = (y * (1.0 + sc_ref[...]) + sh_ref[...]).astype(BF16)
    h2_ref[...] = h2
    logits = _dot(h2, wr_ref[...]) + br_ref[...]
    lane = lax.broadcasted_iota(I32, (tm, LANES), 1)
    lane_f = lane.astype(F32)
    lg = logits
    vals, idxs = [], []
    for _ in range(TOP_K):
        m = jnp.max(lg, axis=1, keepdims=True)
        idx = jnp.min(jnp.where(lg == m, lane_f, float(LANES)), axis=1, keepdims=True)
        vals.append(m)
        idxs.append(idx)
        lg = jnp.where(lane_f == idx, -3e38, lg)
    es = [jnp.exp(v - vals[0]) for v in vals]
    den = es[0] + es[1] + es[2] + es[3]
    ohs = [(lane_f == idx).astype(F32) for idx in idxs]
    oh_all = ohs[0] + ohs[1] + ohs[2] + ohs[3]
    base = run_ref[...] + _dot(ltri_ref[...], oh_all.astype(BF16))
    idx_out = jnp.zeros((tm, LANES), F32)
    gate_out = jnp.zeros((tm, LANES), F32)
    rank_out = jnp.zeros((tm, LANES), F32)
    for kk in range(TOP_K):
        rank = jnp.sum(ohs[kk] * base, axis=1, keepdims=True)
        idx_out = jnp.where(lane == kk, idxs[kk], idx_out)
        gate_out = jnp.where(lane == kk, es[kk] / den, gate_out)
        rank_out = jnp.where(lane == kk, rank, rank_out)
    idx_ref[...] = idx_out[:, :TOP_K].astype(I32)
    gate_ref[...] = gate_out[:, :TOP_K]
    rank_ref[...] = rank_out[:, :TOP_K].astype(I32)
    run_ref[...] = run_ref[...] + jnp.sum(oh_all, axis=0, keepdims=True)
    cnt_ref[...] = run_ref[...]


def _out_proj_route(x, ro, ao, w_out, mod, norm2_g, w_router, b_router):
    t, d = x.shape
    tm = min(RT_TM, t)
    w = GROUP_WIDTH
    wor = w_out[:w].astype(BF16)
    woa = w_out[w:].astype(BF16)
    wr = jnp.pad(w_router, ((0, 0), (0, LANES - N_EXPERTS))).astype(BF16)
    br = jnp.pad(b_router.reshape(1, -1), ((0, 0), (0, LANES - N_EXPERTS)), constant_values=NEG_BIG)
    ltri = jnp.asarray(np.tril(np.ones((tm, tm), np.float32), -1), BF16)
    row = lambda n: pl.BlockSpec((tm, n), lambda i: (i, 0))
    full = lambda a: pl.BlockSpec(a.shape, lambda i: (0, 0))
    return pl.pallas_call(
        _route_kernel,
        grid=(t // tm,),
        in_specs=[row(d), row(w), row(w), full(wor), full(woa),
                  _mod_spec(mod, 2, tm, d), _mod_spec(mod, 3, tm, d), _mod_spec(mod, 4, tm, d),
                  pl.BlockSpec((1, d), lambda i: (0, 0)), full(wr), full(br), full(ltri)],
        out_specs=[row(d), row(d), row(TOP_K), row(TOP_K), row(TOP_K), pl.BlockSpec((1, LANES), lambda i: (0, 0))],
        out_shape=[jax.ShapeDtypeStruct((t, d), F32), jax.ShapeDtypeStruct((t, d), BF16),
                   jax.ShapeDtypeStruct((t, TOP_K), I32), jax.ShapeDtypeStruct((t, TOP_K), F32),
                   jax.ShapeDtypeStruct((t, TOP_K), I32), jax.ShapeDtypeStruct((1, LANES), F32)],
        scratch_shapes=[pltpu.VMEM((1, LANES), F32)],
        compiler_params=_cparams(("arbitrary",)),
    )(x, ro, ao, wor, woa, mod, mod, mod, norm2_g.reshape(1, d), wr, br, ltri)


EX_TM = 256
BF16_ROW = (SUBLANES, LANES)


def _dispatch_kernel(slot_ref, h_ref, init_ref, out_ref, sem):
    del init_ref
    tm = h_ref.shape[0]

    def body(r, _):
        for kk in range(TOP_K):
            pltpu.make_async_copy(h_ref.at[r], out_ref.at[slot_ref[r * TOP_K + kk]], sem).start()
        return 0

    lax.fori_loop(0, tm, body, 0)
    for _ in range(TOP_K):
        pltpu.make_async_copy(h_ref, out_ref.at[pl.ds(0, tm)], sem).wait()


def _dispatch(h2, slot, rows_sorted):
    t, d = h2.shape
    tm = min(RT_TM, t)
    assert d == SUBLANES * LANES
    h3 = h2.reshape((t,) + BF16_ROW)
    return pl.pallas_call(
        _dispatch_kernel,
        grid=(t // tm,),
        in_specs=[pl.BlockSpec((tm * TOP_K,), lambda i: (i,), memory_space=pltpu.SMEM),
                  pl.BlockSpec((tm,) + BF16_ROW, lambda i: (i, 0, 0)),
                  pl.BlockSpec(memory_space=pl.ANY)],
        out_specs=pl.BlockSpec(memory_space=pl.ANY),
        out_shape=jax.ShapeDtypeStruct(rows_sorted.shape, rows_sorted.dtype),
        scratch_shapes=[pltpu.SemaphoreType.DMA(())],
        input_output_aliases={2: 0},
        compiler_params=_cparams(("arbitrary",)),
    )(slot.reshape(-1), h3, rows_sorted)


def _expert_kernel(be_ref, nv_ref, x_ref, wgu_ref, bgu_ref, wd_ref, bd_ref, o_ref, wgu_bf, wd_bf):
    i = pl.program_id(0)
    changed = jnp.logical_or(i == 0, be_ref[i] != be_ref[jnp.maximum(i - 1, 0)])

    @pl.when(changed)
    def _():
        wgu_bf[...] = wgu_ref[0].astype(BF16)
        wd_bf[...] = wd_ref[0].astype(BF16)

    @pl.when(i < nv_ref[0])
    def _():
        f = wd_bf.shape[0]
        gu = _dot(x_ref[...], wgu_bf[...]) + bgu_ref[0]
        glu = jnp.minimum(gu[:, :f], SWIGLU_LIMIT)
        lin = jnp.clip(gu[:, f:], -SWIGLU_LIMIT, SWIGLU_LIMIT)
        act = glu * jax.nn.sigmoid(SWIGLU_ALPHA * glu) * (lin + 1.0)
        o_ref[...] = _dot(act.astype(BF16), wd_bf[...]) + bd_ref[0]

    @pl.when(i >= nv_ref[0])
    def _():
        o_ref[...] = jnp.zeros(o_ref.shape, F32)


def _experts(rows_sorted, blk_e, n_valid, w_gu, b_gu, w_down, b_down):
    ns, d = rows_sorted.shape
    e, _, f2 = w_gu.shape
    f = f2 // 2
    grid_spec = pltpu.PrefetchScalarGridSpec(
        num_scalar_prefetch=2,
        grid=(ns // EX_TM,),
        in_specs=[pl.BlockSpec((EX_TM, d), lambda i, be, nv: (i, 0)),
                  pl.BlockSpec((1, d, f2), lambda i, be, nv: (be[i], 0, 0)),
                  pl.BlockSpec((1, 1, f2), lambda i, be, nv: (be[i], 0, 0)),
                  pl.BlockSpec((1, f, d), lambda i, be, nv: (be[i], 0, 0)),
                  pl.BlockSpec((1, 1, d), lambda i, be, nv: (be[i], 0, 0))],
        out_specs=pl.BlockSpec((EX_TM, d), lambda i, be, nv: (i, 0)),
        scratch_shapes=[pltpu.VMEM((d, f2), BF16), pltpu.VMEM((f, d), BF16)])
    return pl.pallas_call(
        _expert_kernel,
        grid_spec=grid_spec,
        out_shape=jax.ShapeDtypeStruct((ns, d), F32),
        compiler_params=_cparams(("arbitrary",)),
    )(blk_e, n_valid, rows_sorted, w_gu, b_gu.reshape(e, 1, f2), w_down, b_down.reshape(e, 1, d))


def _combine_kernel(slot_ref, rows_ref, gate_ref, x1_ref, g2_ref, nf_ref, y_ref, buf, sem):
    tm = x1_ref.shape[0]

    def body(r, _):
        for kk in range(TOP_K):
            pltpu.make_async_copy(rows_ref.at[pl.ds(slot_ref[r * TOP_K + kk], 1)],
                                  buf.at[kk, pl.ds(r, 1)], sem).start()
        return 0

    lax.fori_loop(0, tm, body, 0)
    for kk in range(TOP_K):
        pltpu.make_async_copy(rows_ref.at[pl.ds(0, tm)], buf.at[kk], sem).wait()
    gates = gate_ref[...]
    moe = gates[:, 0:1] * buf[0]
    for kk in range(1, TOP_K):
        moe = moe + gates[:, kk:kk + 1] * buf[kk]
    x2 = x1_ref[...] + g2_ref[...] * moe
    y_ref[...] = x2 * lax.rsqrt(jnp.mean(x2 * x2, axis=-1, keepdims=True) + RMS_EPS) * nf_ref[...]


def _combine(out_rows, slot, gates, x1, mod, normf_g):
    t, d = x1.shape
    tm = min(RT_TM, t)
    return pl.pallas_call(
        _combine_kernel,
        grid=(t // tm,),
        in_specs=[pl.BlockSpec((tm * TOP_K,), lambda i: (i,), memory_space=pltpu.SMEM),
                  pl.BlockSpec(memory_space=pl.ANY),
                  pl.BlockSpec((tm, TOP_K), lambda i: (i, 0)),
                  pl.BlockSpec((tm, d), lambda i: (i, 0)), _mod_spec(mod, 5, tm, d),
                  pl.BlockSpec((1, d), lambda i: (0, 0))],
        out_specs=pl.BlockSpec((tm, d), lambda i: (i, 0)),
        out_shape=jax.ShapeDtypeStruct((t, d), F32),
        scratch_shapes=[pltpu.VMEM((TOP_K, tm, d), F32), pltpu.SemaphoreType.DMA(())],
        compiler_params=_cparams(("arbitrary",)),
    )(slot.reshape(-1), out_rows, gates, x1, mod, normf_g.reshape(1, d))


def _moe_plan(idx_p, rank_p, cnt_p, idx_s, rank_s, cnt_s):
    n_assign = idx_p.size + idx_s.size
    n_tiles = -(-n_assign // EX_TM) + N_EXPERTS
    cp = cnt_p[0, :N_EXPERTS].astype(I32)
    cs = cnt_s[0, :N_EXPERTS].astype(I32)
    padded = (cp + cs + EX_TM - 1) // EX_TM * EX_TM
    pad_end = jnp.cumsum(padded)
    pad_start = pad_end - padded
    slot_p = pad_start[idx_p] + rank_p
    slot_s = pad_start[idx_s] + cp[idx_s] + rank_s
    tile_row = jnp.arange(n_tiles, dtype=I32)[:, None] * EX_TM
    blk_e = jnp.minimum(jnp.sum((pad_end[None, :] <= tile_row).astype(I32), axis=1), N_EXPERTS - 1)
    n_valid = (pad_end[-1:] // EX_TM).astype(I32)
    return slot_p, slot_s, blk_e, n_valid, n_tiles * EX_TM


PG = 8
QROWS = N_HEADS * SUBLANES


def _page_specs(block, n_pages):
    def spec(u):
        return pl.BlockSpec(block, lambda b, j, pt: (pt[b * n_pages + j * PG + u],) + (0,) * (len(block) - 1))
    return [spec(u) for u in range(PG)]


def _head_sum(x):
    out = x[:SUBLANES]
    for h in range(1, N_HEADS):
        out = out + x[h * SUBLANES:(h + 1) * SUBLANES]
    return out


def _sample_index_kernel(pt_ref, qi_ref, wrep_ref, kin_ref, *rest, n_pages, t_new, topk):
    del pt_ref
    pages = rest[:PG]
    keys_ref, tau_ref = rest[PG:]
    j = pl.program_id(1)
    qi = qi_ref[0]
    wrep = wrep_ref[0]

    def scores(ki_t_bf):
        s = _dot(qi, ki_t_bf)
        return _head_sum(jnp.maximum(s, 0.0) * wrep)

    for u in range(PG):
        keys_ref[0, j * PG + u] = _float_key(scores(pages[u][0].astype(BF16)))

    @pl.when(j == pl.num_programs(1) - 1)
    def _():
        qrow = lax.broadcasted_iota(I32, (SUBLANES, PAGE_SIZE), 0)
        col = lax.broadcasted_iota(I32, (SUBLANES, PAGE_SIZE), 1)
        ok = jnp.logical_and(col <= qrow, col < t_new)
        keys_ref[0, n_pages] = jnp.where(ok, _float_key(scores(kin_ref[0])), KEY_NEG_INF)

        def count_ge(mid):
            c = jnp.sum(jnp.where(keys_ref[0] >= mid[None], 1.0, 0.0), axis=0)
            return jnp.broadcast_to(jnp.sum(c, axis=1, keepdims=True), (SUBLANES, LANES))

        tau = _kth_largest_key(count_ge, jnp.full((SUBLANES, LANES), KEY_NEG_INF + 1, I32),
                               jnp.full((SUBLANES, LANES), 0x7F800000, I32), topk)
        tau_ref[0] = tau
        over_f = jnp.where(count_ge(tau) > topk, 1.0, 0.0)

        @pl.when(jnp.max(over_f) > 0.0)
        def _break_ties():
            need = topk - count_ge(tau + 1)
            shape = keys_ref.shape[1:]
            pos = lax.broadcasted_iota(I32, shape, 0) * PAGE_SIZE + lax.broadcasted_iota(I32, shape, 2)
            tied = keys_ref[0] == tau[None]

            def pbody(_, carry):
                lo, hi = carry
                mid = (lo + hi) >> 1
                c = jnp.sum(jnp.where(tied, jnp.where(pos <= mid[None], 1.0, 0.0), 0.0), axis=0)
                ok = jnp.broadcast_to(jnp.sum(c, axis=1, keepdims=True), (SUBLANES, LANES)) >= need
                return jnp.where(ok, lo, mid + 1), jnp.where(ok, mid, hi)

            n_pos = (n_pages + 1) * PAGE_SIZE
            last_pos, _ = lax.fori_loop(0, (n_pos - 1).bit_length(), pbody,
                                        (jnp.zeros((SUBLANES, LANES), I32), jnp.full((SUBLANES, LANES), n_pos - 1, I32)))
            gone = jnp.where(tied, jnp.where(pos > last_pos[None], over_f[None], 0.0), 0.0)
            keys_ref[0] = jnp.where(gone > 0.5, KEY_NEG_INF, keys_ref[0])


def _sample_attend_kernel(pt_ref, q_ref, keys_ref, tau_ref, band_ref, kn_ref, vn_ref, *rest, n_pages):
    del pt_ref
    kpages = rest[:PG]
    vpages = rest[PG:2 * PG]
    o_ref, m_ref, l_ref, acc_ref = rest[2 * PG:]
    j = pl.program_id(1)
    last = j == pl.num_programs(1) - 1

    @pl.when(j == 0)
    def _():
        m_ref[...] = jnp.full(m_ref.shape, NEG_BIG, F32)
        l_ref[...] = jnp.zeros(l_ref.shape, F32)
        acc_ref[...] = jnp.zeros(acc_ref.shape, F32)

    tau = tau_ref[0]
    heads = range(N_HEADS)

    def attend(k_of, v_of, key_tiles, bias_of):
        pages = range(len(key_tiles))
        sel = [kt >= tau for kt in key_tiles]
        s = [[_dot(q_ref[0, h], k_of(u, h)) for u in pages] for h in heads]
        ps, alphas = [], []
        for h in heads:
            sh = []
            for u in pages:
                b_uh = bias_of(u, h)
                sh.append(jnp.where(sel[u], s[h][u] if b_uh is None else s[h][u] + b_uh, NEG_BIG))
            tile_max = functools.reduce(jnp.maximum, sh)
            m_old = m_ref[h]
            m_new = jnp.maximum(m_old, jnp.broadcast_to(jnp.max(tile_max, axis=1, keepdims=True), m_old.shape))
            p = [jnp.exp(x - m_new) for x in sh]
            alpha = jnp.exp(m_old - m_new)
            p_sum = functools.reduce(jnp.add, p)
            l_ref[h] = alpha * l_ref[h] + jnp.broadcast_to(jnp.sum(p_sum, axis=1, keepdims=True), m_old.shape)
            m_ref[h] = m_new
            ps.append([x.astype(BF16) for x in p])
            alphas.append(alpha[:, :HEAD_DIM])
        pv = [[_dot_nt(ps[h][u], v_of(u, h)) for u in pages] for h in heads]
        for h in heads:
            acc_ref[h] = acc_ref[h] * alphas[h] + functools.reduce(jnp.add, pv[h])

    attend(lambda u, h: kpages[u][0, h].astype(BF16), lambda u, h: vpages[u][0, h].astype(BF16),
           [keys_ref[0, j * PG + u] for u in range(PG)],
           lambda u, h: jnp.where(last, band_ref[h, :, :PAGE_SIZE], 0.0) if u == PG - 1 else None)

    @pl.when(last)
    def _():
        hsl = lambda h: slice(h * HEAD_DIM, (h + 1) * HEAD_DIM)
        attend(lambda u, h: kn_ref[0, hsl(h), :], lambda u, h: vn_ref[0, hsl(h), :], [keys_ref[0, n_pages]],
               lambda u, h: band_ref[h, :, PAGE_SIZE:])
        for h in heads:
            o_ref[0, :, hsl(h)] = acc_ref[h] / l_ref[h][:, :HEAD_DIM]


def _sample_attention(qs, qib, wi, k_new_bf, v_new_bf, ki_new_bf, cache_k, cache_v, cache_kidx, page_table,
                      rel_bias, b, t_new):
    w = GROUP_WIDTH
    n_pages = page_table.shape[1]
    assert n_pages % PG == 0 and t_new <= SUBLANES and cache_k.shape[1] == PAGE_SIZE
    past = n_pages * PAGE_SIZE
    topk = min(TOPK_MAX, (past + t_new) // 4)
    pt = page_table.reshape(-1)
    padq = lambda a: jnp.pad(a, ((0, 0), (0, SUBLANES - t_new)) + ((0, 0),) * (a.ndim - 2))
    padk = lambda a: jnp.pad(a.reshape(b, t_new, -1), ((0, 0), (0, PAGE_SIZE - t_new), (0, 0))).transpose(0, 2, 1)
    qi_r = padq(qib.reshape(b, t_new, N_IDX_HEADS, IDX_DIM)).transpose(0, 2, 1, 3).reshape(b, QROWS, IDX_DIM)
    w_r = padq(wi.reshape(b, t_new, N_IDX_HEADS)).transpose(0, 2, 1).reshape(b, QROWS, 1)
    w_r = jnp.broadcast_to(w_r, (b, QROWS, LANES))
    q4 = padq(qs.reshape(b, t_new, N_HEADS, HEAD_DIM)).transpose(0, 2, 1, 3)
    kin, kn, vn = padk(ki_new_bf), padk(k_new_bf), padk(v_new_bf)
    band = _bias_band(rel_bias, SUBLANES, 2 * PAGE_SIZE, 1, PAGE_SIZE)
    ck_t = cache_k.transpose(0, 2, 3, 1)
    cv_t = cache_v.transpose(0, 2, 3, 1)
    cki_t = cache_kidx.transpose(0, 2, 1)
    page_block = (1, N_HEADS, HEAD_DIM, PAGE_SIZE)

    per_b = lambda shape: pl.BlockSpec((1,) + shape, lambda bb, j, p_: (bb,) + (0,) * len(shape))
    steps = n_pages // PG
    keys, tau = pl.pallas_call(
        functools.partial(_sample_index_kernel, n_pages=n_pages, t_new=t_new, topk=topk),
        grid_spec=pltpu.PrefetchScalarGridSpec(
            num_scalar_prefetch=1, grid=(b, steps),
            in_specs=[per_b((QROWS, IDX_DIM)), per_b((QROWS, LANES)), per_b((IDX_DIM, PAGE_SIZE))]
            + _page_specs((1, IDX_DIM, PAGE_SIZE), n_pages),
            out_specs=[per_b((n_pages + 1, SUBLANES, PAGE_SIZE)), per_b((SUBLANES, LANES))]),
        out_shape=[jax.ShapeDtypeStruct((b, n_pages + 1, SUBLANES, PAGE_SIZE), I32),
                   jax.ShapeDtypeStruct((b, SUBLANES, LANES), I32)],
        compiler_params=_cparams(("arbitrary", "arbitrary")),
    )(pt, qi_r, w_r, kin, *([cki_t] * PG))
    out = pl.pallas_call(
        functools.partial(_sample_attend_kernel, n_pages=n_pages),
        grid_spec=pltpu.PrefetchScalarGridSpec(
            num_scalar_prefetch=1, grid=(b, steps),
            in_specs=[per_b((N_HEADS, SUBLANES, HEAD_DIM)), per_b((n_pages + 1, SUBLANES, PAGE_SIZE)),
                      per_b((SUBLANES, LANES)), pl.BlockSpec(band.shape, lambda bb, j, p_: (0, 0, 0)),
                      per_b((w, PAGE_SIZE)), per_b((w, PAGE_SIZE))]
            + _page_specs(page_block, n_pages) + _page_specs(page_block, n_pages),
            out_specs=per_b((SUBLANES, w)),
            scratch_shapes=[pltpu.VMEM((N_HEADS, SUBLANES, LANES), F32), pltpu.VMEM((N_HEADS, SUBLANES, LANES), F32),
                            pltpu.VMEM((N_HEADS, SUBLANES, HEAD_DIM), F32)]),
        out_shape=jax.ShapeDtypeStruct((b, SUBLANES, w), F32),
        compiler_params=_cparams(("arbitrary", "arbitrary")),
    )(pt, q4, keys, tau, band, kn, vn, *([ck_t] * PG), *([cv_t] * PG))
    return out[:, :t_new].reshape(b * t_new, w)


def kernel(x_prompt, x_sample, c_prompt, c_sample, cache_k, cache_v, cache_kidx, page_table, state_wkv,
           state_shift, w_ada, b_ada, norm1_g, w_in, mu_shift, w0, w_decay2, a0, w_a2, w_g2, k_k, k_a, r_k,
           lnx_g, lnx_b, rel_bias, w_out, norm2_g, w_router, b_router, w_gu, b_gu, w_down, b_down, normf_g):
    depth = w_in.shape[0]
    assert depth == 1, "the merged prompt+sample expert pass is written for a single layer"
    bp, sp, d = x_prompt.shape
    bs, ts, _ = x_sample.shape
    assert bp == 1 and sp % RBLK == 0 and bs % RNB == 0 and ts <= RC
    l = 0
    lw = {'mu_shift': mu_shift[l], 'w0': w0[l], 'w_decay2': w_decay2[l], 'a0': a0[l], 'w_a2': w_a2[l],
          'w_g2': w_g2[l], 'k_k': k_k[l], 'k_a': k_a[l], 'r_k': r_k[l], 'lnx_g': lnx_g[l], 'lnx_b': lnx_b[l]}
    ns = bs * ts

    c_all = jnp.concatenate([c_prompt, c_sample], axis=0)
    n_c = c_all.shape[0]
    c_all = jnp.pad(c_all, ((0, -n_c % SUBLANES), (0, 0)))
    mod = _adaln(c_all, w_ada[l], b_ada[l])
    mod_p = mod[0:1]
    mod_s = jnp.repeat(mod[bp:bp + bs], ts, axis=0)

    xp = x_prompt.reshape(sp, d)
    xs = x_sample.reshape(ns, d)
    pr_p, k_p, v_p, ki_p, wi_p, qs_p, kb_p, _, qib_p, kib_p, vt_p = _in_proj(xp, norm1_g[l], mod_p, w_in[l])
    pr_s, k_s, v_s, ki_s, wi_s, qs_s, kb_s, vb_s, qib_s, kib_s, _ = _in_proj(xs, norm1_g[l], mod_s, w_in[l])

    ro_p, wkv_p = _rwkv(pr_p, lw, chain=True)
    att_p = _prompt_attention(qs_p, qib_p, wi_p, kib_p, kb_p, vt_p, rel_bias)

    pr_s3 = pr_s.reshape(bs, ts, RWKV_COLS)
    prev_s3 = jnp.concatenate([state_shift[l][:, None, :], pr_s3[:, :-1]], axis=1)
    padc = lambda a: jnp.pad(a, ((0, 0), (0, RC - ts), (0, 0))).reshape(bs * RC, RWKV_COLS)
    ro_s, wkv_s = _rwkv(padc(pr_s3), lw, chain=False, pprev=padc(prev_s3), s0=state_wkv[l], t_valid=ts)
    ro_s = ro_s.reshape(bs, RC, GROUP_WIDTH)[:, :ts].reshape(ns, GROUP_WIDTH)
    att_s = _sample_attention(qs_s, qib_s, wi_s, kb_s, vb_s, kib_s, cache_k[l], cache_v[l], cache_kidx[l],
                              page_table, rel_bias, bs, ts)

    x1_p, h2_p, idx_p, gate_p, rank_p, cnt_p = _out_proj_route(xp, ro_p, att_p, w_out[l], mod_p, norm2_g[l],
                                                               w_router[l], b_router[l])
    x1_s, h2_s, idx_s, gate_s, rank_s, cnt_s = _out_proj_route(xs, ro_s, att_s, w_out[l], mod_s, norm2_g[l],
                                                               w_router[l], b_router[l])
    slot_p, slot_s, blk_e, n_valid, n_slots = _moe_plan(idx_p, rank_p, cnt_p, idx_s, rank_s, cnt_s)
    rows = jnp.zeros((n_slots,) + BF16_ROW, BF16)
    rows = _dispatch(h2_p, slot_p, rows)
    rows = _dispatch(h2_s, slot_s, rows)
    out_rows = _experts(rows.reshape(n_slots, d), blk_e, n_valid, w_gu[l], b_gu[l], w_down[l], b_down[l])
    y_p = _combine(out_rows, slot_p, gate_p, x1_p, mod_p, normf_g)
    y_s = _combine(out_rows, slot_s, gate_s, x1_s, mod_s, normf_g)

    hd = (N_HEADS, HEAD_DIM)
    return (y_p.reshape(bp, sp, d), y_s.reshape(bs, ts, d),
            k_p.reshape((1, bp, sp) + hd), v_p.reshape((1, bp, sp) + hd), ki_p.reshape(1, bp, sp, IDX_DIM),
            wkv_p.reshape((1, bp) + (N_HEADS, HEAD_DIM, HEAD_DIM)), pr_p[sp - 1:sp].reshape(1, bp, RWKV_COLS),
            k_s.reshape((1, bs, ts) + hd), v_s.reshape((1, bs, ts) + hd), ki_s.reshape(1, bs, ts, IDX_DIM),
            wkv_s.reshape((1, bs) + (N_HEADS, HEAD_DIM, HEAD_DIM)), pr_s3[:, ts - 1].reshape(1, bs, RWKV_COLS))
```

```python
import functools
import math

import jax
import jax.numpy as jnp
import numpy as np
from jax import lax
from jax.experimental import pallas as pl
from jax.experimental.pallas import tpu as pltpu

F32 = jnp.float32
BF16 = jnp.bfloat16
I32 = jnp.int32

HEAD_DIM = 64
N_HEADS = 8
GROUP_WIDTH = N_HEADS * HEAD_DIM
DECAY_LORA, AAA_LORA, GATE_LORA = 64, 64, 128
RWKV_COLS = 3 * GROUP_WIDTH + DECAY_LORA + AAA_LORA + GATE_LORA
IDX_DIM = 64
N_IDX_HEADS = 8
TOPK_MAX = 256
N_BUCKETS = 32
MAX_DISTANCE = 128
N_EXPERTS = 32
TOP_K = 4
SWIGLU_LIMIT = 7.0
SWIGLU_ALPHA = 1.702
RMS_EPS = 1e-6
GN_EPS = HEAD_DIM * 1e-5
PAGE_SIZE = 128

LANES = 128
SUBLANES = 8
VMEM_LIMIT = 56 * 1024 * 1024

NEG_BIG = -1e30
LOG2E = math.log2(math.e)
KEY_NEG_INF = -2139095041
TIE_CODE = 0x7F800001


def _cparams(sem):
    return pltpu.CompilerParams(dimension_semantics=sem, vmem_limit_bytes=VMEM_LIMIT)


def _dot(a, b):
    return jnp.dot(a, b, preferred_element_type=F32)


def _dot_nt(a, b):
    return lax.dot_general(a, b, (((1,), (1,)), ((), ())), preferred_element_type=F32)


def _split2(a):
    hi = a.astype(BF16)
    lo = (a - hi.astype(F32)).astype(BF16)
    return hi, lo


def _split3(a):
    hi = a.astype(BF16)
    r1 = a - hi.astype(F32)
    mid = r1.astype(BF16)
    lo = (r1 - mid.astype(F32)).astype(BF16)
    return hi, mid, lo


def _rep(x, n, axis):
    return jnp.concatenate([x] * n, axis=axis)


def _float_key(x):
    b = pltpu.bitcast(x, I32)
    return b ^ ((b >> 31) & 0x7FFFFFFF)


def _ada_kernel(c_ref, w_ref, b_ref, o_ref):
    c = c_ref[...]
    s = c * jax.nn.sigmoid(c)
    o_ref[...] = _dot(s.astype(BF16), w_ref[...].astype(BF16)) + b_ref[...]


def _adaln(c, w_ada, b_ada):
    r, d = c.shape
    n = w_ada.shape[1]
    tn = 1536
    return pl.pallas_call(
        _ada_kernel,
        grid=(n // tn,),
        in_specs=[pl.BlockSpec((r, d), lambda j: (0, 0)),
                  pl.BlockSpec((d, tn), lambda j: (0, j)),
                  pl.BlockSpec((1, tn), lambda j: (0, j))],
        out_specs=pl.BlockSpec((r, tn), lambda j: (0, j)),
        out_shape=jax.ShapeDtypeStruct((r, n), F32),
        compiler_params=_cparams(("arbitrary",)),
    )(c, w_ada, b_ada.reshape(1, n))


def _inproj_kernel(x_ref, g_ref, sh_ref, sc_ref, wr_ref, wa_ref, wk_ref,
                   pr_ref, k_ref, v_ref, ki_ref, wi_ref, qs_ref, kb_ref, vb_ref, qib_ref, kib_ref, vt_ref, *, q_scale):
    x = x_ref[...]
    y = x * lax.rsqrt(jnp.mean(x * x, axis=-1, keepdims=True) + RMS_EPS) * g_ref[...]
    h = (y * (1.0 + sc_ref[...]) + sh_ref[...]).astype(BF16)
    pr_ref[...] = _dot(h, wr_ref[...])
    a = _dot(h, wa_ref[...])
    w = GROUP_WIDTH
    q, k, v, qi = a[:, :w], a[:, w:2 * w], a[:, 2 * w:3 * w], a[:, 3 * w:4 * w]
    k_ref[...] = k
    v_ref[...] = v
    qs_ref[...] = (q * q_scale).astype(BF16)
    kb_ref[...] = k.astype(BF16)
    vb_ref[...] = v.astype(BF16)
    vt_ref[...] = v.T.astype(BF16)
    qib_ref[...] = (qi * IDX_DIM ** -0.5).astype(BF16)
    kw = _dot(h, wk_ref[...])
    ki = kw[:, :IDX_DIM]
    ki_ref[...] = ki
    kib_ref[...] = ki.astype(BF16)
    wi_ref[...] = kw[:, IDX_DIM:IDX_DIM + N_IDX_HEADS] * N_IDX_HEADS ** -0.5


def _mod_spec(mod, col, tm, d):
    if mod.shape[0] == 1:
        return pl.BlockSpec((1, d), lambda i: (0, col))
    return pl.BlockSpec((tm, d), lambda i: (i, col))


def _in_proj(x, norm_g, mod, w_in, q_scale):
    t, d = x.shape
    tm = min(512, t)
    w = GROUP_WIDTH
    a0 = RWKV_COLS
    wr = w_in[:, :a0].astype(BF16)
    wa = w_in[:, a0:a0 + 4 * w].astype(BF16)
    wk = jnp.pad(w_in[:, a0 + 4 * w:], ((0, 0), (0, LANES - IDX_DIM - N_IDX_HEADS))).astype(BF16)
    row = lambda n: pl.BlockSpec((tm, n), lambda i: (i, 0))
    full = lambda a: pl.BlockSpec(a.shape, lambda i: (0, 0))
    sds = lambda n, dt: jax.ShapeDtypeStruct((t, n), dt)
    return pl.pallas_call(
        functools.partial(_inproj_kernel, q_scale=q_scale),
        grid=(t // tm,),
        in_specs=[row(d), pl.BlockSpec((1, d), lambda i: (0, 0)), _mod_spec(mod, 0, tm, d), _mod_spec(mod, 1, tm, d),
                  full(wr), full(wa), full(wk)],
        out_specs=[row(a0), row(w), row(w), row(IDX_DIM), row(N_IDX_HEADS), row(w), row(w), row(w), row(w),
                   row(IDX_DIM), pl.BlockSpec((w, tm), lambda i: (0, i))],
        out_shape=[sds(a0, F32), sds(w, F32), sds(w, F32), sds(IDX_DIM, F32), sds(N_IDX_HEADS, F32),
                   sds(w, BF16), sds(w, BF16), sds(w, BF16), sds(w, BF16), sds(IDX_DIM, BF16),
                   jax.ShapeDtypeStruct((w, t), BF16)],
        compiler_params=_cparams(("arbitrary",)),
    )(x, norm_g.reshape(1, d), mod, mod, wr, wa, wk)


def _bias_band_kernel(rb_ref, o_ref, *, sign, off, scale):
    _, rows, cols = o_ref.shape
    r = lax.broadcasted_iota(I32, (rows, cols), 0)
    c = lax.broadcasted_iota(I32, (rows, cols), 1)
    n = jnp.maximum(sign * (r - c) + off, 0)
    max_exact = N_BUCKETS // 2
    nf = jnp.maximum(n, 1).astype(F32)
    large = max_exact + (jnp.log(nf / max_exact) / math.log(MAX_DISTANCE / max_exact)
                         * (N_BUCKETS - max_exact)).astype(I32)
    large = jnp.minimum(large, N_BUCKETS - 1)
    bucket = jnp.where(n < max_exact, n, large)
    for h in range(N_HEADS):
        far = rb_ref[N_BUCKETS - 1, h]
        acc = jnp.zeros((rows, cols), F32)
        for b in range(N_BUCKETS - 1):
            acc = jnp.where(bucket == b, (rb_ref[b, h] - far) * scale, acc)
        o_ref[h] = acc


def _bias_band(rel_bias, rows, cols, sign, off, scale=1.0):
    return pl.pallas_call(
        functools.partial(_bias_band_kernel, sign=sign, off=off, scale=scale),
        in_specs=[pl.BlockSpec(memory_space=pltpu.SMEM)],
        out_specs=pl.BlockSpec((N_HEADS, rows, cols), lambda: (0, 0, 0)),
        out_shape=jax.ShapeDtypeStruct((N_HEADS, rows, cols), F32),
    )(rel_bias)


def _kth_largest_key(count_ge, lo0, hi0, topk):
    def cond(carry):
        return carry[2] > 0

    def body(carry):
        lo, hi, _ = carry
        mid = (lo | hi) - ((lo ^ hi) >> 1)
        cnt = count_ge(mid)
        active = lo < hi
        exact = jnp.logical_and(active, cnt == topk)
        ge = cnt >= topk
        lo_n = jnp.where(active, jnp.where(ge, mid, lo), lo)
        hi_n = jnp.where(active, jnp.where(exact, mid, jnp.where(ge, hi, mid - 1)), hi)
        return lo_n, hi_n, jnp.max(jnp.where(lo_n < hi_n, 1.0, 0.0))

    lo, _, _ = lax.while_loop(cond, body, (lo0, hi0, jnp.max(jnp.where(lo0 < hi0, 1.0, 0.0))))
    return lo


PQB = 256
PKT = 512
PKB = 1024
PSUB = 256
CNT_ROWS = 64


def _prompt_att_kernel(qs_ref, qib_ref, wit_ref, kib_ref, kb_ref, vt_ref, band_ref, o_ref,
                       keys_ref, tau_ref, m_ref, l_ref, acc_ref, s_ref, p_ref, a_ref, *, topk, qb_rows):
    qb = pl.program_id(0)
    j = pl.program_id(1)
    q_lo = qb * qb_rows
    n_kt = (q_lo + qb_rows + PKT - 1) // PKT

    @pl.when(j == 0)
    def _index_phase():
        q_pos = q_lo + lax.broadcasted_iota(I32, (PKT, qb_rows), 1)

        def tile_body(kt, carry):
            k0 = pl.multiple_of(kt * PKT, PKT)
            ki = kib_ref[pl.ds(k0, PKT), :]
            acc = jnp.zeros((PKT, qb_rows), F32)
            for h in range(N_IDX_HEADS):
                s = _dot_nt(ki, qib_ref[:, h * IDX_DIM:(h + 1) * IDX_DIM])
                acc = acc + jnp.maximum(s, 0.0) * wit_ref[h:h + 1, :]
            k_pos = k0 + lax.broadcasted_iota(I32, (PKT, qb_rows), 0)
            causal = k_pos <= q_pos
            keys_ref[kt] = jnp.where(causal, _float_key(acc), KEY_NEG_INF)
            smax, smin = carry
            grp = lambda x: x.reshape(PKT // SUBLANES, SUBLANES, qb_rows)
            smax = jnp.maximum(smax, jnp.max(grp(jnp.where(causal, acc, -jnp.inf)), axis=0))
            smin = jnp.minimum(smin, jnp.min(grp(jnp.where(causal, acc, jnp.inf)), axis=0))
            return smax, smin

        smax, smin = lax.fori_loop(
            0, n_kt, tile_body,
            (jnp.full((SUBLANES, qb_rows), -jnp.inf, F32), jnp.full((SUBLANES, qb_rows), jnp.inf, F32)))
        fmax = jnp.max(smax, axis=0, keepdims=True)
        fmin = jnp.min(smin, axis=0, keepdims=True)
        hi0 = jnp.where(fmax == 0.0, 0, _float_key(fmax))
        lo0 = jnp.where(fmin == 0.0, -1, _float_key(fmin))
        n_causal = q_lo + 1 + lax.broadcasted_iota(I32, (1, qb_rows), 1)
        lo0 = jnp.where(n_causal < topk, KEY_NEG_INF + 1, lo0)
        hi0 = jnp.where(n_causal < topk, KEY_NEG_INF + 1, hi0)

        def count_ge(mid):
            def cbody(kt, c):
                for r0 in range(0, PKT, CNT_ROWS):
                    ge = jnp.where(keys_ref[kt, r0:r0 + CNT_ROWS, :] >= mid, 1.0, 0.0)
                    c = c + jnp.sum(ge.reshape(CNT_ROWS // SUBLANES, SUBLANES, qb_rows), axis=0)
                return c

            c = lax.fori_loop(0, n_kt, cbody, jnp.zeros((SUBLANES, qb_rows), F32))
            return jnp.sum(c, axis=0, keepdims=True)

        tau = _kth_largest_key(count_ge, lo0, hi0, topk)
        tau_ref[...] = tau
        over_f = jnp.where(count_ge(tau) > topk, 1.0, 0.0)

        @pl.when(jnp.max(over_f) > 0.0)
        def _break_ties():
            need = topk - count_ge(tau + 1)
            k_row = lax.broadcasted_iota(I32, (PKT, qb_rows), 0)
            top_code = TIE_CODE + (q_lo + qb_rows - 1)

            def tag_body(kt, _):
                k = keys_ref[kt]
                keys_ref[kt] = jnp.where(k == tau, (top_code - kt * PKT) - k_row, k)
                return 0

            lax.fori_loop(0, n_kt, tag_body, 0)
            row = jnp.zeros((1, qb_rows), I32)
            cut = _kth_largest_key(count_ge, row + TIE_CODE, row + top_code, need)
            drop_to = jnp.where(over_f > 0.5, KEY_NEG_INF, tau)

            def untag_body(kt, _):
                k = keys_ref[kt]
                keys_ref[kt] = jnp.where(k >= TIE_CODE, jnp.where(k >= cut, tau, drop_to), k)
                return 0

            lax.fori_loop(0, n_kt, untag_body, 0)

        m_ref[...] = jnp.full(m_ref.shape, NEG_BIG, F32)
        l_ref[...] = jnp.zeros(l_ref.shape, F32)
        acc_ref[...] = jnp.zeros(acc_ref.shape, F32)

    kb = j - 1
    last_kb = (q_lo + qb_rows - 1) // PKB

    n_sub = PKB // PSUB
    hsl = [slice(h * HEAD_DIM, (h + 1) * HEAD_DIM) for h in range(N_HEADS)]

    def logits_stage(u, b):
        for h in range(N_HEADS):
            s_ref[b, h] = _dot_nt(kb_ref[u * PSUB:(u + 1) * PSUB, hsl[h]], qs_ref[:, hsl[h]])

    def softmax_stage(u, b, near):
        s0 = kb * PKB + u * PSUB
        kt = s0 // PKT
        c0 = (u * PSUB) % PKT
        mask_bias = jnp.where(keys_ref[kt, c0:c0 + PSUB, :] >= tau_ref[...], 0.0, NEG_BIG)
        if near:
            which = jnp.clip((q_lo - s0) // PSUB, 0, 1)
        for h in range(N_HEADS):
            s = s_ref[b, h] + mask_bias
            if near:
                s = s + band_ref[h, 1 - which]
            m_old = m_ref[h]
            m_new = jnp.maximum(m_old, jnp.max(s, axis=0, keepdims=True))
            p = jnp.exp2(s - m_new)
            alpha = jnp.exp2(m_old - m_new)
            l_ref[h] = alpha * l_ref[h] + jnp.sum(p, axis=0, keepdims=True)
            m_ref[h] = m_new
            a_ref[b, h] = alpha
            p_ref[b, h] = p.astype(BF16)

    def values_stage(u, b):
        for h in range(N_HEADS):
            pv = _dot(vt_ref[hsl[h], u * PSUB:(u + 1) * PSUB], p_ref[b, h])
            acc_ref[h] = acc_ref[h] * a_ref[b, h] + pv

    attend = jnp.logical_and(j >= 1, kb <= last_kb)
    block_far = kb * PKB + PKB <= q_lo - PSUB

    @pl.when(jnp.logical_and(attend, block_far))
    def _attend_far_block():
        logits_stage(0, 0)
        for u in range(n_sub):
            if u + 1 < n_sub:
                logits_stage(u + 1, (u + 1) % 2)
            softmax_stage(u, u % 2, False)
            values_stage(u, u % 2)

    @pl.when(jnp.logical_and(attend, jnp.logical_not(block_far)))
    def _attend_near_block():
        for u in range(n_sub):
            s0 = kb * PKB + u * PSUB
            is_near = s0 + PSUB > q_lo - PSUB
            in_range = s0 < q_lo + qb_rows

            def sub_tile(near, u=u):
                logits_stage(u, 0)
                softmax_stage(u, 0, near)
                values_stage(u, 0)

            @pl.when(jnp.logical_and(in_range, is_near))
            def _():
                sub_tile(True)

            @pl.when(jnp.logical_and(in_range, jnp.logical_not(is_near)))
            def _():
                sub_tile(False)

    @pl.when(j == last_kb + 1)
    def _finish():
        for h in range(N_HEADS):
            hs = slice(h * HEAD_DIM, (h + 1) * HEAD_DIM)
            o_ref[:, hs] = (acc_ref[h] / l_ref[h]).T


def _prompt_attention(qs, qib, wi, kib, kb, vt, rel_bias):
    s, w = qs.shape
    topk = min(TOPK_MAX, s // 4)
    qb_rows = min(PQB, s)
    assert s % qb_rows == 0 and s % PKB == 0 and qb_rows == PSUB
    nqb = s // qb_rows
    nkb = s // PKB
    band = _bias_band(rel_bias, 2 * PSUB, qb_rows, -1, PSUB, LOG2E).reshape(N_HEADS, 2, PSUB, qb_rows)

    def last_kb(i):
        return (i * qb_rows + qb_rows - 1) // PKB

    qrow = lambda n: pl.BlockSpec((qb_rows, n), lambda i, j: (i, 0))
    return pl.pallas_call(
        functools.partial(_prompt_att_kernel, topk=topk, qb_rows=qb_rows),
        grid=(nqb, nkb + 1),
        in_specs=[qrow(w), qrow(w), pl.BlockSpec((N_IDX_HEADS, qb_rows), lambda i, j: (0, i)),
                  pl.BlockSpec((s, IDX_DIM), lambda i, j: (0, 0)),
                  pl.BlockSpec((PKB, w), lambda i, j: (jnp.minimum(jnp.maximum(j - 1, 0), last_kb(i)), 0)),
                  pl.BlockSpec((w, PKB), lambda i, j: (0, jnp.minimum(jnp.maximum(j - 1, 0), last_kb(i)))),
                  pl.BlockSpec(band.shape, lambda i, j: (0, 0, 0, 0))],
        out_specs=qrow(w),
        out_shape=jax.ShapeDtypeStruct((s, w), F32),
        scratch_shapes=[pltpu.VMEM((s // PKT, PKT, qb_rows), I32),
                        pltpu.VMEM((1, qb_rows), I32),
                        pltpu.VMEM((N_HEADS, 1, qb_rows), F32),
                        pltpu.VMEM((N_HEADS, 1, qb_rows), F32),
                        pltpu.VMEM((N_HEADS, HEAD_DIM, qb_rows), F32),
                        pltpu.VMEM((2, N_HEADS, PSUB, qb_rows), F32),
                        pltpu.VMEM((2, N_HEADS, PSUB, qb_rows), BF16),
                        pltpu.VMEM((2, N_HEADS, 1, qb_rows), F32)],
        compiler_params=_cparams(("arbitrary", "arbitrary")),
    )(qs, qib, wi.T, kib, kb, vt, band)


RC = 16
RNB = 16
RBLK = RC * RNB


def _dot3(a, b):
    ah, al = _split2(a)
    bh, bl = _split2(b)
    return _dot(ah, bh) + (_dot(ah, bl) + _dot(al, bh))


def _dot3_nt(a, b):
    ah, al = _split2(a)
    bh, bl = _split2(b)
    return _dot_nt(ah, bh) + (_dot_nt(ah, bl) + _dot_nt(al, bh))


def _dot3_tn(a, b):
    dn = (((0,), (0,)), ((), ()))
    ah, al = _split2(a)
    bh, bl = _split2(b)
    f = lambda x, y: lax.dot_general(x, y, dn, preferred_element_type=F32)
    return f(ah, bh) + (f(ah, bl) + f(al, bh))


def _dot_exact_lhs(m_bf, x):
    hi, mid, lo = _split3(x)
    return _dot(m_bf, hi) + (_dot(m_bf, mid) + _dot(m_bf, lo))


def _dot_exact_rhs(x, m_bf):
    hi, mid, lo = _split3(x)
    return _dot(hi, m_bf) + (_dot(mid, m_bf) + _dot(lo, m_bf))


def _rwkv_kernel(*refs, chain, t_valid):
    if chain:
        p_ref, = refs[:1]
        rest = refs[1:]
    else:
        p_ref, pprev_ref, s0_ref = refs[:3]
        rest = refs[3:]
    (mu_ref, w0_ref, wd2_ref, a0_ref, wa2_ref, wg2_ref, kk_ref, ka_ref, rk_ref, lng_ref, lnb_ref,
     lt_ref, bo_ref, bd_ref, out_ref, sfin_ref,
     carry_ref, s_ref, wa_s, rq_s, uv_s, yv_s, bt_s, kt_s, v_s, gc_s, y_s,
     np_s, ti_s, ak_s, rb_s, rk_s) = rest
    i = pl.program_id(0)
    w = GROUP_WIDTH
    p = p_ref[...]
    row = lax.broadcasted_iota(I32, (RBLK, 1), 0)
    if chain:
        @pl.when(i == 0)
        def _():
            carry_ref[...] = jnp.zeros(carry_ref.shape, F32)
            s_ref[...] = jnp.zeros(s_ref.shape, F32)

        pprev = jnp.where(row == 0, carry_ref[...], pltpu.roll(p, 1, axis=0))
        carry_ref[...] = p[RBLK - 1:RBLK, :]
    else:
        pprev = pprev_ref[...]
    ps = p + (pprev - p) * mu_ref[...]
    r, k, v = ps[:, :w], ps[:, w:2 * w], ps[:, 2 * w:3 * w]
    o = 3 * w
    xw = ps[:, o:o + DECAY_LORA]
    xa = ps[:, o + DECAY_LORA:o + DECAY_LORA + AAA_LORA]
    xg = ps[:, o + DECAY_LORA + AAA_LORA:]
    dec = w0_ref[...] + _dot(jnp.tanh(xw).astype(BF16), wd2_ref[...])
    softplus = jnp.maximum(-dec, 0.0) + jnp.log(1.0 + jnp.exp(-jnp.abs(dec)))
    lw = -jnp.exp(-softplus - 0.5)
    a = jax.nn.sigmoid(a0_ref[...] + _dot(xa.astype(BF16), wa2_ref[...]))
    g = _dot(jax.nn.sigmoid(xg).astype(BF16), wg2_ref[...])
    kk = k * kk_ref[...]
    kk = kk * lax.rsqrt(jnp.maximum(_dot_exact_rhs(kk * kk, bd_ref[...]), 1e-24))
    k2 = k * (1.0 + (a - 1.0) * ka_ref[...])
    alpha = -kk
    beta = kk * a
    if t_valid < RC:
        valid = (row % RC) < t_valid
        zero = lambda x: jnp.where(valid, x, 0.0)
        lw, alpha, beta, k2, r, v = zero(lw), zero(alpha), zero(beta), zero(k2), zero(r), zero(v)
    cl = _dot_exact_lhs(lt_ref[...], lw)
    ct = _dot_exact_lhs(bo_ref[...], lw)
    g_in = jnp.exp(cl)
    g_ex = jnp.exp(cl - lw)
    g_inv = jnp.exp(-cl)
    g_end = jnp.exp(ct - cl)
    g_all = jnp.exp(ct)
    at, rt = alpha * g_ex, r * g_in
    bh, kh = beta * g_inv, k2 * g_inv
    bt, kt = beta * g_end, k2 * g_end

    ri = lax.broadcasted_iota(I32, (RBLK, RBLK), 0)
    ci = lax.broadcasted_iota(I32, (RBLK, RBLK), 1)
    same = (ri // RC) == (ci // RC)
    strict = jnp.logical_and(same, ci < ri)
    incl = jnp.logical_and(same, ci <= ri)
    eye = (ri == ci).astype(F32)

    heads = range(N_HEADS)
    hsl = [slice(h * HEAD_DIM, (h + 1) * HEAD_DIM) for h in heads]
    for h in heads:
        hs = hsl[h]
        gm = _dot3_nt(jnp.concatenate([at[:, hs], rt[:, hs]], axis=0),
                      jnp.concatenate([bh[:, hs], kh[:, hs]], axis=0))
        n1 = jnp.where(strict, gm[:RBLK, :RBLK], 0.0)
        np_s[h] = n1
        ti_s[h] = eye + n1
        ak_s[h] = jnp.where(strict, gm[:RBLK, RBLK:], 0.0)
        rb_s[h] = jnp.where(incl, gm[RBLK:, :RBLK], 0.0)
        rk_s[h] = jnp.where(incl, gm[RBLK:, RBLK:], 0.0)
        bt_s[h] = bt[:, hs]
        kt_s[h] = kt[:, hs]
        v_s[h] = v[:, hs]
        gc_s[h] = g_all[:, hs]
    for _ in range(int(math.log2(RC)) - 1):
        for h in heads:
            npow = np_s[h]
            np_s[h] = _dot3(npow, npow)
        for h in heads:
            tinv = ti_s[h]
            ti_s[h] = tinv + _dot3(np_s[h], tinv)
    for h in heads:
        uv_s[h] = _dot3(ak_s[h], v_s[h])
    for h in heads:
        wcat = _dot3(ti_s[h], jnp.concatenate([at[:, hsl[h]], uv_s[h]], axis=1))
        wa_s[h] = wcat[:, :HEAD_DIM]
        uv_s[h] = wcat[:, HEAD_DIM:]
    for h in heads:
        ry = _dot3(rb_s[h], jnp.concatenate([wa_s[h], uv_s[h]], axis=1))
        rq_s[h] = rt[:, hsl[h]] + ry[:, :HEAD_DIM]
        yv_s[h] = ry[:, HEAD_DIM:] + _dot3(rk_s[h], v_s[h])

    def chunk_body(c, _):
        c0 = pl.multiple_of(c * RC, RC)
        sl = pl.ds(c0, RC)
        s_old = [s0_ref[c, h] if not chain else s_ref[h] for h in heads]
        res = [_dot_nt(jnp.concatenate([wa_s[h, sl, :], rq_s[h, sl, :]], axis=0).astype(BF16),
                       s_old[h].astype(BF16)) for h in heads]
        for h in heads:
            y_s[h, sl, :] = res[h][RC:] + yv_s[h, sl, :]
        upd = [lax.dot_general(jnp.concatenate([res[h][:RC] + uv_s[h, sl, :], v_s[h, sl, :]], axis=0).astype(BF16),
                               jnp.concatenate([bt_s[h, sl, :], kt_s[h, sl, :]], axis=0).astype(BF16),
                               (((0,), (0,)), ((), ())), preferred_element_type=F32) for h in heads]
        for h in heads:
            s_new = s_old[h] * gc_s[h, pl.ds(c0, 1), :] + upd[h]
            if chain:
                s_ref[h] = s_new
            else:
                sfin_ref[c, h] = s_new
        return 0

    lax.fori_loop(0, RNB, chunk_body, 0)
    if chain:
        sfin_ref[...] = s_ref[...]

    for h in range(N_HEADS):
        hs = slice(h * HEAD_DIM, (h + 1) * HEAD_DIM)
        y = y_s[h]
        mean = jnp.mean(y, axis=-1, keepdims=True)
        var = jnp.mean(jnp.square(y - mean), axis=-1, keepdims=True)
        yn = (y - mean) * lax.rsqrt(var + GN_EPS) * lng_ref[:, hs] + lnb_ref[:, hs]
        bonus = jnp.sum(r[:, hs] * k2[:, hs] * rk_ref[:, hs], axis=-1, keepdims=True) * v[:, hs]
        out_ref[:, hs] = (yn + bonus) * g[:, hs]


def _rwkv_consts():
    idx = np.arange(RBLK)
    same = (idx[:, None] // RC) == (idx[None, :] // RC)
    lt = (same & (idx[None, :] <= idx[:, None])).astype(np.float32)
    bo = same.astype(np.float32)
    lane = np.arange(GROUP_WIDTH)
    bd = ((lane[:, None] // HEAD_DIM) == (lane[None, :] // HEAD_DIM)).astype(np.float32)
    return jnp.asarray(lt, BF16), jnp.asarray(bo, BF16), jnp.asarray(bd, BF16)


def _rwkv(p, lw, chain, pprev=None, s0=None, t_valid=RC):
    rows = p.shape[0]
    nblk = rows // RBLK
    w = GROUP_WIDTH
    vec = lambda a: a.reshape(1, -1)
    consts = [vec(lw['mu_shift']), vec(lw['w0']), lw['w_decay2'].astype(BF16), vec(lw['a0']),
              lw['w_a2'].astype(BF16), lw['w_g2'].astype(BF16), vec(lw['k_k']), vec(lw['k_a']), vec(lw['r_k']),
              vec(lw['lnx_g']), vec(lw['lnx_b'])] + list(_rwkv_consts())
    full = lambda a: pl.BlockSpec(a.shape, lambda i: (0,) * a.ndim)
    blk = pl.BlockSpec((RBLK, RWKV_COLS), lambda i: (i, 0))
    hshape = (N_HEADS, HEAD_DIM, HEAD_DIM)
    if chain:
        ins, in_specs = [p], [blk]
        sfin_spec = pl.BlockSpec(hshape, lambda i: (0, 0, 0))
        sfin_shape = jax.ShapeDtypeStruct(hshape, F32)
    else:
        sspec = pl.BlockSpec((RNB,) + hshape, lambda i: (i, 0, 0, 0))
        ins, in_specs = [p, pprev, s0], [blk, blk, sspec]
        sfin_spec = sspec
        sfin_shape = jax.ShapeDtypeStruct((nblk * RNB,) + hshape, F32)
    hm = lambda: pltpu.VMEM((N_HEADS, RBLK, HEAD_DIM), F32)
    return pl.pallas_call(
        functools.partial(_rwkv_kernel, chain=chain, t_valid=t_valid),
        grid=(nblk,),
        in_specs=in_specs + [full(c) for c in consts],
        out_specs=[pl.BlockSpec((RBLK, w), lambda i: (i, 0)), sfin_spec],
        out_shape=[jax.ShapeDtypeStruct((rows, w), F32), sfin_shape],
        scratch_shapes=[pltpu.VMEM((1, RWKV_COLS), F32), pltpu.VMEM(hshape, F32)] + [hm() for _ in range(9)]
        + [pltpu.VMEM((N_HEADS, RBLK, RBLK), F32) for _ in range(5)],
        compiler_params=_cparams(("arbitrary",)),
    )(*ins, *consts)


RT_TM = 256


def _route_kernel(x_ref, ro_ref, ao_ref, wor_ref, woa_ref, g1_ref, sh_ref, sc_ref, n2_ref, wr_ref, br_ref,
                  ltri_ref, x1_ref, h2_ref, idx_ref, gate_ref, rank_ref, cnt_ref, run_ref):
    @pl.when(pl.program_id(0) == 0)
    def _():
        run_ref[...] = jnp.zeros(run_ref.shape, F32)

    tm = x_ref.shape[0]
    mix = _dot(ro_ref[...].astype(BF16), wor_ref[...]) + _dot(ao_ref[...].astype(BF16), woa_ref[...])
    x1 = x_ref[...] + g1_ref[...] * mix
    x1_ref[...] = x1
    y = x1 * lax.rsqrt(jnp.mean(x1 * x1, axis=-1, keepdims=True) + RMS_EPS) * n2_ref[...]
    h2 = (y * (1.0 + sc_ref[...]) + sh_ref[...]).astype(BF16)
    h2_ref[...] = h2
    logits = _dot(h2, wr_ref[...]) + br_ref[...]
    lane = lax.broadcasted_iota(I32, (tm, LANES), 1)
    lane_f = lane.astype(F32)
    lg = logits
    vals, idxs = [], []
    for _ in range(TOP_K):
        m = jnp.max(lg, axis=1, keepdims=True)
        idx = jnp.min(jnp.where(lg == m, lane_f, float(LANES)), axis=1, keepdims=True)
        vals.append(m)
        idxs.append(idx)
        lg = jnp.where(lane_f == idx, -3e38, lg)
    es = [jnp.exp(v - vals[0]) for v in vals]
    den = es[0] + es[1] + es[2] + es[3]
    ohs = [(lane_f == idx).astype(F32) for idx in idxs]
    oh_all = ohs[0] + ohs[1] + ohs[2] + ohs[3]
    base = run_ref[...] + _dot(ltri_ref[...], oh_all.astype(BF16))
    idx_out = jnp.zeros((tm, LANES), F32)
    gate_out = jnp.zeros((tm, LANES), F32)
    rank_out = jnp.zeros((tm, LANES), F32)
    for kk in range(TOP_K):
        rank = jnp.sum(ohs[kk] * base, axis=1, keepdims=True)
        idx_out = jnp.where(lane == kk, idxs[kk], idx_out)
        gate_out = jnp.where(lane == kk, es[kk] / den, gate_out)
        rank_out = jnp.where(lane == kk, rank, rank_out)
    idx_ref[...] = idx_out[:, :TOP_K].astype(I32)
    gate_ref[...] = gate_out[:, :TOP_K]
    rank_ref[...] = rank_out[:, :TOP_K].astype(I32)
    run_ref[...] = run_ref[...] + jnp.sum(oh_all, axis=0, keepdims=True)
    cnt_ref[...] = run_ref[...]


def _out_proj_route(x, ro, ao, w_out, mod, norm2_g, w_router, b_router):
    t, d = x.shape
    tm = min(RT_TM, t)
    w = GROUP_WIDTH
    wor = w_out[:w].astype(BF16)
    woa = w_out[w:].astype(BF16)
    wr = jnp.pad(w_router, ((0, 0), (0, LANES - N_EXPERTS))).astype(BF16)
    br = jnp.pad(b_router.reshape(1, -1), ((0, 0), (0, LANES - N_EXPERTS)), constant_values=NEG_BIG)
    ltri = jnp.asarray(np.tril(np.ones((tm, tm), np.float32), -1), BF16)
    row = lambda n: pl.BlockSpec((tm, n), lambda i: (i, 0))
    full = lambda a: pl.BlockSpec(a.shape, lambda i: (0, 0))
    return pl.pallas_call(
        _route_kernel,
        grid=(t // tm,),
        in_specs=[row(d), row(w), row(w), full(wor), full(woa),
                  _mod_spec(mod, 2, tm, d), _mod_spec(mod, 3, tm, d), _mod_spec(mod, 4, tm, d),
                  pl.BlockSpec((1, d), lambda i: (0, 0)), full(wr), full(br), full(ltri)],
        out_specs=[row(d), row(d), row(TOP_K), row(TOP_K), row(TOP_K), pl.BlockSpec((1, LANES), lambda i: (0, 0))],
        out_shape=[jax.ShapeDtypeStruct((t, d), F32), jax.ShapeDtypeStruct((t, d), BF16),
                   jax.ShapeDtypeStruct((t, TOP_K), I32), jax.ShapeDtypeStruct((t, TOP_K), F32),
                   jax.ShapeDtypeStruct((t, TOP_K), I32), jax.ShapeDtypeStruct((1, LANES), F32)],
        scratch_shapes=[pltpu.VMEM((1, LANES), F32)],
        compiler_params=_cparams(("arbitrary",)),
    )(x, ro, ao, wor, woa, mod, mod, mod, norm2_g.reshape(1, d), wr, br, ltri)


EX_TM = 256
BF16_ROW = (SUBLANES, LANES)


def _dispatch_kernel(slot_ref, h_ref, init_ref, out_ref, sem):
    del init_ref
    tm = h_ref.shape[0]

    def body(r, _):
        for kk in range(TOP_K):
            pltpu.make_async_copy(h_ref.at[r], out_ref.at[slot_ref[r * TOP_K + kk]], sem).start()
        return 0

    lax.fori_loop(0, tm, body, 0)
    for _ in range(TOP_K):
        pltpu.make_async_copy(h_ref, out_ref.at[pl.ds(0, tm)], sem).wait()


def _dispatch(h2, slot, rows_sorted):
    t, d = h2.shape
    tm = min(RT_TM, t)
    assert d == SUBLANES * LANES
    h3 = h2.reshape((t,) + BF16_ROW)
    return pl.pallas_call(
        _dispatch_kernel,
        grid=(t // tm,),
        in_specs=[pl.BlockSpec((tm * TOP_K,), lambda i: (i,), memory_space=pltpu.SMEM),
                  pl.BlockSpec((tm,) + BF16_ROW, lambda i: (i, 0, 0)),
                  pl.BlockSpec(memory_space=pl.ANY)],
        out_specs=pl.BlockSpec(memory_space=pl.ANY),
        out_shape=jax.ShapeDtypeStruct(rows_sorted.shape, rows_sorted.dtype),
        scratch_shapes=[pltpu.SemaphoreType.DMA(())],
        input_output_aliases={2: 0},
        compiler_params=_cparams(("arbitrary",)),
    )(slot.reshape(-1), h3, rows_sorted)


def _expert_kernel(be_ref, nv_ref, x_ref, wgu_ref, bgu_ref, wd_ref, bd_ref, o_ref, wgu_bf, wd_bf):
    i = pl.program_id(0)
    changed = jnp.logical_or(i == 0, be_ref[i] != be_ref[jnp.maximum(i - 1, 0)])

    @pl.when(changed)
    def _():
        wgu_bf[...] = wgu_ref[0].astype(BF16)
        wd_bf[...] = wd_ref[0].astype(BF16)

    @pl.when(i < nv_ref[0])
    def _():
        f = wd_bf.shape[0]
        gu = _dot(x_ref[...], wgu_bf[...]) + bgu_ref[0]
        glu = jnp.minimum(gu[:, :f], SWIGLU_LIMIT)
        lin = jnp.clip(gu[:, f:], -SWIGLU_LIMIT, SWIGLU_LIMIT)
        act = glu * jax.nn.sigmoid(SWIGLU_ALPHA * glu) * (lin + 1.0)
        o_ref[...] = _dot(act.astype(BF16), wd_bf[...]) + bd_ref[0]

    @pl.when(i >= nv_ref[0])
    def _():
        o_ref[...] = jnp.zeros(o_ref.shape, F32)


def _experts(rows_sorted, blk_e, n_valid, w_gu, b_gu, w_down, b_down):
    ns, d = rows_sorted.shape
    e, _, f2 = w_gu.shape
    f = f2 // 2
    grid_spec = pltpu.PrefetchScalarGridSpec(
        num_scalar_prefetch=2,
        grid=(ns // EX_TM,),
        in_specs=[pl.BlockSpec((EX_TM, d), lambda i, be, nv: (i, 0)),
                  pl.BlockSpec((1, d, f2), lambda i, be, nv: (be[i], 0, 0)),
                  pl.BlockSpec((1, 1, f2), lambda i, be, nv: (be[i], 0, 0)),
                  pl.BlockSpec((1, f, d), lambda i, be, nv: (be[i], 0, 0)),
                  pl.BlockSpec((1, 1, d), lambda i, be, nv: (be[i], 0, 0))],
        out_specs=pl.BlockSpec((EX_TM, d), lambda i, be, nv: (i, 0)),
        scratch_shapes=[pltpu.VMEM((d, f2), BF16), pltpu.VMEM((f, d), BF16)])
    return pl.pallas_call(
        _expert_kernel,
        grid_spec=grid_spec,
        out_shape=jax.ShapeDtypeStruct((ns, d), F32),
        compiler_params=_cparams(("arbitrary",)),
    )(blk_e, n_valid, rows_sorted, w_gu, b_gu.reshape(e, 1, f2), w_down, b_down.reshape(e, 1, d))


def _combine_kernel(slot_ref, rows_ref, gate_ref, x1_ref, g2_ref, nf_ref, y_ref, buf, sem):
    tm = x1_ref.shape[0]

    def body(r, _):
        for kk in range(TOP_K):
            pltpu.make_async_copy(rows_ref.at[pl.ds(slot_ref[r * TOP_K + kk], 1)],
                                  buf.at[kk, pl.ds(r, 1)], sem).start()
        return 0

    lax.fori_loop(0, tm, body, 0)
    for kk in range(TOP_K):
        pltpu.make_async_copy(rows_ref.at[pl.ds(0, tm)], buf.at[kk], sem).wait()
    gates = gate_ref[...]
    moe = gates[:, 0:1] * buf[0]
    for kk in range(1, TOP_K):
        moe = moe + gates[:, kk:kk + 1] * buf[kk]
    x2 = x1_ref[...] + g2_ref[...] * moe
    y_ref[...] = x2 * lax.rsqrt(jnp.mean(x2 * x2, axis=-1, keepdims=True) + RMS_EPS) * nf_ref[...]


def _combine(out_rows, slot, gates, x1, mod, normf_g):
    t, d = x1.shape
    tm = min(RT_TM, t)
    return pl.pallas_call(
        _combine_kernel,
        grid=(t // tm,),
        in_specs=[pl.BlockSpec((tm * TOP_K,), lambda i: (i,), memory_space=pltpu.SMEM),
                  pl.BlockSpec(memory_space=pl.ANY),
                  pl.BlockSpec((tm, TOP_K), lambda i: (i, 0)),
                  pl.BlockSpec((tm, d), lambda i: (i, 0)), _mod_spec(mod, 5, tm, d),
                  pl.BlockSpec((1, d), lambda i: (0, 0))],
        out_specs=pl.BlockSpec((tm, d), lambda i: (i, 0)),
        out_shape=jax.ShapeDtypeStruct((t, d), F32),
        scratch_shapes=[pltpu.VMEM((TOP_K, tm, d), F32), pltpu.SemaphoreType.DMA(())],
        compiler_params=_cparams(("arbitrary",)),
    )(slot.reshape(-1), out_rows, gates, x1, mod, normf_g.reshape(1, d))


def _moe_plan(idx_p, rank_p, cnt_p, idx_s, rank_s, cnt_s):
    n_assign = idx_p.size + idx_s.size
    n_tiles = -(-n_assign // EX_TM) + N_EXPERTS
    cp = cnt_p[0, :N_EXPERTS].astype(I32)
    cs = cnt_s[0, :N_EXPERTS].astype(I32)
    padded = (cp + cs + EX_TM - 1) // EX_TM * EX_TM
    pad_end = jnp.cumsum(padded)
    pad_start = pad_end - padded
    slot_p = pad_start[idx_p] + rank_p
    slot_s = pad_start[idx_s] + cp[idx_s] + rank_s
    tile_row = jnp.arange(n_tiles, dtype=I32)[:, None] * EX_TM
    blk_e = jnp.minimum(jnp.sum((pad_end[None, :] <= tile_row).astype(I32), axis=1), N_EXPERTS - 1)
    n_valid = (pad_end[-1:] // EX_TM).astype(I32)
    return slot_p, slot_s, blk_e, n_valid, n_tiles * EX_TM


PG = 8
QROWS = N_HEADS * SUBLANES


def _page_specs(block, n_pages):
    def spec(u):
        return pl.BlockSpec(block, lambda b, j, pt: (pt[b * n_pages + j * PG + u],) + (0,) * (len(block) - 1))
    return [spec(u) for u in range(PG)]


def _head_sum(x):
    out = x[:SUBLANES]
    for h in range(1, N_HEADS):
        out = out + x[h * SUBLANES:(h + 1) * SUBLANES]
    return out


def _sample_index_kernel(pt_ref, qi_ref, wrep_ref, kin_ref, *rest, n_pages, t_new, topk):
    del pt_ref
    pages = rest[:PG]
    keys_ref, tau_ref = rest[PG:]
    j = pl.program_id(1)
    qi = qi_ref[0]
    wrep = wrep_ref[0]

    def scores(ki_t_bf):
        s = _dot(qi, ki_t_bf)
        return _head_sum(jnp.maximum(s, 0.0) * wrep)

    for u in range(PG):
        keys_ref[0, j * PG + u] = _float_key(scores(pages[u][0].astype(BF16)))

    @pl.when(j == pl.num_programs(1) - 1)
    def _():
        qrow = lax.broadcasted_iota(I32, (SUBLANES, PAGE_SIZE), 0)
        col = lax.broadcasted_iota(I32, (SUBLANES, PAGE_SIZE), 1)
        ok = jnp.logical_and(col <= qrow, col < t_new)
        keys_ref[0, n_pages] = jnp.where(ok, _float_key(scores(kin_ref[0])), KEY_NEG_INF)

        def count_ge(mid):
            c = jnp.sum(jnp.where(keys_ref[0] >= mid[None], 1.0, 0.0), axis=0)
            return jnp.broadcast_to(jnp.sum(c, axis=1, keepdims=True), (SUBLANES, LANES))

        tau = _kth_largest_key(count_ge, jnp.full((SUBLANES, LANES), KEY_NEG_INF + 1, I32),
                               jnp.full((SUBLANES, LANES), 0x7F800000, I32), topk)
        tau_ref[0] = tau
        over_f = jnp.where(count_ge(tau) > topk, 1.0, 0.0)

        @pl.when(jnp.max(over_f) > 0.0)
        def _break_ties():
            need = topk - count_ge(tau + 1)
            shape = keys_ref.shape[1:]
            pos = lax.broadcasted_iota(I32, shape, 0) * PAGE_SIZE + lax.broadcasted_iota(I32, shape, 2)
            tied = keys_ref[0] == tau[None]

            def pbody(_, carry):
                lo, hi = carry
                mid = (lo + hi) >> 1
                c = jnp.sum(jnp.where(tied, jnp.where(pos <= mid[None], 1.0, 0.0), 0.0), axis=0)
                ok = jnp.broadcast_to(jnp.sum(c, axis=1, keepdims=True), (SUBLANES, LANES)) >= need
                return jnp.where(ok, lo, mid + 1), jnp.where(ok, mid, hi)

            n_pos = (n_pages + 1) * PAGE_SIZE
            last_pos, _ = lax.fori_loop(0, (n_pos - 1).bit_length(), pbody,
                                        (jnp.zeros((SUBLANES, LANES), I32), jnp.full((SUBLANES, LANES), n_pos - 1, I32)))
            gone = jnp.where(tied, jnp.where(pos > last_pos[None], over_f[None], 0.0), 0.0)
            keys_ref[0] = jnp.where(gone > 0.5, KEY_NEG_INF, keys_ref[0])


def _sample_attend_kernel(pt_ref, q_ref, keys_ref, tau_ref, band_ref, kn_ref, vn_ref, *rest, n_pages):
    del pt_ref
    kpages = rest[:PG]
    vpages = rest[PG:2 * PG]
    o_ref, m_ref, l_ref, acc_ref = rest[2 * PG:]
    j = pl.program_id(1)
    last = j == pl.num_programs(1) - 1

    @pl.when(j == 0)
    def _():
        m_ref[...] = jnp.full(m_ref.shape, NEG_BIG, F32)
        l_ref[...] = jnp.zeros(l_ref.shape, F32)
        acc_ref[...] = jnp.zeros(acc_ref.shape, F32)

    tau = tau_ref[0]
    heads = range(N_HEADS)

    def attend(k_of, v_of, key_tiles, bias_of):
        pages = range(len(key_tiles))
        sel = [kt >= tau for kt in key_tiles]
        s = [[_dot(q_ref[0, h], k_of(u, h)) for u in pages] for h in heads]
        ps, alphas = [], []
        for h in heads:
            sh = []
            for u in pages:
                b_uh = bias_of(u, h)
                sh.append(jnp.where(sel[u], s[h][u] if b_uh is None else s[h][u] + b_uh, NEG_BIG))
            tile_max = functools.reduce(jnp.maximum, sh)
            m_old = m_ref[h]
            m_new = jnp.maximum(m_old, jnp.broadcast_to(jnp.max(tile_max, axis=1, keepdims=True), m_old.shape))
            p = [jnp.exp(x - m_new) for x in sh]
            alpha = jnp.exp(m_old - m_new)
            p_sum = functools.reduce(jnp.add, p)
            l_ref[h] = alpha * l_ref[h] + jnp.broadcast_to(jnp.sum(p_sum, axis=1, keepdims=True), m_old.shape)
            m_ref[h] = m_new
            ps.append([x.astype(BF16) for x in p])
            alphas.append(alpha[:, :HEAD_DIM])
        pv = [[_dot_nt(ps[h][u], v_of(u, h)) for u in pages] for h in heads]
        for h in heads:
            acc_ref[h] = acc_ref[h] * alphas[h] + functools.reduce(jnp.add, pv[h])

    attend(lambda u, h: kpages[u][0, h].astype(BF16), lambda u, h: vpages[u][0, h].astype(BF16),
           [keys_ref[0, j * PG + u] for u in range(PG)],
           lambda u, h: jnp.where(last, band_ref[h, :, :PAGE_SIZE], 0.0) if u == PG - 1 else None)

    @pl.when(last)
    def _():
        hsl = lambda h: slice(h * HEAD_DIM, (h + 1) * HEAD_DIM)
        attend(lambda u, h: kn_ref[0, hsl(h), :], lambda u, h: vn_ref[0, hsl(h), :], [keys_ref[0, n_pages]],
               lambda u, h: band_ref[h, :, PAGE_SIZE:])
        for h in heads:
            o_ref[0, :, hsl(h)] = acc_ref[h] / l_ref[h][:, :HEAD_DIM]


def _sample_attention(qs, qib, wi, k_new_bf, v_new_bf, ki_new_bf, cache_k, cache_v, cache_kidx, page_table,
                      rel_bias, b, t_new):
    w = GROUP_WIDTH
    n_pages = page_table.shape[1]
    assert n_pages % PG == 0 and t_new <= SUBLANES and cache_k.shape[1] == PAGE_SIZE
    past = n_pages * PAGE_SIZE
    topk = min(TOPK_MAX, (past + t_new) // 4)
    pt = page_table.reshape(-1)
    padq = lambda a: jnp.pad(a, ((0, 0), (0, SUBLANES - t_new)) + ((0, 0),) * (a.ndim - 2))
    padk = lambda a: jnp.pad(a.reshape(b, t_new, -1), ((0, 0), (0, PAGE_SIZE - t_new), (0, 0))).transpose(0, 2, 1)
    qi_r = padq(qib.reshape(b, t_new, N_IDX_HEADS, IDX_DIM)).transpose(0, 2, 1, 3).reshape(b, QROWS, IDX_DIM)
    w_r = padq(wi.reshape(b, t_new, N_IDX_HEADS)).transpose(0, 2, 1).reshape(b, QROWS, 1)
    w_r = jnp.broadcast_to(w_r, (b, QROWS, LANES))
    q4 = padq(qs.reshape(b, t_new, N_HEADS, HEAD_DIM)).transpose(0, 2, 1, 3)
    kin, kn, vn = padk(ki_new_bf), padk(k_new_bf), padk(v_new_bf)
    band = _bias_band(rel_bias, SUBLANES, 2 * PAGE_SIZE, 1, PAGE_SIZE)
    ck_t = cache_k.transpose(0, 2, 3, 1)
    cv_t = cache_v.transpose(0, 2, 3, 1)
    cki_t = cache_kidx.transpose(0, 2, 1)
    page_block = (1, N_HEADS, HEAD_DIM, PAGE_SIZE)

    per_b = lambda shape: pl.BlockSpec((1,) + shape, lambda bb, j, p_: (bb,) + (0,) * len(shape))
    steps = n_pages // PG
    keys, tau = pl.pallas_call(
        functools.partial(_sample_index_kernel, n_pages=n_pages, t_new=t_new, topk=topk),
        grid_spec=pltpu.PrefetchScalarGridSpec(
            num_scalar_prefetch=1, grid=(b, steps),
            in_specs=[per_b((QROWS, IDX_DIM)), per_b((QROWS, LANES)), per_b((IDX_DIM, PAGE_SIZE))]
            + _page_specs((1, IDX_DIM, PAGE_SIZE), n_pages),
            out_specs=[per_b((n_pages + 1, SUBLANES, PAGE_SIZE)), per_b((SUBLANES, LANES))]),
        out_shape=[jax.ShapeDtypeStruct((b, n_pages + 1, SUBLANES, PAGE_SIZE), I32),
                   jax.ShapeDtypeStruct((b, SUBLANES, LANES), I32)],
        compiler_params=_cparams(("arbitrary", "arbitrary")),
    )(pt, qi_r, w_r, kin, *([cki_t] * PG))
    out = pl.pallas_call(
        functools.partial(_sample_attend_kernel, n_pages=n_pages),
        grid_spec=pltpu.PrefetchScalarGridSpec(
            num_scalar_prefetch=1, grid=(b, steps),
            in_specs=[per_b((N_HEADS, SUBLANES, HEAD_DIM)), per_b((n_pages + 1, SUBLANES, PAGE_SIZE)),
                      per_b((SUBLANES, LANES)), pl.BlockSpec(band.shape, lambda bb, j, p_: (0, 0, 0)),
                      per_b((w, PAGE_SIZE)), per_b((w, PAGE_SIZE))]
            + _page_specs(page_block, n_pages) + _page_specs(page_block, n_pages),
            out_specs=per_b((SUBLANES, w)),
            scratch_shapes=[pltpu.VMEM((N_HEADS, SUBLANES, LANES), F32), pltpu.VMEM((N_HEADS, SUBLANES, LANES), F32),
                            pltpu.VMEM((N_HEADS, SUBLANES, HEAD_DIM), F32)]),
        out_shape=jax.ShapeDtypeStruct((b, SUBLANES, w), F32),
        compiler_params=_cparams(("arbitrary", "arbitrary")),
    )(pt, q4, keys, tau, band, kn, vn, *([ck_t] * PG), *([cv_t] * PG))
    return out[:, :t_new].reshape(b * t_new, w)


def kernel(x_prompt, x_sample, c_prompt, c_sample, cache_k, cache_v, cache_kidx, page_table, state_wkv,
           state_shift, w_ada, b_ada, norm1_g, w_in, mu_shift, w0, w_decay2, a0, w_a2, w_g2, k_k, k_a, r_k,
           lnx_g, lnx_b, rel_bias, w_out, norm2_g, w_router, b_router, w_gu, b_gu, w_down, b_down, normf_g):
    depth = w_in.shape[0]
    assert depth == 1, "the merged prompt+sample expert pass is written for a single layer"
    bp, sp, d = x_prompt.shape
    bs, ts, _ = x_sample.shape
    assert bp == 1 and sp % RBLK == 0 and bs % RNB == 0 and ts <= RC
    l = 0
    lw = {'mu_shift': mu_shift[l], 'w0': w0[l], 'w_decay2': w_decay2[l], 'a0': a0[l], 'w_a2': w_a2[l],
          'w_g2': w_g2[l], 'k_k': k_k[l], 'k_a': k_a[l], 'r_k': r_k[l], 'lnx_g': lnx_g[l], 'lnx_b': lnx_b[l]}
    ns = bs * ts

    c_all = jnp.concatenate([c_prompt, c_sample], axis=0)
    n_c = c_all.shape[0]
    c_all = jnp.pad(c_all, ((0, -n_c % SUBLANES), (0, 0)))
    mod = _adaln(c_all, w_ada[l], b_ada[l])
    mod_p = mod[0:1]
    mod_s = jnp.repeat(mod[bp:bp + bs], ts, axis=0)

    xp = x_prompt.reshape(sp, d)
    xs = x_sample.reshape(ns, d)
    pr_p, k_p, v_p, ki_p, wi_p, qs_p, kb_p, _, qib_p, kib_p, vt_p = _in_proj(xp, norm1_g[l], mod_p, w_in[l],
                                                                             HEAD_DIM ** -0.5 * LOG2E)
    pr_s, k_s, v_s, ki_s, wi_s, qs_s, kb_s, vb_s, qib_s, kib_s, _ = _in_proj(xs, norm1_g[l], mod_s, w_in[l],
                                                                             HEAD_DIM ** -0.5)

    ro_p, wkv_p = _rwkv(pr_p, lw, chain=True)
    att_p = _prompt_attention(qs_p, qib_p, wi_p, kib_p, kb_p, vt_p, rel_bias)

    pr_s3 = pr_s.reshape(bs, ts, RWKV_COLS)
    prev_s3 = jnp.concatenate([state_shift[l][:, None, :], pr_s3[:, :-1]], axis=1)
    padc = lambda a: jnp.pad(a, ((0, 0), (0, RC - ts), (0, 0))).reshape(bs * RC, RWKV_COLS)
    ro_s, wkv_s = _rwkv(padc(pr_s3), lw, chain=False, pprev=padc(prev_s3), s0=state_wkv[l], t_valid=ts)
    ro_s = ro_s.reshape(bs, RC, GROUP_WIDTH)[:, :ts].reshape(ns, GROUP_WIDTH)
    att_s = _sample_attention(qs_s, qib_s, wi_s, kb_s, vb_s, kib_s, cache_k[l], cache_v[l], cache_kidx[l],
                              page_table, rel_bias, bs, ts)

    x1_p, h2_p, idx_p, gate_p, rank_p, cnt_p = _out_proj_route(xp, ro_p, att_p, w_out[l], mod_p, norm2_g[l],
                                                               w_router[l], b_router[l])
    x1_s, h2_s, idx_s, gate_s, rank_s, cnt_s = _out_proj_route(xs, ro_s, att_s, w_out[l], mod_s, norm2_g[l],
                                                               w_router[l], b_router[l])
    slot_p, slot_s, blk_e, n_valid, n_slots = _moe_plan(idx_p, rank_p, cnt_p, idx_s, rank_s, cnt_s)
    rows = jnp.zeros((n_slots,) + BF16_ROW, BF16)
    rows = _dispatch(h2_p, slot_p, rows)
    rows = _dispatch(h2_s, slot_s, rows)
    out_rows = _experts(rows.reshape(n_slots, d), blk_e, n_valid, w_gu[l], b_gu[l], w_down[l], b_down[l])
    y_p = _combine(out_rows, slot_p, gate_p, x1_p, mod_p, normf_g)
    y_s = _combine(out_rows, slot_s, gate_s, x1_s, mod_s, normf_g)

    hd = (N_HEADS, HEAD_DIM)
    return (y_p.reshape(bp, sp, d), y_s.reshape(bs, ts, d),
            k_p.reshape((1, bp, sp) + hd), v_p.reshape((1, bp, sp) + hd), ki_p.reshape(1, bp, sp, IDX_DIM),
            wkv_p.reshape((1, bp) + (N_HEADS, HEAD_DIM, HEAD_DIM)), pr_p[sp - 1:sp].reshape(1, bp, RWKV_COLS),
            k_s.reshape((1, bs, ts) + hd), v_s.reshape((1, bs, ts) + hd), ki_s.reshape(1, bs, ts, IDX_DIM),
            wkv_s.reshape((1, bs) + (N_HEADS, HEAD_DIM, HEAD_DIM)), pr_s3[:, ts - 1].reshape(1, bs, RWKV_COLS))
```

```python
import functools
import math

import jax
import jax.numpy as jnp
import numpy as np
from jax import lax
from jax.experimental import pallas as pl
from jax.experimental.pallas import tpu as pltpu

F32 = jnp.float32
BF16 = jnp.bfloat16
I32 = jnp.int32

HEAD_DIM = 64
N_HEADS = 8
GROUP_WIDTH = N_HEADS * HEAD_DIM
DECAY_LORA, AAA_LORA, GATE_LORA = 64, 64, 128
RWKV_COLS = 3 * GROUP_WIDTH + DECAY_LORA + AAA_LORA + GATE_LORA
IDX_DIM = 64
N_IDX_HEADS = 8
TOPK_MAX = 256
N_BUCKETS = 32
MAX_DISTANCE = 128
N_EXPERTS = 32
TOP_K = 4
SWIGLU_LIMIT = 7.0
SWIGLU_ALPHA = 1.702
RMS_EPS = 1e-6
GN_EPS = HEAD_DIM * 1e-5
PAGE_SIZE = 128

LANES = 128
SUBLANES = 8
VMEM_LIMIT = 56 * 1024 * 1024

NEG_BIG = -1e30
LOG2E = math.log2(math.e)
KEY_NEG_INF = -2139095041
TIE_CODE = 0x7F800001


def _cparams(sem):
    return pltpu.CompilerParams(dimension_semantics=sem, vmem_limit_bytes=VMEM_LIMIT)


def _dot(a, b):
    return jnp.dot(a, b, preferred_element_type=F32)


def _dot_nt(a, b):
    return lax.dot_general(a, b, (((1,), (1,)), ((), ())), preferred_element_type=F32)


def _split3(a):
    hi = a.astype(BF16)
    r1 = a - hi.astype(F32)
    mid = r1.astype(BF16)
    lo = (r1 - mid.astype(F32)).astype(BF16)
    return hi, mid, lo


def _rep(x, n, axis):
    return jnp.concatenate([x] * n, axis=axis)


def _float_key(x):
    b = pltpu.bitcast(x, I32)
    return b ^ ((b >> 31) & 0x7FFFFFFF)


def _ada_kernel(c_ref, w_ref, b_ref, o_ref):
    c = c_ref[...]
    s = c * jax.nn.sigmoid(c)
    o_ref[...] = _dot(s.astype(BF16), w_ref[...].astype(BF16)) + b_ref[...]


def _adaln(c, w_ada, b_ada):
    r, d = c.shape
    n = w_ada.shape[1]
    tn = 1536
    return pl.pallas_call(
        _ada_kernel,
        grid=(n // tn,),
        in_specs=[pl.BlockSpec((r, d), lambda j: (0, 0)),
                  pl.BlockSpec((d, tn), lambda j: (0, j)),
                  pl.BlockSpec((1, tn), lambda j: (0, j))],
        out_specs=pl.BlockSpec((r, tn), lambda j: (0, j)),
        out_shape=jax.ShapeDtypeStruct((r, n), F32),
        compiler_params=_cparams(("arbitrary",)),
    )(c, w_ada, b_ada.reshape(1, n))


def _inproj_kernel(x_ref, g_ref, sh_ref, sc_ref, wr_ref, wa_ref, wk_ref,
                   pr_ref, k_ref, v_ref, ki_ref, wi_ref, qs_ref, kb_ref, vb_ref, qib_ref, kib_ref, vt_ref, *, q_scale):
    x = x_ref[...]
    y = x * lax.rsqrt(jnp.mean(x * x, axis=-1, keepdims=True) + RMS_EPS) * g_ref[...]
    h = (y * (1.0 + sc_ref[...]) + sh_ref[...]).astype(BF16)
    pr_ref[...] = _dot(h, wr_ref[...])
    a = _dot(h, wa_ref[...])
    w = GROUP_WIDTH
    q, k, v, qi = a[:, :w], a[:, w:2 * w], a[:, 2 * w:3 * w], a[:, 3 * w:4 * w]
    k_ref[...] = k
    v_ref[...] = v
    qs_ref[...] = (q * q_scale).astype(BF16)
    kb_ref[...] = k.astype(BF16)
    vb_ref[...] = v.astype(BF16)
    vt_ref[...] = v.T.astype(BF16)
    qib_ref[...] = (qi * IDX_DIM ** -0.5).astype(BF16)
    kw = _dot(h, wk_ref[...])
    ki = kw[:, :IDX_DIM]
    ki_ref[...] = ki
    kib_ref[...] = ki.astype(BF16)
    wi_ref[...] = kw[:, IDX_DIM:IDX_DIM + N_IDX_HEADS] * N_IDX_HEADS ** -0.5


def _mod_spec(mod, col, tm, d):
    if mod.shape[0] == 1:
        return pl.BlockSpec((1, d), lambda i: (0, col))
    return pl.BlockSpec((tm, d), lambda i: (i, col))


def _in_proj(x, norm_g, mod, w_in, q_scale):
    t, d = x.shape
    tm = min(512, t)
    w = GROUP_WIDTH
    a0 = RWKV_COLS
    wr = w_in[:, :a0].astype(BF16)
    wa = w_in[:, a0:a0 + 4 * w].astype(BF16)
    wk = jnp.pad(w_in[:, a0 + 4 * w:], ((0, 0), (0, LANES - IDX_DIM - N_IDX_HEADS))).astype(BF16)
    row = lambda n: pl.BlockSpec((tm, n), lambda i: (i, 0))
    full = lambda a: pl.BlockSpec(a.shape, lambda i: (0, 0))
    sds = lambda n, dt: jax.ShapeDtypeStruct((t, n), dt)
    return pl.pallas_call(
        functools.partial(_inproj_kernel, q_scale=q_scale),
        grid=(t // tm,),
        in_specs=[row(d), pl.BlockSpec((1, d), lambda i: (0, 0)), _mod_spec(mod, 0, tm, d), _mod_spec(mod, 1, tm, d),
                  full(wr), full(wa), full(wk)],
        out_specs=[row(a0), row(w), row(w), row(IDX_DIM), row(N_IDX_HEADS), row(w), row(w), row(w), row(w),
                   row(IDX_DIM), pl.BlockSpec((w, tm), lambda i: (0, i))],
        out_shape=[sds(a0, F32), sds(w, F32), sds(w, F32), sds(IDX_DIM, F32), sds(N_IDX_HEADS, F32),
                   sds(w, BF16), sds(w, BF16), sds(w, BF16), sds(w, BF16), sds(IDX_DIM, BF16),
                   jax.ShapeDtypeStruct((w, t), BF16)],
        compiler_params=_cparams(("arbitrary",)),
    )(x, norm_g.reshape(1, d), mod, mod, wr, wa, wk)


def _bias_band_kernel(rb_ref, o_ref, *, sign, off, scale):
    _, rows, cols = o_ref.shape
    r = lax.broadcasted_iota(I32, (rows, cols), 0)
    c = lax.broadcasted_iota(I32, (rows, cols), 1)
    n = jnp.maximum(sign * (r - c) + off, 0)
    max_exact = N_BUCKETS // 2
    nf = jnp.maximum(n, 1).astype(F32)
    large = max_exact + (jnp.log(nf / max_exact) / math.log(MAX_DISTANCE / max_exact)
                         * (N_BUCKETS - max_exact)).astype(I32)
    large = jnp.minimum(large, N_BUCKETS - 1)
    bucket = jnp.where(n < max_exact, n, large)
    for h in range(N_HEADS):
        far = rb_ref[N_BUCKETS - 1, h]
        acc = jnp.zeros((rows, cols), F32)
        for b in range(N_BUCKETS - 1):
            acc = jnp.where(bucket == b, (rb_ref[b, h] - far) * scale, acc)
        o_ref[h] = acc


def _bias_band(rel_bias, rows, cols, sign, off, scale=1.0):
    return pl.pallas_call(
        functools.partial(_bias_band_kernel, sign=sign, off=off, scale=scale),
        in_specs=[pl.BlockSpec(memory_space=pltpu.SMEM)],
        out_specs=pl.BlockSpec((N_HEADS, rows, cols), lambda: (0, 0, 0)),
        out_shape=jax.ShapeDtypeStruct((N_HEADS, rows, cols), F32),
    )(rel_bias)


def _kth_largest_key(count_ge, lo0, hi0, topk):
    def cond(carry):
        return carry[2] > 0

    def body(carry):
        lo, hi, _ = carry
        mid = (lo | hi) - ((lo ^ hi) >> 1)
        cnt = count_ge(mid)
        active = lo < hi
        exact = jnp.logical_and(active, cnt == topk)
        ge = cnt >= topk
        lo_n = jnp.where(active, jnp.where(ge, mid, lo), lo)
        hi_n = jnp.where(active, jnp.where(exact, mid, jnp.where(ge, hi, mid - 1)), hi)
        return lo_n, hi_n, jnp.max(jnp.where(lo_n < hi_n, 1.0, 0.0))

    lo, _, _ = lax.while_loop(cond, body, (lo0, hi0, jnp.max(jnp.where(lo0 < hi0, 1.0, 0.0))))
    return lo


PQB = 256
PKT = 512
PKB = 1024
PSUB = 256
CNT_ROWS = 64


def _prompt_att_kernel(qs_ref, qib_ref, wit_ref, kib_ref, kb_ref, vt_ref, band_ref, o_ref,
                       keys_ref, tau_ref, m_ref, l_ref, acc_ref, s_ref, p_ref, a_ref, *, topk, qb_rows):
    qb = pl.program_id(0)
    j = pl.program_id(1)
    q_lo = qb * qb_rows
    n_kt = (q_lo + qb_rows + PKT - 1) // PKT

    @pl.when(j == 0)
    def _index_phase():
        q_pos = q_lo + lax.broadcasted_iota(I32, (PKT, qb_rows), 1)

        def tile_body(kt, carry):
            k0 = pl.multiple_of(kt * PKT, PKT)
            ki = kib_ref[pl.ds(k0, PKT), :]
            acc = jnp.zeros((PKT, qb_rows), F32)
            for h in range(N_IDX_HEADS):
                s = _dot_nt(ki, qib_ref[:, h * IDX_DIM:(h + 1) * IDX_DIM])
                acc = acc + jnp.maximum(s, 0.0) * wit_ref[h:h + 1, :]
            k_pos = k0 + lax.broadcasted_iota(I32, (PKT, qb_rows), 0)
            causal = k_pos <= q_pos
            keys_ref[kt] = jnp.where(causal, _float_key(acc), KEY_NEG_INF)
            smax, smin = carry
            grp = lambda x: x.reshape(PKT // SUBLANES, SUBLANES, qb_rows)
            smax = jnp.maximum(smax, jnp.max(grp(jnp.where(causal, acc, -jnp.inf)), axis=0))
            smin = jnp.minimum(smin, jnp.min(grp(jnp.where(causal, acc, jnp.inf)), axis=0))
            return smax, smin

        smax, smin = lax.fori_loop(
            0, n_kt, tile_body,
            (jnp.full((SUBLANES, qb_rows), -jnp.inf, F32), jnp.full((SUBLANES, qb_rows), jnp.inf, F32)))
        fmax = jnp.max(smax, axis=0, keepdims=True)
        fmin = jnp.min(smin, axis=0, keepdims=True)
        hi0 = jnp.where(fmax == 0.0, 0, _float_key(fmax))
        lo0 = jnp.where(fmin == 0.0, -1, _float_key(fmin))
        n_causal = q_lo + 1 + lax.broadcasted_iota(I32, (1, qb_rows), 1)
        lo0 = jnp.where(n_causal < topk, KEY_NEG_INF + 1, lo0)
        hi0 = jnp.where(n_causal < topk, KEY_NEG_INF + 1, hi0)

        def count_ge(mid):
            def cbody(kt, c):
                for r0 in range(0, PKT, CNT_ROWS):
                    ge = jnp.where(keys_ref[kt, r0:r0 + CNT_ROWS, :] >= mid, 1.0, 0.0)
                    c = c + jnp.sum(ge.reshape(CNT_ROWS // SUBLANES, SUBLANES, qb_rows), axis=0)
                return c

            c = lax.fori_loop(0, n_kt, cbody, jnp.zeros((SUBLANES, qb_rows), F32))
            return jnp.sum(c, axis=0, keepdims=True)

        tau = _kth_largest_key(count_ge, lo0, hi0, topk)
        tau_ref[...] = tau
        over_f = jnp.where(count_ge(tau) > topk, 1.0, 0.0)

        @pl.when(jnp.max(over_f) > 0.0)
        def _break_ties():
            need = topk - count_ge(tau + 1)
            k_row = lax.broadcasted_iota(I32, (PKT, qb_rows), 0)
            top_code = TIE_CODE + (q_lo + qb_rows - 1)

            def tag_body(kt, _):
                k = keys_ref[kt]
                keys_ref[kt] = jnp.where(k == tau, (top_code - kt * PKT) - k_row, k)
                return 0

            lax.fori_loop(0, n_kt, tag_body, 0)
            row = jnp.zeros((1, qb_rows), I32)
            cut = _kth_largest_key(count_ge, row + TIE_CODE, row + top_code, need)
            drop_to = jnp.where(over_f > 0.5, KEY_NEG_INF, tau)

            def untag_body(kt, _):
                k = keys_ref[kt]
                keys_ref[kt] = jnp.where(k >= TIE_CODE, jnp.where(k >= cut, tau, drop_to), k)
                return 0

            lax.fori_loop(0, n_kt, untag_body, 0)

        m_ref[...] = jnp.full(m_ref.shape, NEG_BIG, F32)
        l_ref[...] = jnp.zeros(l_ref.shape, F32)
        acc_ref[...] = jnp.zeros(acc_ref.shape, F32)

    kb = j - 1
    last_kb = (q_lo + qb_rows - 1) // PKB

    n_sub = PKB // PSUB
    hsl = [slice(h * HEAD_DIM, (h + 1) * HEAD_DIM) for h in range(N_HEADS)]

    def logits_stage(u, b):
        for h in range(N_HEADS):
            s_ref[b, h] = _dot_nt(kb_ref[u * PSUB:(u + 1) * PSUB, hsl[h]], qs_ref[:, hsl[h]])

    def softmax_stage(u, b, near):
        s0 = kb * PKB + u * PSUB
        kt = s0 // PKT
        c0 = (u * PSUB) % PKT
        mask_bias = jnp.where(keys_ref[kt, c0:c0 + PSUB, :] >= tau_ref[...], 0.0, NEG_BIG)
        if near:
            which = jnp.clip((q_lo - s0) // PSUB, 0, 1)
        for h in range(N_HEADS):
            s = s_ref[b, h] + mask_bias
            if near:
                s = s + band_ref[h, 1 - which]
            m_old = m_ref[h]
            m_new = jnp.maximum(m_old, jnp.max(s, axis=0, keepdims=True))
            p = jnp.exp2(s - m_new)
            alpha = jnp.exp2(m_old - m_new)
            l_ref[h] = alpha * l_ref[h] + jnp.sum(p, axis=0, keepdims=True)
            m_ref[h] = m_new
            a_ref[b, h] = alpha
            p_ref[b, h] = p.astype(BF16)

    def values_stage(u, b):
        for h in range(N_HEADS):
            pv = _dot(vt_ref[hsl[h], u * PSUB:(u + 1) * PSUB], p_ref[b, h])
            acc_ref[h] = acc_ref[h] * a_ref[b, h] + pv

    attend = jnp.logical_and(j >= 1, kb <= last_kb)
    block_far = kb * PKB + PKB <= q_lo - PSUB

    @pl.when(jnp.logical_and(attend, block_far))
    def _attend_far_block():
        logits_stage(0, 0)
        for u in range(n_sub):
            if u + 1 < n_sub:
                logits_stage(u + 1, (u + 1) % 2)
            softmax_stage(u, u % 2, False)
            values_stage(u, u % 2)

    @pl.when(jnp.logical_and(attend, jnp.logical_not(block_far)))
    def _attend_near_block():
        for u in range(n_sub):
            s0 = kb * PKB + u * PSUB
            is_near = s0 + PSUB > q_lo - PSUB
            in_range = s0 < q_lo + qb_rows

            def sub_tile(near, u=u):
                logits_stage(u, 0)
                softmax_stage(u, 0, near)
                values_stage(u, 0)

            @pl.when(jnp.logical_and(in_range, is_near))
            def _():
                sub_tile(True)

            @pl.when(jnp.logical_and(in_range, jnp.logical_not(is_near)))
            def _():
                sub_tile(False)

    @pl.when(j == last_kb + 1)
    def _finish():
        for h in range(N_HEADS):
            hs = slice(h * HEAD_DIM, (h + 1) * HEAD_DIM)
            o_ref[:, hs] = (acc_ref[h] / l_ref[h]).T


def _prompt_attention(qs, qib, wi, kib, kb, vt, rel_bias):
    s, w = qs.shape
    topk = min(TOPK_MAX, s // 4)
    qb_rows = min(PQB, s)
    assert s % qb_rows == 0 and s % PKB == 0 and qb_rows == PSUB
    nqb = s // qb_rows
    nkb = s // PKB
    band = _bias_band(rel_bias, 2 * PSUB, qb_rows, -1, PSUB, LOG2E).reshape(N_HEADS, 2, PSUB, qb_rows)

    def last_kb(i):
        return (i * qb_rows + qb_rows - 1) // PKB

    qrow = lambda n: pl.BlockSpec((qb_rows, n), lambda i, j: (i, 0))
    return pl.pallas_call(
        functools.partial(_prompt_att_kernel, topk=topk, qb_rows=qb_rows),
        grid=(nqb, nkb + 1),
        in_specs=[qrow(w), qrow(w), pl.BlockSpec((N_IDX_HEADS, qb_rows), lambda i, j: (0, i)),
                  pl.BlockSpec((s, IDX_DIM), lambda i, j: (0, 0)),
                  pl.BlockSpec((PKB, w), lambda i, j: (jnp.minimum(jnp.maximum(j - 1, 0), last_kb(i)), 0)),
                  pl.BlockSpec((w, PKB), lambda i, j: (0, jnp.minimum(jnp.maximum(j - 1, 0), last_kb(i)))),
                  pl.BlockSpec(band.shape, lambda i, j: (0, 0, 0, 0))],
        out_specs=qrow(w),
        out_shape=jax.ShapeDtypeStruct((s, w), F32),
        scratch_shapes=[pltpu.VMEM((s // PKT, PKT, qb_rows), I32),
                        pltpu.VMEM((1, qb_rows), I32),
                        pltpu.VMEM((N_HEADS, 1, qb_rows), F32),
                        pltpu.VMEM((N_HEADS, 1, qb_rows), F32),
                        pltpu.VMEM((N_HEADS, HEAD_DIM, qb_rows), F32),
                        pltpu.VMEM((2, N_HEADS, PSUB, qb_rows), F32),
                        pltpu.VMEM((2, N_HEADS, PSUB, qb_rows), BF16),
                        pltpu.VMEM((2, N_HEADS, 1, qb_rows), F32)],
        compiler_params=_cparams(("arbitrary", "arbitrary")),
    )(qs, qib, wi.T, kib, kb, vt, band)


RC = 16
RNB = 16
RBLK = RC * RNB


def _dot_exact_lhs(m_bf, x):
    hi, mid, lo = _split3(x)
    return _dot(m_bf, hi) + (_dot(m_bf, mid) + _dot(m_bf, lo))


def _dot_exact_rhs(x, m_bf):
    hi, mid, lo = _split3(x)
    return _dot(hi, m_bf) + (_dot(mid, m_bf) + _dot(lo, m_bf))


def _rwkv_kernel(*refs, chain, t_valid):
    if chain:
        p_ref, = refs[:1]
        rest = refs[1:]
    else:
        p_ref, pprev_ref, s0_ref = refs[:3]
        rest = refs[3:]
    (mu_ref, w0_ref, wd2_ref, a0_ref, wa2_ref, wg2_ref, kk_ref, ka_ref, rk_ref, lng_ref, lnb_ref,
     lt_ref, bo_ref, bd_ref, out_ref, sfin_ref,
     carry_ref, s_ref, wa_s, rq_s, uv_s, yv_s, bt_s, kt_s, v_s, gc_s, y_s,
     np_s, ti_s, ak_s, rb_s, rk_s) = rest
    i = pl.program_id(0)
    w = GROUP_WIDTH
    p = p_ref[...]
    row = lax.broadcasted_iota(I32, (RBLK, 1), 0)
    if chain:
        @pl.when(i == 0)
        def _():
            carry_ref[...] = jnp.zeros(carry_ref.shape, F32)
            s_ref[...] = jnp.zeros(s_ref.shape, F32)

        pprev = jnp.where(row == 0, carry_ref[...], pltpu.roll(p, 1, axis=0))
        carry_ref[...] = p[RBLK - 1:RBLK, :]
    else:
        pprev = pprev_ref[...]
    ps = p + (pprev - p) * mu_ref[...]
    r, k, v = ps[:, :w], ps[:, w:2 * w], ps[:, 2 * w:3 * w]
    o = 3 * w
    xw = ps[:, o:o + DECAY_LORA]
    xa = ps[:, o + DECAY_LORA:o + DECAY_LORA + AAA_LORA]
    xg = ps[:, o + DECAY_LORA + AAA_LORA:]
    dec = w0_ref[...] + _dot(jnp.tanh(xw).astype(BF16), wd2_ref[...])
    softplus = jnp.maximum(-dec, 0.0) + jnp.log(1.0 + jnp.exp(-jnp.abs(dec)))
    lw = -jnp.exp(-softplus - 0.5)
    a = jax.nn.sigmoid(a0_ref[...] + _dot(xa.astype(BF16), wa2_ref[...]))
    g = _dot(jax.nn.sigmoid(xg).astype(BF16), wg2_ref[...])
    kk = k * kk_ref[...]
    kk = kk * lax.rsqrt(jnp.maximum(_dot_exact_rhs(kk * kk, bd_ref[...]), 1e-24))
    k2 = k * (1.0 + (a - 1.0) * ka_ref[...])
    alpha = -kk
    beta = kk * a
    if t_valid < RC:
        valid = (row % RC) < t_valid
        zero = lambda x: jnp.where(valid, x, 0.0)
        lw, alpha, beta, k2, r, v = zero(lw), zero(alpha), zero(beta), zero(k2), zero(r), zero(v)
    cl = _dot_exact_lhs(lt_ref[...], lw)
    ct = _dot_exact_lhs(bo_ref[...], lw)
    g_in = jnp.exp(cl)
    g_ex = jnp.exp(cl - lw)
    g_inv = jnp.exp(-cl)
    g_end = jnp.exp(ct - cl)
    g_all = jnp.exp(ct)
    at, rt = alpha * g_ex, r * g_in
    bh, kh = beta * g_inv, k2 * g_inv
    bt, kt = beta * g_end, k2 * g_end

    ri = lax.broadcasted_iota(I32, (RBLK, RBLK), 0)
    ci = lax.broadcasted_iota(I32, (RBLK, RBLK), 1)
    same = (ri // RC) == (ci // RC)
    strict = jnp.logical_and(same, ci < ri)
    incl = jnp.logical_and(same, ci <= ri)
    eye = (ri == ci).astype(F32)

    heads = range(N_HEADS)
    hsl = [slice(h * HEAD_DIM, (h + 1) * HEAD_DIM) for h in heads]
    for h in heads:
        hs = hsl[h]
        gm = _dot_nt(jnp.concatenate([at[:, hs], rt[:, hs]], axis=0).astype(BF16),
                     jnp.concatenate([bh[:, hs], kh[:, hs]], axis=0).astype(BF16))
        n1 = jnp.where(strict, gm[:RBLK, :RBLK], 0.0)
        np_s[h] = n1
        ti_s[h] = eye + n1
        ak_s[h] = jnp.where(strict, gm[:RBLK, RBLK:], 0.0)
        rb_s[h] = jnp.where(incl, gm[RBLK:, :RBLK], 0.0)
        rk_s[h] = jnp.where(incl, gm[RBLK:, RBLK:], 0.0)
        bt_s[h] = bt[:, hs]
        kt_s[h] = kt[:, hs]
        v_s[h] = v[:, hs]
        gc_s[h] = g_all[:, hs]
    for _ in range(int(math.log2(RC)) - 1):
        for h in heads:
            npow = np_s[h].astype(BF16)
            np_s[h] = _dot(npow, npow)
        for h in heads:
            tinv = ti_s[h]
            ti_s[h] = tinv + _dot(np_s[h].astype(BF16), tinv.astype(BF16))
    for h in heads:
        uv_s[h] = _dot(ak_s[h].astype(BF16), v_s[h].astype(BF16))
    for h in heads:
        wcat = _dot(ti_s[h].astype(BF16), jnp.concatenate([at[:, hsl[h]], uv_s[h]], axis=1).astype(BF16))
        wa_s[h] = wcat[:, :HEAD_DIM]
        uv_s[h] = wcat[:, HEAD_DIM:]
    for h in heads:
        ry =_dot(rb_s[h].astype(BF16), jnp.concatenate([wa_s[h], uv_s[h]], axis=1).astype(BF16))
        rq_s[h] = rt[:, hsl[h]] + ry[:, :HEAD_DIM]
        yv_s[h] = ry[:, HEAD_DIM:] + _dot(rk_s[h].astype(BF16), v_s[h].astype(BF16))

    def chunk_body(c, _):
        c0 = pl.multiple_of(c * RC, RC)
        sl = pl.ds(c0, RC)
        s_old = [s0_ref[c, h] if not chain else s_ref[h] for h in heads]
        res = [_dot_nt(jnp.concatenate([wa_s[h, sl, :], rq_s[h, sl, :]], axis=0).astype(BF16),
                       s_old[h].astype(BF16)) for h in heads]
        for h in heads:
            y_s[h, sl, :] = res[h][RC:] + yv_s[h, sl, :]
        upd = [lax.dot_general(jnp.concatenate([res[h][:RC] + uv_s[h, sl, :], v_s[h, sl, :]], axis=0).astype(BF16),
                               jnp.concatenate([bt_s[h, sl, :], kt_s[h, sl, :]], axis=0).astype(BF16),
                               (((0,), (0,)), ((), ())), preferred_element_type=F32) for h in heads]
        for h in heads:
            s_new = s_old[h] * gc_s[h, pl.ds(c0, 1), :] + upd[h]
            if chain:
                s_ref[h] = s_new
            else:
                sfin_ref[c, h] = s_new
        return 0

    lax.fori_loop(0, RNB, chunk_body, 0)
    if chain:
        sfin_ref[...] = s_ref[...]

    for h in range(N_HEADS):
        hs = slice(h * HEAD_DIM, (h + 1) * HEAD_DIM)
        y = y_s[h]
        mean = jnp.mean(y, axis=-1, keepdims=True)
        var = jnp.mean(jnp.square(y - mean), axis=-1, keepdims=True)
        yn = (y - mean) * lax.rsqrt(var + GN_EPS) * lng_ref[:, hs] + lnb_ref[:, hs]
        bonus = jnp.sum(r[:, hs] * k2[:, hs] * rk_ref[:, hs], axis=-1, keepdims=True) * v[:, hs]
        out_ref[:, hs] = (yn + bonus) * g[:, hs]


def _rwkv_consts():
    idx = np.arange(RBLK)
    same = (idx[:, None] // RC) == (idx[None, :] // RC)
    lt = (same & (idx[None, :] <= idx[:, None])).astype(np.float32)
    bo = same.astype(np.float32)
    lane = np.arange(GROUP_WIDTH)
    bd = ((lane[:, None] // HEAD_DIM) == (lane[None, :] // HEAD_DIM)).astype(np.float32)
    return jnp.asarray(lt, BF16), jnp.asarray(bo, BF16), jnp.asarray(bd, BF16)


def _rwkv(p, lw, chain, pprev=None, s0=None, t_valid=RC):
    rows = p.shape[0]
    nblk = rows // RBLK
    w = GROUP_WIDTH
    vec = lambda a: a.reshape(1, -1)
    consts = [vec(lw['mu_shift']), vec(lw['w0']), lw['w_decay2'].astype(BF16), vec(lw['a0']),
              lw['w_a2'].astype(BF16), lw['w_g2'].astype(BF16), vec(lw['k_k']), vec(lw['k_a']), vec(lw['r_k']),
              vec(lw['lnx_g']), vec(lw['lnx_b'])] + list(_rwkv_consts())
    full = lambda a: pl.BlockSpec(a.shape, lambda i: (0,) * a.ndim)
    blk = pl.BlockSpec((RBLK, RWKV_COLS), lambda i: (i, 0))
    hshape = (N_HEADS, HEAD_DIM, HEAD_DIM)
    if chain:
        ins, in_specs = [p], [blk]
        sfin_spec = pl.BlockSpec(hshape, lambda i: (0, 0, 0))
        sfin_shape = jax.ShapeDtypeStruct(hshape, F32)
    else:
        sspec = pl.BlockSpec((RNB,) + hshape, lambda i: (i, 0, 0, 0))
        ins, in_specs = [p, pprev, s0], [blk, blk, sspec]
        sfin_spec = sspec
        sfin_shape = jax.ShapeDtypeStruct((nblk * RNB,) + hshape, F32)
    hm = lambda: pltpu.VMEM((N_HEADS, RBLK, HEAD_DIM), F32)
    return pl.pallas_call(
        functools.partial(_rwkv_kernel, chain=chain, t_valid=t_valid),
        grid=(nblk,),
        in_specs=in_specs + [full(c) for c in consts],
        out_specs=[pl.BlockSpec((RBLK, w), lambda i: (i, 0)), sfin_spec],
        out_shape=[jax.ShapeDtypeStruct((rows, w), F32), sfin_shape],
        scratch_shapes=[pltpu.VMEM((1, RWKV_COLS), F32), pltpu.VMEM(hshape, F32)] + [hm() for _ in range(9)]
        + [pltpu.VMEM((N_HEADS, RBLK, RBLK), F32) for _ in range(5)],
        compiler_params=_cparams(("arbitrary",)),
    )(*ins, *consts)


RT_TM = 256


def _route_kernel(x_ref, ro_ref, ao_ref, wor_ref, woa_ref, g1_ref, sh_ref, sc_ref, n2_ref, wr_ref, br_ref,
                  ltri_ref, x1_ref, h2_ref, idx_ref, gate_ref, rank_ref, cnt_ref, run_ref):
    @pl.when(pl.program_id(0) == 0)
    def _():
        run_ref[...] = jnp.zeros(run_ref.shape, F32)

    tm = x_ref.shape[0]
    mix = _dot(ro_ref[...].astype(BF16), wor_ref[...]) + _dot(ao_ref[...].astype(BF16), woa_ref[...])
    x1 = x_ref[...] + g1_ref[...] * mix
    x1_ref[...] = x1
    y = x1 * lax.rsqrt(jnp.mean(x1 * x1, axis=-1, keepdims=True) + RMS_EPS) * n2_ref[...]
    h2 = (y * (1.0 + sc_ref[...]) + sh_ref[...]).astype(BF16)
    h2_ref[...] = h2
    logits = _dot(h2, wr_ref[...]) + br_ref[...]
    lane = lax.broadcasted_iota(I32, (tm, LANES), 1)
    lane_f = lane.astype(F32)
    lg = logits
    vals, idxs = [], []
    for _ in range(TOP_K):
        m = jnp.max(lg, axis=1, keepdims=True)
        idx = jnp.min(jnp.where(lg == m, lane_f, float(LANES)), axis=1, keepdims=True)
        vals.append(m)
        idxs.append(idx)
        lg = jnp.where(lane_f == idx, -3e38, lg)
    es = [jnp.exp(v - vals[0]) for v in vals]
    den = es[0] + es[1] + es[2] + es[3]
    ohs = [(lane_f == idx).astype(F32) for idx in idxs]
    oh_all = ohs[0] + ohs[1] + ohs[2] + ohs[3]
    base = run_ref[...] + _dot(ltri_ref[...], oh_all.astype(BF16))
    idx_out = jnp.zeros((tm, LANES), F32)
    gate_out = jnp.zeros((tm, LANES), F32)
    rank_out = jnp.zeros((tm, LANES), F32)
    for kk in range(TOP_K):
        rank = jnp.sum(ohs[kk] * base, axis=1, keepdims=True)
        idx_out = jnp.where(lane == kk, idxs[kk], idx_out)
        gate_out = jnp.where(lane == kk, es[kk] / den, gate_out)
        rank_out = jnp.where(lane == kk, rank, rank_out)
    idx_ref[...] = idx_out[:, :TOP_K].astype(I32)
    gate_ref[...] = gate_out[:, :TOP_K]
    rank_ref[...] = rank_out[:, :TOP_K].astype(I32)
    run_ref[...] = run_ref[...] + jnp.sum(oh_all, axis=0, keepdims=True)
    cnt_ref[...] = run_ref[...]


def _out_proj_route(x, ro, ao, w_out, mod, norm2_g, w_router, b_router):
    t, d = x.shape
    tm = min(RT_TM, t)
    w = GROUP_WIDTH
    wor = w_out[:w].astype(BF16)
    woa = w_out[w:].astype(BF16)
    wr = jnp.pad(w_router, ((0, 0), (0, LANES - N_EXPERTS))).astype(BF16)
    br = jnp.pad(b_router.reshape(1, -1), ((0, 0), (0, LANES - N_EXPERTS)), constant_values=NEG_BIG)
    ltri = jnp.asarray(np.tril(np.ones((tm, tm), np.float32), -1), BF16)
    row = lambda n: pl.BlockSpec((tm, n), lambda i: (i, 0))
    full = lambda a: pl.BlockSpec(a.shape, lambda i: (0, 0))
    return pl.pallas_call(
        _route_kernel,
        grid=(t // tm,),
        in_specs=[row(d), row(w), row(w), full(wor), full(woa),
                  _mod_spec(mod, 2, tm, d), _mod_spec(mod, 3, tm, d), _mod_spec(mod, 4, tm, d),
                  pl.BlockSpec((1, d), lambda i: (0, 0)), full(wr), full(br), full(ltri)],
        out_specs=[row(d), row(d), row(TOP_K), row(TOP_K), row(TOP_K), pl.BlockSpec((1, LANES), lambda i: (0, 0))],
        out_shape=[jax.ShapeDtypeStruct((t, d), F32), jax.ShapeDtypeStruct((t, d), BF16),
                   jax.ShapeDtypeStruct((t, TOP_K), I32), jax.ShapeDtypeStruct((t, TOP_K), F32),
                   jax.ShapeDtypeStruct((t, TOP_K), I32), jax.ShapeDtypeStruct((1, LANES), F32)],
        scratch_shapes=[pltpu.VMEM((1, LANES), F32)],
        compiler_params=_cparams(("arbitrary",)),
    )(x, ro, ao, wor, woa, mod, mod, mod, norm2_g.reshape(1, d), wr, br, ltri)


EX_TM = 256
BF16_ROW = (SUBLANES, LANES)


def _dispatch_kernel(slot_ref, h_ref, init_ref, out_ref, sem):
    del init_ref
    tm = h_ref.shape[0]

    def body(r, _):
        for kk in range(TOP_K):
            pltpu.make_async_copy(h_ref.at[r], out_ref.at[slot_ref[r * TOP_K + kk]], sem).start()
        return 0

    lax.fori_loop(0, tm, body, 0)
    for _ in range(TOP_K):
        pltpu.make_async_copy(h_ref, out_ref.at[pl.ds(0, tm)], sem).wait()


def _dispatch(h2, slot, rows_sorted):
    t, d = h2.shape
    tm = min(RT_TM, t)
    assert d == SUBLANES * LANES
    h3 = h2.reshape((t,) + BF16_ROW)
    return pl.pallas_call(
        _dispatch_kernel,
        grid=(t // tm,),
        in_specs=[pl.BlockSpec((tm * TOP_K,), lambda i: (i,), memory_space=pltpu.SMEM),
                  pl.BlockSpec((tm,) + BF16_ROW, lambda i: (i, 0, 0)),
                  pl.BlockSpec(memory_space=pl.ANY)],
        out_specs=pl.BlockSpec(memory_space=pl.ANY),
        out_shape=jax.ShapeDtypeStruct(rows_sorted.shape, rows_sorted.dtype),
        scratch_shapes=[pltpu.SemaphoreType.DMA(())],
        input_output_aliases={2: 0},
        compiler_params=_cparams(("arbitrary",)),
    )(slot.reshape(-1), h3, rows_sorted)


def _expert_kernel(be_ref, nv_ref, x_ref, wgu_ref, bgu_ref, wd_ref, bd_ref, o_ref, wgu_bf, wd_bf):
    i = pl.program_id(0)
    changed = jnp.logical_or(i == 0, be_ref[i] != be_ref[jnp.maximum(i - 1, 0)])

    @pl.when(changed)
    def _():
        wgu_bf[...] = wgu_ref[0].astype(BF16)
        wd_bf[...] = wd_ref[0].astype(BF16)

    @pl.when(i < nv_ref[0])
    def _():
        f = wd_bf.shape[0]
        gu = _dot(x_ref[...], wgu_bf[...]) + bgu_ref[0]
        glu = jnp.minimum(gu[:, :f], SWIGLU_LIMIT)
        lin = jnp.clip(gu[:, f:], -SWIGLU_LIMIT, SWIGLU_LIMIT)
        act = glu * jax.nn.sigmoid(SWIGLU_ALPHA * glu) * (lin + 1.0)
        o_ref[...] = _dot(act.astype(BF16), wd_bf[...]) + bd_ref[0]

    @pl.when(i >= nv_ref[0])
    def _():
        o_ref[...] = jnp.zeros(o_ref.shape, F32)


def _experts(rows_sorted, blk_e, n_valid, w_gu, b_gu, w_down, b_down):
    ns, d = rows_sorted.shape
    e, _, f2 = w_gu.shape
    f = f2 // 2
    grid_spec = pltpu.PrefetchScalarGridSpec(
        num_scalar_prefetch=2,
        grid=(ns // EX_TM,),
        in_specs=[pl.BlockSpec((EX_TM, d), lambda i, be, nv: (i, 0)),
                  pl.BlockSpec((1, d, f2), lambda i, be, nv: (be[i], 0, 0)),
                  pl.BlockSpec((1, 1, f2), lambda i, be, nv: (be[i], 0, 0)),
                  pl.BlockSpec((1, f, d), lambda i, be, nv: (be[i], 0, 0)),
                  pl.BlockSpec((1, 1, d), lambda i, be, nv: (be[i], 0, 0))],
        out_specs=pl.BlockSpec((EX_TM, d), lambda i, be, nv: (i, 0)),
        scratch_shapes=[pltpu.VMEM((d, f2), BF16), pltpu.VMEM((f, d), BF16)])
    return pl.pallas_call(
        _expert_kernel,
        grid_spec=grid_spec,
        out_shape=jax.ShapeDtypeStruct((ns, d), F32),
        compiler_params=_cparams(("arbitrary",)),
    )(blk_e, n_valid, rows_sorted, w_gu, b_gu.reshape(e, 1, f2), w_down, b_down.reshape(e, 1, d))


def _combine_kernel(slot_ref, rows_ref, gate_ref, x1_ref, g2_ref, nf_ref, y_ref, buf, sem):
    tm = x1_ref.shape[0]

    def body(r, _):
        for kk in range(TOP_K):
            pltpu.make_async_copy(rows_ref.at[pl.ds(slot_ref[r * TOP_K + kk], 1)],
                                  buf.at[kk, pl.ds(r, 1)], sem).start()
        return 0

    lax.fori_loop(0, tm, body, 0)
    for kk in range(TOP_K):
        pltpu.make_async_copy(rows_ref.at[pl.ds(0, tm)], buf.at[kk], sem).wait()
    gates = gate_ref[...]
    moe = gates[:, 0:1] * buf[0]
    for kk in range(1, TOP_K):
        moe = moe + gates[:, kk:kk + 1] * buf[kk]
    x2 = x1_ref[...] + g2_ref[...] * moe
    y_ref[...] = x2 * lax.rsqrt(jnp.mean(x2 * x2, axis=-1, keepdims=True) + RMS_EPS) * nf_ref[...]


def _combine(out_rows, slot, gates, x1, mod, normf_g):
    t, d = x1.shape
    tm = min(RT_TM, t)
    return pl.pallas_call(
        _combine_kernel,
        grid=(t // tm,),
        in_specs=[pl.BlockSpec((tm * TOP_K,), lambda i: (i,), memory_space=pltpu.SMEM),
                  pl.BlockSpec(memory_space=pl.ANY),
                  pl.BlockSpec((tm, TOP_K), lambda i: (i, 0)),
                  pl.BlockSpec((tm, d), lambda i: (i, 0)), _mod_spec(mod, 5, tm, d),
                  pl.BlockSpec((1, d), lambda i: (0, 0))],
        out_specs=pl.BlockSpec((tm, d), lambda i: (i, 0)),
        out_shape=jax.ShapeDtypeStruct((t, d), F32),
        scratch_shapes=[pltpu.VMEM((TOP_K, tm, d), F32), pltpu.SemaphoreType.DMA(())],
        compiler_params=_cparams(("arbitrary",)),
    )(slot.reshape(-1), out_rows, gates, x1, mod, normf_g.reshape(1, d))


def _moe_plan(idx_p, rank_p, cnt_p, idx_s, rank_s, cnt_s):
    n_assign = idx_p.size + idx_s.size
    n_tiles = -(-n_assign // EX_TM) + N_EXPERTS
    cp = cnt_p[0, :N_EXPERTS].astype(I32)
    cs = cnt_s[0, :N_EXPERTS].astype(I32)
    padded = (cp + cs + EX_TM - 1) // EX_TM * EX_TM
    pad_end = jnp.cumsum(padded)
    pad_start = pad_end - padded
    slot_p = pad_start[idx_p] + rank_p
    slot_s = pad_start[idx_s] + cp[idx_s] + rank_s
    tile_row = jnp.arange(n_tiles, dtype=I32)[:, None] * EX_TM
    blk_e = jnp.minimum(jnp.sum((pad_end[None, :] <= tile_row).astype(I32), axis=1), N_EXPERTS - 1)
    n_valid = (pad_end[-1:] // EX_TM).astype(I32)
    return slot_p, slot_s, blk_e, n_valid, n_tiles * EX_TM


PG = 8
QROWS = N_HEADS * SUBLANES


def _page_specs(block, n_pages):
    def spec(u):
        return pl.BlockSpec(block, lambda b, j, pt: (pt[b * n_pages + j * PG + u],) + (0,) * (len(block) - 1))
    return [spec(u) for u in range(PG)]


def _head_sum(x):
    out = x[:SUBLANES]
    for h in range(1, N_HEADS):
        out = out + x[h * SUBLANES:(h + 1) * SUBLANES]
    return out


def _sample_index_kernel(pt_ref, qi_ref, wrep_ref, kin_ref, *rest, n_pages, t_new, topk):
    del pt_ref
    pages = rest[:PG]
    keys_ref, tau_ref = rest[PG:]
    j = pl.program_id(1)
    qi = qi_ref[0]
    wrep = wrep_ref[0]

    def scores(ki_t_bf):
        s = _dot(qi, ki_t_bf)
        return _head_sum(jnp.maximum(s, 0.0) * wrep)

    for u in range(PG):
        keys_ref[0, j * PG + u] = _float_key(scores(pages[u][0].astype(BF16)))

    @pl.when(j == pl.num_programs(1) - 1)
    def _():
        qrow = lax.broadcasted_iota(I32, (SUBLANES, PAGE_SIZE), 0)
        col = lax.broadcasted_iota(I32, (SUBLANES, PAGE_SIZE), 1)
        ok = jnp.logical_and(col <= qrow, col < t_new)
        keys_ref[0, n_pages] = jnp.where(ok, _float_key(scores(kin_ref[0])), KEY_NEG_INF)

        def count_ge(mid):
            c = jnp.sum(jnp.where(keys_ref[0] >= mid[None], 1.0, 0.0), axis=0)
            return jnp.broadcast_to(jnp.sum(c, axis=1, keepdims=True), (SUBLANES, LANES))

        tau = _kth_largest_key(count_ge, jnp.full((SUBLANES, LANES), KEY_NEG_INF + 1, I32),
                               jnp.full((SUBLANES, LANES), 0x7F800000, I32), topk)
        tau_ref[0] = tau
        over_f = jnp.where(count_ge(tau) > topk, 1.0, 0.0)

        @pl.when(jnp.max(over_f) > 0.0)
        def _break_ties():
            need = topk - count_ge(tau + 1)
            shape = keys_ref.shape[1:]
            pos = lax.broadcasted_iota(I32, shape, 0) * PAGE_SIZE + lax.broadcasted_iota(I32, shape, 2)
            tied = keys_ref[0] == tau[None]

            def pbody(_, carry):
                lo, hi = carry
                mid = (lo + hi) >> 1
                c = jnp.sum(jnp.where(tied, jnp.where(pos <= mid[None], 1.0, 0.0), 0.0), axis=0)
                ok = jnp.broadcast_to(jnp.sum(c, axis=1, keepdims=True), (SUBLANES, LANES)) >= need
                return jnp.where(ok, lo, mid + 1), jnp.where(ok, mid, hi)

            n_pos = (n_pages + 1) * PAGE_SIZE
            last_pos, _ = lax.fori_loop(0, (n_pos - 1).bit_length(), pbody,
                                        (jnp.zeros((SUBLANES, LANES), I32), jnp.full((SUBLANES, LANES), n_pos - 1, I32)))
            gone = jnp.where(tied, jnp.where(pos > last_pos[None], over_f[None], 0.0), 0.0)
            keys_ref[0] = jnp.where(gone > 0.5, KEY_NEG_INF, keys_ref[0])


def _sample_attend_kernel(pt_ref, q_ref, keys_ref, tau_ref, band_ref, kn_ref, vn_ref, *rest, n_pages):
    del pt_ref
    kpages = rest[:PG]
    vpages = rest[PG:2 * PG]
    o_ref, m_ref, l_ref, acc_ref = rest[2 * PG:]
    j = pl.program_id(1)
    last = j == pl.num_programs(1) - 1

    @pl.when(j == 0)
    def _():
        m_ref[...] = jnp.full(m_ref.shape, NEG_BIG, F32)
        l_ref[...] = jnp.zeros(l_ref.shape, F32)
        acc_ref[...] = jnp.zeros(acc_ref.shape, F32)

    tau = tau_ref[0]
    heads = range(N_HEADS)

    def attend(k_of, v_of, key_tiles, bias_of):
        pages = range(len(key_tiles))
        sel = [kt >= tau for kt in key_tiles]
        s = [[_dot(q_ref[0, h], k_of(u, h)) for u in pages] for h in heads]
        ps, alphas = [], []
        for h in heads:
            sh = []
            for u in pages:
                b_uh = bias_of(u, h)
                sh.append(jnp.where(sel[u], s[h][u] if b_uh is None else s[h][u] + b_uh, NEG_BIG))
            tile_max = functools.reduce(jnp.maximum, sh)
            m_old = m_ref[h]
            m_new = jnp.maximum(m_old, jnp.broadcast_to(jnp.max(tile_max, axis=1, keepdims=True), m_old.shape))
            p = [jnp.exp(x - m_new) for x in sh]
            alpha = jnp.exp(m_old - m_new)
            p_sum = functools.reduce(jnp.add, p)
            l_ref[h] = alpha * l_ref[h] + jnp.broadcast_to(jnp.sum(p_sum, axis=1, keepdims=True), m_old.shape)
            m_ref[h] = m_new
            ps.append([x.astype(BF16) for x in p])
            alphas.append(alpha[:, :HEAD_DIM])
        pv = [[_dot_nt(ps[h][u], v_of(u, h)) for u in pages] for h in heads]
        for h in heads:
            acc_ref[h] = acc_ref[h] * alphas[h] + functools.reduce(jnp.add, pv[h])

    attend(lambda u, h: kpages[u][0, h].astype(BF16), lambda u, h: vpages[u][0, h].astype(BF16),
           [keys_ref[0, j * PG + u] for u in range(PG)],
           lambda u, h: jnp.where(last, band_ref[h, :, :PAGE_SIZE], 0.0) if u == PG - 1 else None)

    @pl.when(last)
    def _():
        hsl = lambda h: slice(h * HEAD_DIM, (h + 1) * HEAD_DIM)
        attend(lambda u, h: kn_ref[0, hsl(h), :], lambda u, h: vn_ref[0, hsl(h), :], [keys_ref[0, n_pages]],
               lambda u, h: band_ref[h, :, PAGE_SIZE:])
        for h in heads:
            o_ref[0, :, hsl(h)] = acc_ref[h] / l_ref[h][:, :HEAD_DIM]


def _sample_attention(qs, qib, wi, k_new_bf, v_new_bf, ki_new_bf, cache_k, cache_v, cache_kidx, page_table,
                      rel_bias, b, t_new):
    w = GROUP_WIDTH
    n_pages = page_table.shape[1]
    assert n_pages % PG == 0 and t_new <= SUBLANES and cache_k.shape[1] == PAGE_SIZE
    past = n_pages * PAGE_SIZE
    topk = min(TOPK_MAX, (past + t_new) // 4)
    pt = page_table.reshape(-1)
    padq = lambda a: jnp.pad(a, ((0, 0), (0, SUBLANES - t_new)) + ((0, 0),) * (a.ndim - 2))
    padk = lambda a: jnp.pad(a.reshape(b, t_new, -1), ((0, 0), (0, PAGE_SIZE - t_new), (0, 0))).transpose(0, 2, 1)
    qi_r = padq(qib.reshape(b, t_new, N_IDX_HEADS, IDX_DIM)).transpose(0, 2, 1, 3).reshape(b, QROWS, IDX_DIM)
    w_r = padq(wi.reshape(b, t_new, N_IDX_HEADS)).transpose(0, 2, 1).reshape(b, QROWS, 1)
    w_r = jnp.broadcast_to(w_r, (b, QROWS, LANES))
    q4 = padq(qs.reshape(b, t_new, N_HEADS, HEAD_DIM)).transpose(0, 2, 1, 3)
    kin, kn, vn = padk(ki_new_bf), padk(k_new_bf), padk(v_new_bf)
    band = _bias_band(rel_bias, SUBLANES, 2 * PAGE_SIZE, 1, PAGE_SIZE)
    ck_t = cache_k.transpose(0, 2, 3, 1)
    cv_t = cache_v.transpose(0, 2, 3, 1)
    cki_t = cache_kidx.transpose(0, 2, 1)
    page_block = (1, N_HEADS, HEAD_DIM, PAGE_SIZE)

    per_b = lambda shape: pl.BlockSpec((1,) + shape, lambda bb, j, p_: (bb,) + (0,) * len(shape))
    steps = n_pages // PG
    keys, tau = pl.pallas_call(
        functools.partial(_sample_index_kernel, n_pages=n_pages, t_new=t_new, topk=topk),
        grid_spec=pltpu.PrefetchScalarGridSpec(
            num_scalar_prefetch=1, grid=(b, steps),
            in_specs=[per_b((QROWS, IDX_DIM)), per_b((QROWS, LANES)), per_b((IDX_DIM, PAGE_SIZE))]
            + _page_specs((1, IDX_DIM, PAGE_SIZE), n_pages),
            out_specs=[per_b((n_pages + 1, SUBLANES, PAGE_SIZE)), per_b((SUBLANES, LANES))]),
        out_shape=[jax.ShapeDtypeStruct((b, n_pages + 1, SUBLANES, PAGE_SIZE), I32),
                   jax.ShapeDtypeStruct((b, SUBLANES, LANES), I32)],
        compiler_params=_cparams(("arbitrary", "arbitrary")),
    )(pt, qi_r, w_r, kin, *([cki_t] * PG))
    out = pl.pallas_call(
        functools.partial(_sample_attend_kernel, n_pages=n_pages),
        grid_spec=pltpu.PrefetchScalarGridSpec(
            num_scalar_prefetch=1, grid=(b, steps),
            in_specs=[per_b((N_HEADS, SUBLANES, HEAD_DIM)), per_b((n_pages + 1, SUBLANES, PAGE_SIZE)),
                      per_b((SUBLANES, LANES)), pl.BlockSpec(band.shape, lambda bb, j, p_: (0, 0, 0)),
                      per_b((w, PAGE_SIZE)), per_b((w, PAGE_SIZE))]
            + _page_specs(page_block, n_pages) + _page_specs(page_block, n_pages),
            out_specs=per_b((SUBLANES, w)),
            scratch_shapes=[pltpu.VMEM((N_HEADS, SUBLANES, LANES), F32), pltpu.VMEM((N_HEADS, SUBLANES, LANES), F32),
                            pltpu.VMEM((N_HEADS, SUBLANES, HEAD_DIM), F32)]),
        out_shape=jax.ShapeDtypeStruct((b, SUBLANES, w), F32),
        compiler_params=_cparams(("arbitrary", "arbitrary")),
    )(pt, q4, keys, tau, band, kn, vn, *([ck_t] * PG), *([cv_t] * PG))
    return out[:, :t_new].reshape(b * t_new, w)


def kernel(x_prompt, x_sample, c_prompt, c_sample, cache_k, cache_v, cache_kidx, page_table, state_wkv,
           state_shift, w_ada, b_ada, norm1_g, w_in, mu_shift, w0, w_decay2, a0, w_a2, w_g2, k_k, k_a, r_k,
           lnx_g, lnx_b, rel_bias, w_out, norm2_g, w_router, b_router, w_gu, b_gu, w_down, b_down, normf_g):
    depth = w_in.shape[0]
    assert depth == 1, "the merged prompt+sample expert pass is written for a single layer"
    bp, sp, d = x_prompt.shape
    bs, ts, _ = x_sample.shape
    assert bp == 1 and sp % RBLK == 0 and bs % RNB == 0 and ts <= RC
    l = 0
    lw = {'mu_shift': mu_shift[l], 'w0': w0[l], 'w_decay2': w_decay2[l], 'a0': a0[l], 'w_a2': w_a2[l],
          'w_g2': w_g2[l], 'k_k': k_k[l], 'k_a': k_a[l], 'r_k': r_k[l], 'lnx_g': lnx_g[l], 'lnx_b': lnx_b[l]}
    ns = bs * ts

    c_all = jnp.concatenate([c_prompt, c_sample], axis=0)
    n_c = c_all.shape[0]
    c_all = jnp.pad(c_all, ((0, -n_c % SUBLANES), (0, 0)))
    mod = _adaln(c_all, w_ada[l], b_ada[l])
    mod_p = mod[0:1]
    mod_s = jnp.repeat(mod[bp:bp + bs], ts, axis=0)

    xp = x_prompt.reshape(sp, d)
    xs = x_sample.reshape(ns, d)
    pr_p, k_p, v_p, ki_p, wi_p, qs_p, kb_p, _, qib_p, kib_p, vt_p = _in_proj(xp, norm1_g[l], mod_p, w_in[l],
                                                                             HEAD_DIM ** -0.5 * LOG2E)
    pr_s, k_s, v_s, ki_s, wi_s, qs_s, kb_s, vb_s, qib_s, kib_s, _ = _in_proj(xs, norm1_g[l], mod_s, w_in[l],
                                                                             HEAD_DIM ** -0.5)

    ro_p, wkv_p = _rwkv(pr_p, lw, chain=True)
    att_p = _prompt_attention(qs_p, qib_p, wi_p, kib_p, kb_p, vt_p, rel_bias)

    pr_s3 = pr_s.reshape(bs, ts, RWKV_COLS)
    prev_s3 = jnp.concatenate([state_shift[l][:, None, :], pr_s3[:, :-1]], axis=1)
    padc = lambda a: jnp.pad(a, ((0, 0), (0, RC - ts), (0, 0))).reshape(bs * RC, RWKV_COLS)
    ro_s, wkv_s = _rwkv(padc(pr_s3), lw, chain=False, pprev=padc(prev_s3), s0=state_wkv[l], t_valid=ts)
    ro_s = ro_s.reshape(bs, RC, GROUP_WIDTH)[:, :ts].reshape(ns, GROUP_WIDTH)
    att_s = _sample_attention(qs_s, qib_s, wi_s, kb_s, vb_s, kib_s, cache_k[l], cache_v[l], cache_kidx[l],
                              page_table, rel_bias, bs, ts)

    x1_p, h2_p, idx_p, gate_p, rank_p, cnt_p = _out_proj_route(xp, ro_p, att_p, w_out[l], mod_p, norm2_g[l],
                                                               w_router[l], b_router[l])
    x1_s, h2_s, idx_s, gate_s, rank_s, cnt_s = _out_proj_route(xs, ro_s, att_s, w_out[l], mod_s, norm2_g[l],
                                                               w_router[l], b_router[l])
    slot_p, slot_s, blk_e, n_valid, n_slots = _moe_plan(idx_p, rank_p, cnt_p, idx_s, rank_s, cnt_s)
    rows = jnp.zeros((n_slots,) + BF16_ROW, BF16)
    rows = _dispatch(h2_p, slot_p, rows)
    rows = _dispatch(h2_s, slot_s, rows)
    out_rows = _experts(rows.reshape(n_slots, d), blk_e, n_valid, w_gu[l], b_gu[l], w_down[l], b_down[l])
    y_p = _combine(out_rows, slot_p, gate_p, x1_p, mod_p, normf_g)
    y_s = _combine(out_rows, slot_s, gate_s, x1_s, mod_s, normf_g)

    hd = (N_HEADS, HEAD_DIM)
    return (y_p.reshape(bp, sp, d), y_s.reshape(bs, ts, d),
            k_p.reshape((1, bp, sp) + hd), v_p.reshape((1, bp, sp) + hd), ki_p.reshape(1, bp, sp, IDX_DIM),
            wkv_p.reshape((1, bp) + (N_HEADS, HEAD_DIM, HEAD_DIM)), pr_p[sp - 1:sp].reshape(1, bp, RWKV_COLS),
            k_s.reshape((1, bs, ts) + hd), v_s.reshape((1, bs, ts) + hd), ki_s.reshape(1, bs, ts, IDX_DIM),
            wkv_s.reshape((1, bs) + (N_HEADS, HEAD_DIM, HEAD_DIM)), pr_s3[:, ts - 1].reshape(1, bs, RWKV_COLS))
```

```python
import functools
import math

import jax
import jax.numpy as jnp
import numpy as np
from jax import lax
from jax.experimental import pallas as pl
from jax.experimental.pallas import tpu as pltpu

F32 = jnp.float32
BF16 = jnp.bfloat16
I32 = jnp.int32

HEAD_DIM = 64
N_HEADS = 8
GROUP_WIDTH = N_HEADS * HEAD_DIM
DECAY_LORA, AAA_LORA, GATE_LORA = 64, 64, 128
RWKV_COLS = 3 * GROUP_WIDTH + DECAY_LORA + AAA_LORA + GATE_LORA
IDX_DIM = 64
N_IDX_HEADS = 8
TOPK_MAX = 256
N_BUCKETS = 32
MAX_DISTANCE = 128
N_EXPERTS = 32
TOP_K = 4
SWIGLU_LIMIT = 7.0
SWIGLU_ALPHA = 1.702
RMS_EPS = 1e-6
GN_EPS = HEAD_DIM * 1e-5
PAGE_SIZE = 128

LANES = 128
SUBLANES = 8
VMEM_LIMIT = 56 * 1024 * 1024

NEG_BIG = -1e30
LOG2E = math.log2(math.e)
KEY_NEG_INF = -2139095041
TIE_CODE = 0x7F800001


def _cparams(sem):
    return pltpu.CompilerParams(dimension_semantics=sem, vmem_limit_bytes=VMEM_LIMIT)


def _dot(a, b):
    return jnp.dot(a, b, preferred_element_type=F32)


def _dot_nt(a, b):
    return lax.dot_general(a, b, (((1,), (1,)), ((), ())), preferred_element_type=F32)


def _split3(a):
    hi = a.astype(BF16)
    r1 = a - hi.astype(F32)
    mid = r1.astype(BF16)
    lo = (r1 - mid.astype(F32)).astype(BF16)
    return hi, mid, lo


def _rep(x, n, axis):
    return jnp.concatenate([x] * n, axis=axis)


def _float_key(x):
    b = pltpu.bitcast(x, I32)
    return b ^ ((b >> 31) & 0x7FFFFFFF)


def _ada_kernel(c_ref, w_ref, b_ref, o_ref):
    c = c_ref[...]
    s = c * jax.nn.sigmoid(c)
    o_ref[...] = _dot(s.astype(BF16), w_ref[...].astype(BF16)) + b_ref[...]


def _adaln(c, w_ada, b_ada):
    r, d = c.shape
    n = w_ada.shape[1]
    tn = 1536
    return pl.pallas_call(
        _ada_kernel,
        grid=(n // tn,),
        in_specs=[pl.BlockSpec((r, d), lambda j: (0, 0)),
                  pl.BlockSpec((d, tn), lambda j: (0, j)),
                  pl.BlockSpec((1, tn), lambda j: (0, j))],
        out_specs=pl.BlockSpec((r, tn), lambda j: (0, j)),
        out_shape=jax.ShapeDtypeStruct((r, n), F32),
        compiler_params=_cparams(("arbitrary",)),
    )(c, w_ada, b_ada.reshape(1, n))


def _inproj_kernel(x_ref, g_ref, sh_ref, sc_ref, wr_ref, wa_ref, wk_ref,
                   pr_ref, k_ref, v_ref, ki_ref, wi_ref, qs_ref, kb_ref, vb_ref, qib_ref, kib_ref, vt_ref, *, q_scale):
    x = x_ref[...]
    y = x * lax.rsqrt(jnp.mean(x * x, axis=-1, keepdims=True) + RMS_EPS) * g_ref[...]
    h = (y * (1.0 + sc_ref[...]) + sh_ref[...]).astype(BF16)
    pr_ref[...] = _dot(h, wr_ref[...])
    a = _dot(h, wa_ref[...])
    w = GROUP_WIDTH
    q, k, v, qi = a[:, :w], a[:, w:2 * w], a[:, 2 * w:3 * w], a[:, 3 * w:4 * w]
    k_ref[...] = k
    v_ref[...] = v
    qs_ref[...] = (q * q_scale).astype(BF16)
    kb_ref[...] = k.astype(BF16)
    vb_ref[...] = v.astype(BF16)
    vt_ref[...] = v.T.astype(BF16)
    qib_ref[...] = (qi * IDX_DIM ** -0.5).astype(BF16)
    kw = _dot(h, wk_ref[...])
    ki = kw[:, :IDX_DIM]
    ki_ref[...] = ki
    kib_ref[...] = ki.astype(BF16)
    wi_ref[...] = kw[:, IDX_DIM:IDX_DIM + N_IDX_HEADS] * N_IDX_HEADS ** -0.5


def _mod_spec(mod, col, tm, d):
    if mod.shape[0] == 1:
        return pl.BlockSpec((1, d), lambda i: (0, col))
    return pl.BlockSpec((tm, d), lambda i: (i, col))


def _in_proj(x, norm_g, mod, w_in, q_scale):
    t, d = x.shape
    tm = min(512, t)
    w = GROUP_WIDTH
    a0 = RWKV_COLS
    wr = w_in[:, :a0].astype(BF16)
    wa = w_in[:, a0:a0 + 4 * w].astype(BF16)
    wk = jnp.pad(w_in[:, a0 + 4 * w:], ((0, 0), (0, LANES - IDX_DIM - N_IDX_HEADS))).astype(BF16)
    row = lambda n: pl.BlockSpec((tm, n), lambda i: (i, 0))
    full = lambda a: pl.BlockSpec(a.shape, lambda i: (0, 0))
    sds = lambda n, dt: jax.ShapeDtypeStruct((t, n), dt)
    return pl.pallas_call(
        functools.partial(_inproj_kernel, q_scale=q_scale),
        grid=(t // tm,),
        in_specs=[row(d), pl.BlockSpec((1, d), lambda i: (0, 0)), _mod_spec(mod, 0, tm, d), _mod_spec(mod, 1, tm, d),
                  full(wr), full(wa), full(wk)],
        out_specs=[row(a0), row(w), row(w), row(IDX_DIM), row(N_IDX_HEADS), row(w), row(w), row(w), row(w),
                   row(IDX_DIM), pl.BlockSpec((w, tm), lambda i: (0, i))],
        out_shape=[sds(a0, F32), sds(w, F32), sds(w, F32), sds(IDX_DIM, F32), sds(N_IDX_HEADS, F32),
                   sds(w, BF16), sds(w, BF16), sds(w, BF16), sds(w, BF16), sds(IDX_DIM, BF16),
                   jax.ShapeDtypeStruct((w, t), BF16)],
        compiler_params=_cparams(("arbitrary",)),
    )(x, norm_g.reshape(1, d), mod, mod, wr, wa, wk)


def _bias_band_kernel(rb_ref, o_ref, *, sign, off, scale):
    _, rows, cols = o_ref.shape
    r = lax.broadcasted_iota(I32, (rows, cols), 0)
    c = lax.broadcasted_iota(I32, (rows, cols), 1)
    n = jnp.maximum(sign * (r - c) + off, 0)
    max_exact = N_BUCKETS // 2
    nf = jnp.maximum(n, 1).astype(F32)
    large = max_exact + (jnp.log(nf / max_exact) / math.log(MAX_DISTANCE / max_exact)
                         * (N_BUCKETS - max_exact)).astype(I32)
    large = jnp.minimum(large, N_BUCKETS - 1)
    bucket = jnp.where(n < max_exact, n, large)
    for h in range(N_HEADS):
        far = rb_ref[N_BUCKETS - 1, h]
        acc = jnp.zeros((rows, cols), F32)
        for b in range(N_BUCKETS - 1):
            acc = jnp.where(bucket == b, (rb_ref[b, h] - far) * scale, acc)
        o_ref[h] = acc


def _bias_band(rel_bias, rows, cols, sign, off, scale=1.0):
    return pl.pallas_call(
        functools.partial(_bias_band_kernel, sign=sign, off=off, scale=scale),
        in_specs=[pl.BlockSpec(memory_space=pltpu.SMEM)],
        out_specs=pl.BlockSpec((N_HEADS, rows, cols), lambda: (0, 0, 0)),
        out_shape=jax.ShapeDtypeStruct((N_HEADS, rows, cols), F32),
    )(rel_bias)


def _kth_largest_key(count_ge, lo0, hi0, topk):
    def cond(carry):
        return carry[2] > 0

    def body(carry):
        lo, hi, _ = carry
        mid = (lo | hi) - ((lo ^ hi) >> 1)
        cnt = count_ge(mid)
        active = lo < hi
        exact = jnp.logical_and(active, cnt == topk)
        ge = cnt >= topk
        lo_n = jnp.where(active, jnp.where(ge, mid, lo), lo)
        hi_n = jnp.where(active, jnp.where(exact, mid, jnp.where(ge, hi, mid - 1)), hi)
        return lo_n, hi_n, jnp.max(jnp.where(lo_n < hi_n, 1.0, 0.0))

    lo, _, _ = lax.while_loop(cond, body, (lo0, hi0, jnp.max(jnp.where(lo0 < hi0, 1.0, 0.0))))
    return lo


PQB = 256
PKT = 512
PKB = 1024
PSUB = 256
CNT_ROWS = 64


def _prompt_att_kernel(qs_ref, qib_ref, wit_ref, kib_ref, kb_ref, vt_ref, band_ref, o_ref,
                       keys_ref, tau_ref, m_ref, l_ref, acc_ref, s_ref, p_ref, a_ref, *, topk, qb_rows):
    qb = pl.program_id(0)
    j = pl.program_id(1)
    q_lo = qb * qb_rows
    n_kt = (q_lo + qb_rows + PKT - 1) // PKT

    @pl.when(j == 0)
    def _index_phase():
        q_pos = q_lo + lax.broadcasted_iota(I32, (PKT, qb_rows), 1)

        def tile_body(kt, carry):
            k0 = pl.multiple_of(kt * PKT, PKT)
            ki = kib_ref[pl.ds(k0, PKT), :]
            acc = jnp.zeros((PKT, qb_rows), F32)
            for h in range(N_IDX_HEADS):
                s = _dot_nt(ki, qib_ref[:, h * IDX_DIM:(h + 1) * IDX_DIM])
                acc = acc + jnp.maximum(s, 0.0) * wit_ref[h:h + 1, :]
            k_pos = k0 + lax.broadcasted_iota(I32, (PKT, qb_rows), 0)
            causal = k_pos <= q_pos
            keys_ref[kt] = jnp.where(causal, _float_key(acc), KEY_NEG_INF)
            smax, smin = carry
            grp = lambda x: x.reshape(PKT // SUBLANES, SUBLANES, qb_rows)
            smax = jnp.maximum(smax, jnp.max(grp(jnp.where(causal, acc, -jnp.inf)), axis=0))
            smin = jnp.minimum(smin, jnp.min(grp(jnp.where(causal, acc, jnp.inf)), axis=0))
            return smax, smin

        smax, smin = lax.fori_loop(
            0, n_kt, tile_body,
            (jnp.full((SUBLANES, qb_rows), -jnp.inf, F32), jnp.full((SUBLANES, qb_rows), jnp.inf, F32)))
        fmax = jnp.max(smax, axis=0, keepdims=True)
        fmin = jnp.min(smin, axis=0, keepdims=True)
        hi0 = jnp.where(fmax == 0.0, 0, _float_key(fmax))
        lo0 = jnp.where(fmin == 0.0, -1, _float_key(fmin))
        n_causal = q_lo + 1 + lax.broadcasted_iota(I32, (1, qb_rows), 1)
        lo0 = jnp.where(n_causal < topk, KEY_NEG_INF + 1, lo0)
        hi0 = jnp.where(n_causal < topk, KEY_NEG_INF + 1, hi0)

        def count_ge(mid):
            def cbody(kt, c):
                for r0 in range(0, PKT, CNT_ROWS):
                    ge = jnp.where(keys_ref[kt, r0:r0 + CNT_ROWS, :] >= mid, 1.0, 0.0)
                    c = c + jnp.sum(ge.reshape(CNT_ROWS // SUBLANES, SUBLANES, qb_rows), axis=0)
                return c

            c = lax.fori_loop(0, n_kt, cbody, jnp.zeros((SUBLANES, qb_rows), F32))
            return jnp.sum(c, axis=0, keepdims=True)

        tau = _kth_largest_key(count_ge, lo0, hi0, topk)
        tau_ref[...] = tau
        over_f = jnp.where(count_ge(tau) > topk, 1.0, 0.0)

        @pl.when(jnp.max(over_f) > 0.0)
        def _break_ties():
            need = topk - count_ge(tau + 1)
            k_row = lax.broadcasted_iota(I32, (PKT, qb_rows), 0)
            top_code = TIE_CODE + (q_lo + qb_rows - 1)

            def tag_body(kt, _):
                k = keys_ref[kt]
                keys_ref[kt] = jnp.where(k == tau, (top_code - kt * PKT) - k_row, k)
                return 0

            lax.fori_loop(0, n_kt, tag_body, 0)
            row = jnp.zeros((1, qb_rows), I32)
            cut = _kth_largest_key(count_ge, row + TIE_CODE, row + top_code, need)
            drop_to = jnp.where(over_f > 0.5, KEY_NEG_INF, tau)

            def untag_body(kt, _):
                k = keys_ref[kt]
                keys_ref[kt] = jnp.where(k >= TIE_CODE, jnp.where(k >= cut, tau, drop_to), k)
                return 0

            lax.fori_loop(0, n_kt, untag_body, 0)

        m_ref[...] = jnp.full(m_ref.shape, NEG_BIG, F32)
        l_ref[...] = jnp.zeros(l_ref.shape, F32)
        acc_ref[...] = jnp.zeros(acc_ref.shape, F32)

    kb = j - 1
    last_kb = (q_lo + qb_rows - 1) // PKB

    n_sub = PKB // PSUB
    hsl = [slice(h * HEAD_DIM, (h + 1) * HEAD_DIM) for h in range(N_HEADS)]

    def logits_stage(u, b):
        for h in range(N_HEADS):
            s_ref[b, h] = _dot_nt(kb_ref[u * PSUB:(u + 1) * PSUB, hsl[h]], qs_ref[:, hsl[h]])

    def softmax_stage(u, b, near):
        s0 = kb * PKB + u * PSUB
        kt = s0 // PKT
        c0 = (u * PSUB) % PKT
        mask_bias = jnp.where(keys_ref[kt, c0:c0 + PSUB, :] >= tau_ref[...], 0.0, NEG_BIG)
        if near:
            which = jnp.clip((q_lo - s0) // PSUB, 0, 1)
        for h in range(N_HEADS):
            s = s_ref[b, h] + mask_bias
            if near:
                s = s + band_ref[h, 1 - which]
            m_old = m_ref[h]
            m_new = jnp.maximum(m_old, jnp.max(s, axis=0, keepdims=True))
            p = jnp.exp2(s - m_new)
            alpha = jnp.exp2(m_old - m_new)
            l_ref[h] = alpha * l_ref[h] + jnp.sum(p, axis=0, keepdims=True)
            m_ref[h] = m_new
            a_ref[b, h] = alpha
            p_ref[b, h] = p.astype(BF16)

    def values_stage(u, b):
        for h in range(N_HEADS):
            pv = _dot(vt_ref[hsl[h], u * PSUB:(u + 1) * PSUB], p_ref[b, h])
            acc_ref[h] = acc_ref[h] * a_ref[b, h] + pv

    attend = jnp.logical_and(j >= 1, kb <= last_kb)
    block_far = kb * PKB + PKB <= q_lo - PSUB

    @pl.when(jnp.logical_and(attend, block_far))
    def _attend_far_block():
        logits_stage(0, 0)
        for u in range(n_sub):
            if u + 1 < n_sub:
                logits_stage(u + 1, (u + 1) % 2)
            softmax_stage(u, u % 2, False)
            values_stage(u, u % 2)

    @pl.when(jnp.logical_and(attend, jnp.logical_not(block_far)))
    def _attend_near_block():
        for u in range(n_sub):
            s0 = kb * PKB + u * PSUB
            is_near = s0 + PSUB > q_lo - PSUB
            in_range = s0 < q_lo + qb_rows

            def sub_tile(near, u=u):
                logits_stage(u, 0)
                softmax_stage(u, 0, near)
                values_stage(u, 0)

            @pl.when(jnp.logical_and(in_range, is_near))
            def _():
                sub_tile(True)

            @pl.when(jnp.logical_and(in_range, jnp.logical_not(is_near)))
            def _():
                sub_tile(False)

    @pl.when(j == last_kb + 1)
    def _finish():
        for h in range(N_HEADS):
            hs = slice(h * HEAD_DIM, (h + 1) * HEAD_DIM)
            o_ref[:, hs] = (acc_ref[h] / l_ref[h]).T


def _prompt_attention(qs, qib, wi, kib, kb, vt, rel_bias):
    s, w = qs.shape
    topk = min(TOPK_MAX, s // 4)
    qb_rows = min(PQB, s)
    assert s % qb_rows == 0 and s % PKB == 0 and qb_rows == PSUB
    nqb = s // qb_rows
    nkb = s // PKB
    band = _bias_band(rel_bias, 2 * PSUB, qb_rows, -1, PSUB, LOG2E).reshape(N_HEADS, 2, PSUB, qb_rows)

    def last_kb(i):
        return (i * qb_rows + qb_rows - 1) // PKB

    qrow = lambda n: pl.BlockSpec((qb_rows, n), lambda i, j: (i, 0))
    return pl.pallas_call(
        functools.partial(_prompt_att_kernel, topk=topk, qb_rows=qb_rows),
        grid=(nqb, nkb + 1),
        in_specs=[qrow(w), qrow(w), pl.BlockSpec((N_IDX_HEADS, qb_rows), lambda i, j: (0, i)),
                  pl.BlockSpec((s, IDX_DIM), lambda i, j: (0, 0)),
                  pl.BlockSpec((PKB, w), lambda i, j: (jnp.minimum(jnp.maximum(j - 1, 0), last_kb(i)), 0)),
                  pl.BlockSpec((w, PKB), lambda i, j: (0, jnp.minimum(jnp.maximum(j - 1, 0), last_kb(i)))),
                  pl.BlockSpec(band.shape, lambda i, j: (0, 0, 0, 0))],
        out_specs=qrow(w),
        out_shape=jax.ShapeDtypeStruct((s, w), F32),
        scratch_shapes=[pltpu.VMEM((s // PKT, PKT, qb_rows), I32),
                        pltpu.VMEM((1, qb_rows), I32),
                        pltpu.VMEM((N_HEADS, 1, qb_rows), F32),
                        pltpu.VMEM((N_HEADS, 1, qb_rows), F32),
                        pltpu.VMEM((N_HEADS, HEAD_DIM, qb_rows), F32),
                        pltpu.VMEM((2, N_HEADS, PSUB, qb_rows), F32),
                        pltpu.VMEM((2, N_HEADS, PSUB, qb_rows), BF16),
                        pltpu.VMEM((2, N_HEADS, 1, qb_rows), F32)],
        compiler_params=_cparams(("arbitrary", "arbitrary")),
    )(qs, qib, wi.T, kib, kb, vt, band)


RC = 16
RNB = 16
RBLK = RC * RNB


def _dot_exact_lhs(m_bf, x):
    hi, mid, lo = _split3(x)
    return _dot(m_bf, hi) + (_dot(m_bf, mid) + _dot(m_bf, lo))


def _dot_exact_rhs(x, m_bf):
    hi, mid, lo = _split3(x)
    return _dot(hi, m_bf) + (_dot(mid, m_bf) + _dot(lo, m_bf))


def _rwkv_kernel(*refs, chain, t_valid):
    if chain:
        p_ref, = refs[:1]
        rest = refs[1:]
    else:
        p_ref, pprev_ref, s0_ref = refs[:3]
        rest = refs[3:]
    (mu_ref, w0_ref, wd2_ref, a0_ref, wa2_ref, wg2_ref, kk_ref, ka_ref, rk_ref, lng_ref, lnb_ref,
     lt_ref, bo_ref, bd_ref, out_ref, sfin_ref,
     carry_ref, s_ref, wa_s, rq_s, uv_s, yv_s, bt_s, kt_s, v_s, gc_s, y_s,
     np_s, ti_s, ak_s, rb_s, rk_s) = rest
    i = pl.program_id(0)
    w = GROUP_WIDTH
    p = p_ref[...]
    row = lax.broadcasted_iota(I32, (RBLK, 1), 0)
    if chain:
        @pl.when(i == 0)
        def _():
            carry_ref[...] = jnp.zeros(carry_ref.shape, F32)
            s_ref[...] = jnp.zeros(s_ref.shape, F32)

        pprev = jnp.where(row == 0, carry_ref[...], pltpu.roll(p, 1, axis=0))
        carry_ref[...] = p[RBLK - 1:RBLK, :]
    else:
        pprev = pprev_ref[...]
    ps = p + (pprev - p) * mu_ref[...]
    r, k, v = ps[:, :w], ps[:, w:2 * w], ps[:, 2 * w:3 * w]
    o = 3 * w
    xw = ps[:, o:o + DECAY_LORA]
    xa = ps[:, o + DECAY_LORA:o + DECAY_LORA + AAA_LORA]
    xg = ps[:, o + DECAY_LORA + AAA_LORA:]
    dec = w0_ref[...] + _dot(jnp.tanh(xw).astype(BF16), wd2_ref[...])
    softplus = jnp.maximum(-dec, 0.0) + jnp.log(1.0 + jnp.exp(-jnp.abs(dec)))
    lw = -jnp.exp(-softplus - 0.5)
    a = jax.nn.sigmoid(a0_ref[...] + _dot(xa.astype(BF16), wa2_ref[...]))
    g = _dot(jax.nn.sigmoid(xg).astype(BF16), wg2_ref[...])
    kk = k * kk_ref[...]
    kk = kk * lax.rsqrt(jnp.maximum(_dot_exact_rhs(kk * kk, bd_ref[...]), 1e-24))
    k2 = k * (1.0 + (a - 1.0) * ka_ref[...])
    alpha = -kk
    beta = kk * a
    if t_valid < RC:
        valid = (row % RC) < t_valid
        zero = lambda x: jnp.where(valid, x, 0.0)
        lw, alpha, beta, k2, r, v = zero(lw), zero(alpha), zero(beta), zero(k2), zero(r), zero(v)
    cl = _dot_exact_lhs(lt_ref[...], lw)
    ct = _dot_exact_lhs(bo_ref[...], lw)
    g_in = jnp.exp(cl)
    g_ex = jnp.exp(cl - lw)
    g_inv = jnp.exp(-cl)
    g_end = jnp.exp(ct - cl)
    g_all = jnp.exp(ct)
    at, rt = alpha * g_ex, r * g_in
    bh, kh = beta * g_inv, k2 * g_inv
    bt, kt = beta * g_end, k2 * g_end

    ri = lax.broadcasted_iota(I32, (RBLK, RBLK), 0)
    ci = lax.broadcasted_iota(I32, (RBLK, RBLK), 1)
    same = (ri // RC) == (ci // RC)
    strict = jnp.logical_and(same, ci < ri)
    incl = jnp.logical_and(same, ci <= ri)
    eye = (ri == ci).astype(F32)

    heads = range(N_HEADS)
    hsl = [slice(h * HEAD_DIM, (h + 1) * HEAD_DIM) for h in heads]
    for h in heads:
        hs = hsl[h]
        gm = _dot_nt(jnp.concatenate([at[:, hs], rt[:, hs]], axis=0).astype(BF16),
                     jnp.concatenate([bh[:, hs], kh[:, hs]], axis=0).astype(BF16))
        n1 = jnp.where(strict, gm[:RBLK, :RBLK], 0.0)
        np_s[h] = n1
        ti_s[h] = eye + n1
        ak_s[h] = jnp.where(strict, gm[:RBLK, RBLK:], 0.0)
        rb_s[h] = jnp.where(incl, gm[RBLK:, :RBLK], 0.0)
        rk_s[h] = jnp.where(incl, gm[RBLK:, RBLK:], 0.0)
        bt_s[h] = bt[:, hs]
        kt_s[h] = kt[:, hs]
        v_s[h] = v[:, hs]
        gc_s[h] = g_all[:, hs]
    for _ in range(int(math.log2(RC)) - 1):
        for h in heads:
            npow = np_s[h].astype(BF16)
            np_s[h] = _dot(npow, npow)
        for h in heads:
            tinv = ti_s[h]
            ti_s[h] = tinv + _dot(np_s[h].astype(BF16), tinv.astype(BF16))
    for h in heads:
        uv_s[h] = _dot(ak_s[h].astype(BF16), v_s[h].astype(BF16))
    for h in heads:
        wcat = _dot(ti_s[h].astype(BF16), jnp.concatenate([at[:, hsl[h]], uv_s[h]], axis=1).astype(BF16))
        wa_s[h] = wcat[:, :HEAD_DIM]
        uv_s[h] = wcat[:, HEAD_DIM:]
    for h in heads:
        ry =_dot(rb_s[h].astype(BF16), jnp.concatenate([wa_s[h], uv_s[h]], axis=1).astype(BF16))
        rq_s[h] = rt[:, hsl[h]] + ry[:, :HEAD_DIM]
        yv_s[h] = ry[:, HEAD_DIM:] + _dot(rk_s[h].astype(BF16), v_s[h].astype(BF16))

    def chunk_body(c, _):
        c0 = pl.multiple_of(c * RC, RC)
        sl = pl.ds(c0, RC)
        s_old = [s0_ref[c, h] if not chain else s_ref[h] for h in heads]
        res = [_dot_nt(jnp.concatenate([wa_s[h, sl, :], rq_s[h, sl, :]], axis=0).astype(BF16),
                       s_old[h].astype(BF16)) for h in heads]
        for h in heads:
            y_s[h, sl, :] = res[h][RC:] + yv_s[h, sl, :]
        upd = [lax.dot_general(jnp.concatenate([res[h][:RC] + uv_s[h, sl, :], v_s[h, sl, :]], axis=0).astype(BF16),
                               jnp.concatenate([bt_s[h, sl, :], kt_s[h, sl, :]], axis=0).astype(BF16),
                               (((0,), (0,)), ((), ())), preferred_element_type=F32) for h in heads]
        for h in heads:
            s_new = s_old[h] * gc_s[h, pl.ds(c0, 1), :] + upd[h]
            if chain:
                s_ref[h] = s_new
            else:
                sfin_ref[c, h] = s_new
        return 0

    lax.fori_loop(0, RNB, chunk_body, 0)
    if chain:
        sfin_ref[...] = s_ref[...]

    for h in range(N_HEADS):
        hs = slice(h * HEAD_DIM, (h + 1) * HEAD_DIM)
        y = y_s[h]
        mean = jnp.mean(y, axis=-1, keepdims=True)
        var = jnp.mean(jnp.square(y - mean), axis=-1, keepdims=True)
        yn = (y - mean) * lax.rsqrt(var + GN_EPS) * lng_ref[:, hs] + lnb_ref[:, hs]
        bonus = jnp.sum(r[:, hs] * k2[:, hs] * rk_ref[:, hs], axis=-1, keepdims=True) * v[:, hs]
        out_ref[:, hs] = (yn + bonus) * g[:, hs]


def _rwkv_consts():
    idx = np.arange(RBLK)
    same = (idx[:, None] // RC) == (idx[None, :] // RC)
    lt = (same & (idx[None, :] <= idx[:, None])).astype(np.float32)
    bo = same.astype(np.float32)
    lane = np.arange(GROUP_WIDTH)
    bd = ((lane[:, None] // HEAD_DIM) == (lane[None, :] // HEAD_DIM)).astype(np.float32)
    return jnp.asarray(lt, BF16), jnp.asarray(bo, BF16), jnp.asarray(bd, BF16)


def _rwkv(p, lw, chain, pprev=None, s0=None, t_valid=RC):
    rows = p.shape[0]
    nblk = rows // RBLK
    w = GROUP_WIDTH
    vec = lambda a: a.reshape(1, -1)
    consts = [vec(lw['mu_shift']), vec(lw['w0']), lw['w_decay2'].astype(BF16), vec(lw['a0']),
              lw['w_a2'].astype(BF16), lw['w_g2'].astype(BF16), vec(lw['k_k']), vec(lw['k_a']), vec(lw['r_k']),
              vec(lw['lnx_g']), vec(lw['lnx_b'])] + list(_rwkv_consts())
    full = lambda a: pl.BlockSpec(a.shape, lambda i: (0,) * a.ndim)
    blk = pl.BlockSpec((RBLK, RWKV_COLS), lambda i: (i, 0))
    hshape = (N_HEADS, HEAD_DIM, HEAD_DIM)
    if chain:
        ins, in_specs = [p], [blk]
        sfin_spec = pl.BlockSpec(hshape, lambda i: (0, 0, 0))
        sfin_shape = jax.ShapeDtypeStruct(hshape, F32)
    else:
        sspec = pl.BlockSpec((RNB,) + hshape, lambda i: (i, 0, 0, 0))
        ins, in_specs = [p, pprev, s0], [blk, blk, sspec]
        sfin_spec = sspec
        sfin_shape = jax.ShapeDtypeStruct((nblk * RNB,) + hshape, F32)
    hm = lambda: pltpu.VMEM((N_HEADS, RBLK, HEAD_DIM), F32)
    return pl.pallas_call(
        functools.partial(_rwkv_kernel, chain=chain, t_valid=t_valid),
        grid=(nblk,),
        in_specs=in_specs + [full(c) for c in consts],
        out_specs=[pl.BlockSpec((RBLK, w), lambda i: (i, 0)), sfin_spec],
        out_shape=[jax.ShapeDtypeStruct((rows, w), F32), sfin_shape],
        scratch_shapes=[pltpu.VMEM((1, RWKV_COLS), F32), pltpu.VMEM(hshape, F32)] + [hm() for _ in range(9)]
        + [pltpu.VMEM((N_HEADS, RBLK, RBLK), F32) for _ in range(5)],
        compiler_params=_cparams(("arbitrary",)),
    )(*ins, *consts)


RT_TM = 256


def _route_kernel(x_ref, ro_ref, ao_ref, wor_ref, woa_ref, g1_ref, sh_ref, sc_ref, n2_ref, wr_ref, br_ref,
                  ltri_ref, x1_ref, h2_ref, idx_ref, gate_ref, rank_ref, cnt_ref, run_ref):
    @pl.when(pl.program_id(0) == 0)
    def _():
        run_ref[...] = jnp.zeros(run_ref.shape, F32)

    tm = x_ref.shape[0]
    mix = _dot(ro_ref[...].astype(BF16), wor_ref[...]) + _dot(ao_ref[...].astype(BF16), woa_ref[...])
    x1 = x_ref[...] + g1_ref[...] * mix
    x1_ref[...] = x1
    y = x1 * lax.rsqrt(jnp.mean(x1 * x1, axis=-1, keepdims=True) + RMS_EPS) * n2_ref[...]
    h2 = (y * (1.0 + sc_ref[...]) + sh_ref[...]).astype(BF16)
    h2_ref[...] = h2
    logits = _dot(h2, wr_ref[...]) + br_ref[...]
    lane = lax.broadcasted_iota(I32, (tm, LANES), 1)
    lane_f = lane.astype(F32)
    lg = logits
    vals, idxs = [], []
    for _ in range(TOP_K):
        m = jnp.max(lg, axis=1, keepdims=True)
        idx = jnp.min(jnp.where(lg == m, lane_f, float(LANES)), axis=1, keepdims=True)
        vals.append(m)
        idxs.append(idx)
        lg = jnp.where(lane_f == idx, -3e38, lg)
    es = [jnp.exp(v - vals[0]) for v in vals]
    den = es[0] + es[1] + es[2] + es[3]
    ohs = [(lane_f == idx).astype(F32) for idx in idxs]
    oh_all = ohs[0] + ohs[1] + ohs[2] + ohs[3]
    base = run_ref[...] + _dot(ltri_ref[...], oh_all.astype(BF16))
    idx_out = jnp.zeros((tm, LANES), F32)
    gate_out = jnp.zeros((tm, LANES), F32)
    rank_out = jnp.zeros((tm, LANES), F32)
    for kk in range(TOP_K):
        rank = jnp.sum(ohs[kk] * base, axis=1, keepdims=True)
        idx_out = jnp.where(lane == kk, idxs[kk], idx_out)
        gate_out = jnp.where(lane == kk, es[kk] / den, gate_out)
        rank_out = jnp.where(lane == kk, rank, rank_out)
    idx_ref[...] = idx_out[:, :TOP_K].astype(I32)
    gate_ref[...] = gate_out[:, :TOP_K]
    rank_ref[...] = rank_out[:, :TOP_K].astype(I32)
    run_ref[...] = run_ref[...] + jnp.sum(oh_all, axis=0, keepdims=True)
    cnt_ref[...] = run_ref[...]


def _out_proj_route(x, ro, ao, w_out, mod, norm2_g, w_router, b_router):
    t, d = x.shape
    tm = min(RT_TM, t)
    w = GROUP_WIDTH
    wor = w_out[:w].astype(BF16)
    woa = w_out[w:].astype(BF16)
    wr = jnp.pad(w_router, ((0, 0), (0, LANES - N_EXPERTS))).astype(BF16)
    br = jnp.pad(b_router.reshape(1, -1), ((0, 0), (0, LANES - N_EXPERTS)), constant_values=NEG_BIG)
    ltri = jnp.asarray(np.tril(np.ones((tm, tm), np.float32), -1), BF16)
    row = lambda n: pl.BlockSpec((tm, n), lambda i: (i, 0))
    full = lambda a: pl.BlockSpec(a.shape, lambda i: (0, 0))
    return pl.pallas_call(
        _route_kernel,
        grid=(t // tm,),
        in_specs=[row(d), row(w), row(w), full(wor), full(woa),
                  _mod_spec(mod, 2, tm, d), _mod_spec(mod, 3, tm, d), _mod_spec(mod, 4, tm, d),
                  pl.BlockSpec((1, d), lambda i: (0, 0)), full(wr), full(br), full(ltri)],
        out_specs=[row(d), row(d), row(TOP_K), row(TOP_K), row(TOP_K), pl.BlockSpec((1, LANES), lambda i: (0, 0))],
        out_shape=[jax.ShapeDtypeStruct((t, d), F32), jax.ShapeDtypeStruct((t, d), BF16),
                   jax.ShapeDtypeStruct((t, TOP_K), I32), jax.ShapeDtypeStruct((t, TOP_K), F32),
                   jax.ShapeDtypeStruct((t, TOP_K), I32), jax.ShapeDtypeStruct((1, LANES), F32)],
        scratch_shapes=[pltpu.VMEM((1, LANES), F32)],
        compiler_params=_cparams(("arbitrary",)),
    )(x, ro, ao, wor, woa, mod, mod, mod, norm2_g.reshape(1, d), wr, br, ltri)


EX_TM = 256
BF16_ROW = (SUBLANES, LANES)


def _dispatch_kernel(slot_ref, h_ref, init_ref, out_ref, sem):
    del init_ref
    tm = h_ref.shape[0]

    def body(r, _):
        for kk in range(TOP_K):
            pltpu.make_async_copy(h_ref.at[r], out_ref.at[slot_ref[r * TOP_K + kk]], sem).start()
        return 0

    lax.fori_loop(0, tm, body, 0)
    for _ in range(TOP_K):
        pltpu.make_async_copy(h_ref, out_ref.at[pl.ds(0, tm)], sem).wait()


def _dispatch(h2, slot, rows_sorted):
    t, d = h2.shape
    tm = min(RT_TM, t)
    assert d == SUBLANES * LANES
    h3 = h2.reshape((t,) + BF16_ROW)
    return pl.pallas_call(
        _dispatch_kernel,
        grid=(t // tm,),
        in_specs=[pl.BlockSpec((tm * TOP_K,), lambda i: (i,), memory_space=pltpu.SMEM),
                  pl.BlockSpec((tm,) + BF16_ROW, lambda i: (i, 0, 0)),
                  pl.BlockSpec(memory_space=pl.ANY)],
        out_specs=pl.BlockSpec(memory_space=pl.ANY),
        out_shape=jax.ShapeDtypeStruct(rows_sorted.shape, rows_sorted.dtype),
        scratch_shapes=[pltpu.SemaphoreType.DMA(())],
        input_output_aliases={2: 0},
        compiler_params=_cparams(("arbitrary",)),
    )(slot.reshape(-1), h3, rows_sorted)


def _expert_kernel(be_ref, nv_ref, x_ref, wgu_ref, bgu_ref, wd_ref, bd_ref, o_ref, wgu_bf, wd_bf):
    i = pl.program_id(0)
    changed = jnp.logical_or(i == 0, be_ref[i] != be_ref[jnp.maximum(i - 1, 0)])

    @pl.when(changed)
    def _():
        wgu_bf[...] = wgu_ref[0].astype(BF16)
        wd_bf[...] = wd_ref[0].astype(BF16)

    @pl.when(i < nv_ref[0])
    def _():
        f = wd_bf.shape[0]
        gu = _dot(x_ref[...], wgu_bf[...]) + bgu_ref[0]
        glu = jnp.minimum(gu[:, :f], SWIGLU_LIMIT)
        lin = jnp.clip(gu[:, f:], -SWIGLU_LIMIT, SWIGLU_LIMIT)
        act = glu * jax.nn.sigmoid(SWIGLU_ALPHA * glu) * (lin + 1.0)
        o_ref[...] = _dot(act.astype(BF16), wd_bf[...]) + bd_ref[0]

    @pl.when(i >= nv_ref[0])
    def _():
        o_ref[...] = jnp.zeros(o_ref.shape, F32)


def _experts(rows_sorted, blk_e, n_valid, w_gu, b_gu, w_down, b_down):
    ns, d = rows_sorted.shape
    e, _, f2 = w_gu.shape
    f = f2 // 2
    grid_spec = pltpu.PrefetchScalarGridSpec(
        num_scalar_prefetch=2,
        grid=(ns // EX_TM,),
        in_specs=[pl.BlockSpec((EX_TM, d), lambda i, be, nv: (i, 0)),
                  pl.BlockSpec((1, d, f2), lambda i, be, nv: (be[i], 0, 0)),
                  pl.BlockSpec((1, 1, f2), lambda i, be, nv: (be[i], 0, 0)),
                  pl.BlockSpec((1, f, d), lambda i, be, nv: (be[i], 0, 0)),
                  pl.BlockSpec((1, 1, d), lambda i, be, nv: (be[i], 0, 0))],
        out_specs=pl.BlockSpec((EX_TM, d), lambda i, be, nv: (i, 0)),
        scratch_shapes=[pltpu.VMEM((d, f2), BF16), pltpu.VMEM((f, d), BF16)])
    return pl.pallas_call(
        _expert_kernel,
        grid_spec=grid_spec,
        out_shape=jax.ShapeDtypeStruct((ns, d), F32),
        compiler_params=_cparams(("arbitrary",)),
    )(blk_e, n_valid, rows_sorted, w_gu, b_gu.reshape(e, 1, f2), w_down, b_down.reshape(e, 1, d))


def _combine_kernel(slot_ref, rows_ref, gate_ref, x1_ref, g2_ref, nf_ref, y_ref, buf, sem):
    tm = x1_ref.shape[0]

    def body(r, _):
        for kk in range(TOP_K):
            pltpu.make_async_copy(rows_ref.at[pl.ds(slot_ref[r * TOP_K + kk], 1)],
                                  buf.at[kk, pl.ds(r, 1)], sem).start()
        return 0

    lax.fori_loop(0, tm, body, 0)
    for kk in range(TOP_K):
        pltpu.make_async_copy(rows_ref.at[pl.ds(0, tm)], buf.at[kk], sem).wait()
    gates = gate_ref[...]
    moe = gates[:, 0:1] * buf[0]
    for kk in range(1, TOP_K):
        moe = moe + gates[:, kk:kk + 1] * buf[kk]
    x2 = x1_ref[...] + g2_ref[...] * moe
    y_ref[...] = x2 * lax.rsqrt(jnp.mean(x2 * x2, axis=-1, keepdims=True) + RMS_EPS) * nf_ref[...]


def _combine(out_rows, slot, gates, x1, mod, normf_g):
    t, d = x1.shape
    tm = min(RT_TM, t)
    return pl.pallas_call(
        _combine_kernel,
        grid=(t // tm,),
        in_specs=[pl.BlockSpec((tm * TOP_K,), lambda i: (i,), memory_space=pltpu.SMEM),
                  pl.BlockSpec(memory_space=pl.ANY),
                  pl.BlockSpec((tm, TOP_K), lambda i: (i, 0)),
                  pl.BlockSpec((tm, d), lambda i: (i, 0)), _mod_spec(mod, 5, tm, d),
                  pl.BlockSpec((1, d), lambda i: (0, 0))],
        out_specs=pl.BlockSpec((tm, d), lambda i: (i, 0)),
        out_shape=jax.ShapeDtypeStruct((t, d), F32),
        scratch_shapes=[pltpu.VMEM((TOP_K, tm, d), F32), pltpu.SemaphoreType.DMA(())],
        compiler_params=_cparams(("arbitrary",)),
    )(slot.reshape(-1), out_rows, gates, x1, mod, normf_g.reshape(1, d))


def _moe_plan(idx_p, rank_p, cnt_p, idx_s, rank_s, cnt_s):
    n_assign = idx_p.size + idx_s.size
    n_tiles = -(-n_assign // EX_TM) + N_EXPERTS
    cp = cnt_p[0, :N_EXPERTS].astype(I32)
    cs = cnt_s[0, :N_EXPERTS].astype(I32)
    padded = (cp + cs + EX_TM - 1) // EX_TM * EX_TM
    pad_end = jnp.cumsum(padded)
    pad_start = pad_end - padded
    slot_p = pad_start[idx_p] + rank_p
    slot_s = pad_start[idx_s] + cp[idx_s] + rank_s
    tile_row = jnp.arange(n_tiles, dtype=I32)[:, None] * EX_TM
    blk_e = jnp.minimum(jnp.sum((pad_end[None, :] <= tile_row).astype(I32), axis=1), N_EXPERTS - 1)
    n_valid = (pad_end[-1:] // EX_TM).astype(I32)
    return slot_p, slot_s, blk_e, n_valid, n_tiles * EX_TM


PG = 16
QROWS = N_HEADS * SUBLANES


def _page_specs(block, n_pages):
    def spec(u):
        return pl.BlockSpec(block, lambda b, j, pt: (pt[b * n_pages + j * PG + u],) + (0,) * (len(block) - 1))
    return [spec(u) for u in range(PG)]


def _head_sum(x):
    out = x[:SUBLANES]
    for h in range(1, N_HEADS):
        out = out + x[h * SUBLANES:(h + 1) * SUBLANES]
    return out


def _sample_index_kernel(pt_ref, qi_ref, wrep_ref, kin_ref, *rest, n_pages, t_new, topk):
    del pt_ref
    pages = rest[:PG]
    keys_ref, tau_ref = rest[PG:]
    j = pl.program_id(1)
    qi = qi_ref[0]
    wrep = wrep_ref[0]

    def scores(ki_t_bf):
        s = _dot(qi, ki_t_bf)
        return _head_sum(jnp.maximum(s, 0.0) * wrep)

    for u in range(PG):
        keys_ref[0, j * PG + u] = _float_key(scores(pages[u][0].astype(BF16)))

    @pl.when(j == pl.num_programs(1) - 1)
    def _():
        qrow = lax.broadcasted_iota(I32, (SUBLANES, PAGE_SIZE), 0)
        col = lax.broadcasted_iota(I32, (SUBLANES, PAGE_SIZE), 1)
        ok = jnp.logical_and(col <= qrow, col < t_new)
        keys_ref[0, n_pages] = jnp.where(ok, _float_key(scores(kin_ref[0])), KEY_NEG_INF)

        def count_ge(mid):
            c = jnp.sum(jnp.where(keys_ref[0] >= mid[None], 1.0, 0.0), axis=0)
            return jnp.broadcast_to(jnp.sum(c, axis=1, keepdims=True), (SUBLANES, LANES))

        tau = _kth_largest_key(count_ge, jnp.full((SUBLANES, LANES), KEY_NEG_INF + 1, I32),
                               jnp.full((SUBLANES, LANES), 0x7F800000, I32), topk)
        tau_ref[0] = tau
        over_f = jnp.where(count_ge(tau) > topk, 1.0, 0.0)

        @pl.when(jnp.max(over_f) > 0.0)
        def _break_ties():
            need = topk - count_ge(tau + 1)
            shape = keys_ref.shape[1:]
            pos = lax.broadcasted_iota(I32, shape, 0) * PAGE_SIZE + lax.broadcasted_iota(I32, shape, 2)
            tied = keys_ref[0] == tau[None]

            def pbody(_, carry):
                lo, hi = carry
                mid = (lo + hi) >> 1
                c = jnp.sum(jnp.where(tied, jnp.where(pos <= mid[None], 1.0, 0.0), 0.0), axis=0)
                ok = jnp.broadcast_to(jnp.sum(c, axis=1, keepdims=True), (SUBLANES, LANES)) >= need
                return jnp.where(ok, lo, mid + 1), jnp.where(ok, mid, hi)

            n_pos = (n_pages + 1) * PAGE_SIZE
            last_pos, _ = lax.fori_loop(0, (n_pos - 1).bit_length(), pbody,
                                        (jnp.zeros((SUBLANES, LANES), I32), jnp.full((SUBLANES, LANES), n_pos - 1, I32)))
            gone = jnp.where(tied, jnp.where(pos > last_pos[None], over_f[None], 0.0), 0.0)
            keys_ref[0] = jnp.where(gone > 0.5, KEY_NEG_INF, keys_ref[0])


def _sample_attend_kernel(pt_ref, q_ref, keys_ref, tau_ref, band_ref, kn_ref, vn_ref, *rest, n_pages):
    del pt_ref
    kpages = rest[:PG]
    vpages = rest[PG:2 * PG]
    o_ref, m_ref, l_ref, acc_ref = rest[2 * PG:]
    j = pl.program_id(1)
    last = j == pl.num_programs(1) - 1

    @pl.when(j == 0)
    def _():
        m_ref[...] = jnp.full(m_ref.shape, NEG_BIG, F32)
        l_ref[...] = jnp.zeros(l_ref.shape, F32)
        acc_ref[...] = jnp.zeros(acc_ref.shape, F32)

    tau = tau_ref[0]
    heads = range(N_HEADS)

    def attend(k_of, v_of, key_tiles, bias_of):
        pages = range(len(key_tiles))
        sel = [kt >= tau for kt in key_tiles]
        s = [[_dot(q_ref[0, h], k_of(u, h)) for u in pages] for h in heads]
        ps, alphas = [], []
        for h in heads:
            sh = []
            for u in pages:
                b_uh = bias_of(u, h)
                sh.append(jnp.where(sel[u], s[h][u] if b_uh is None else s[h][u] + b_uh, NEG_BIG))
            tile_max = functools.reduce(jnp.maximum, sh)
            m_old = m_ref[h]
            m_new = jnp.maximum(m_old, jnp.broadcast_to(jnp.max(tile_max, axis=1, keepdims=True), m_old.shape))
            p = [jnp.exp(x - m_new) for x in sh]
            alpha = jnp.exp(m_old - m_new)
            p_sum = functools.reduce(jnp.add, p)
            l_ref[h] = alpha * l_ref[h] + jnp.broadcast_to(jnp.sum(p_sum, axis=1, keepdims=True), m_old.shape)
            m_ref[h] = m_new
            ps.append([x.astype(BF16) for x in p])
            alphas.append(alpha[:, :HEAD_DIM])
        pv = [[_dot_nt(ps[h][u], v_of(u, h)) for u in pages] for h in heads]
        for h in heads:
            acc_ref[h] = acc_ref[h] * alphas[h] + functools.reduce(jnp.add, pv[h])

    attend(lambda u, h: kpages[u][0, h].astype(BF16), lambda u, h: vpages[u][0, h].astype(BF16),
           [keys_ref[0, j * PG + u] for u in range(PG)],
           lambda u, h: jnp.where(last, band_ref[h, :, :PAGE_SIZE], 0.0) if u == PG - 1 else None)

    @pl.when(last)
    def _():
        hsl = lambda h: slice(h * HEAD_DIM, (h + 1) * HEAD_DIM)
        attend(lambda u, h: kn_ref[0, hsl(h), :], lambda u, h: vn_ref[0, hsl(h), :], [keys_ref[0, n_pages]],
               lambda u, h: band_ref[h, :, PAGE_SIZE:])
        for h in heads:
            o_ref[0, :, hsl(h)] = acc_ref[h] / l_ref[h][:, :HEAD_DIM]


def _sample_attention(qs, qib, wi, k_new_bf, v_new_bf, ki_new_bf, cache_k, cache_v, cache_kidx, page_table,
                      rel_bias, b, t_new):
    w = GROUP_WIDTH
    n_pages = page_table.shape[1]
    assert n_pages % PG == 0 and t_new <= SUBLANES and cache_k.shape[1] == PAGE_SIZE
    past = n_pages * PAGE_SIZE
    topk = min(TOPK_MAX, (past + t_new) // 4)
    pt = page_table.reshape(-1)
    padq = lambda a: jnp.pad(a, ((0, 0), (0, SUBLANES - t_new)) + ((0, 0),) * (a.ndim - 2))
    padk = lambda a: jnp.pad(a.reshape(b, t_new, -1), ((0, 0), (0, PAGE_SIZE - t_new), (0, 0))).transpose(0, 2, 1)
    qi_r = padq(qib.reshape(b, t_new, N_IDX_HEADS, IDX_DIM)).transpose(0, 2, 1, 3).reshape(b, QROWS, IDX_DIM)
    w_r = padq(wi.reshape(b, t_new, N_IDX_HEADS)).transpose(0, 2, 1).reshape(b, QROWS, 1)
    w_r = jnp.broadcast_to(w_r, (b, QROWS, LANES))
    q4 = padq(qs.reshape(b, t_new, N_HEADS, HEAD_DIM)).transpose(0, 2, 1, 3)
    kin, kn, vn = padk(ki_new_bf), padk(k_new_bf), padk(v_new_bf)
    band = _bias_band(rel_bias, SUBLANES, 2 * PAGE_SIZE, 1, PAGE_SIZE)
    ck_t = cache_k.transpose(0, 2, 3, 1)
    cv_t = cache_v.transpose(0, 2, 3, 1)
    cki_t = cache_kidx.transpose(0, 2, 1)
    page_block = (1, N_HEADS, HEAD_DIM, PAGE_SIZE)

    per_b = lambda shape: pl.BlockSpec((1,) + shape, lambda bb, j, p_: (bb,) + (0,) * len(shape))
    steps = n_pages // PG
    keys, tau = pl.pallas_call(
        functools.partial(_sample_index_kernel, n_pages=n_pages, t_new=t_new, topk=topk),
        grid_spec=pltpu.PrefetchScalarGridSpec(
            num_scalar_prefetch=1, grid=(b, steps),
            in_specs=[per_b((QROWS, IDX_DIM)), per_b((QROWS, LANES)), per_b((IDX_DIM, PAGE_SIZE))]
            + _page_specs((1, IDX_DIM, PAGE_SIZE), n_pages),
            out_specs=[per_b((n_pages + 1, SUBLANES, PAGE_SIZE)), per_b((SUBLANES, LANES))]),
        out_shape=[jax.ShapeDtypeStruct((b, n_pages + 1, SUBLANES, PAGE_SIZE), I32),
                   jax.ShapeDtypeStruct((b, SUBLANES, LANES), I32)],
        compiler_params=_cparams(("arbitrary", "arbitrary")),
    )(pt, qi_r, w_r, kin, *([cki_t] * PG))
    out = pl.pallas_call(
        functools.partial(_sample_attend_kernel, n_pages=n_pages),
        grid_spec=pltpu.PrefetchScalarGridSpec(
            num_scalar_prefetch=1, grid=(b, steps),
            in_specs=[per_b((N_HEADS, SUBLANES, HEAD_DIM)), per_b((n_pages + 1, SUBLANES, PAGE_SIZE)),
                      per_b((SUBLANES, LANES)), pl.BlockSpec(band.shape, lambda bb, j, p_: (0, 0, 0)),
                      per_b((w, PAGE_SIZE)), per_b((w, PAGE_SIZE))]
            + _page_specs(page_block, n_pages) + _page_specs(page_block, n_pages),
            out_specs=per_b((SUBLANES, w)),
            scratch_shapes=[pltpu.VMEM((N_HEADS, SUBLANES, LANES), F32), pltpu.VMEM((N_HEADS, SUBLANES, LANES), F32),
                            pltpu.VMEM((N_HEADS, SUBLANES, HEAD_DIM), F32)]),
        out_shape=jax.ShapeDtypeStruct((b, SUBLANES, w), F32),
        compiler_params=_cparams(("arbitrary", "arbitrary")),
    )(pt, q4, keys, tau, band, kn, vn, *([ck_t] * PG), *([cv_t] * PG))
    return out[:, :t_new].reshape(b * t_new, w)


def kernel(x_prompt, x_sample, c_prompt, c_sample, cache_k, cache_v, cache_kidx, page_table, state_wkv,
           state_shift, w_ada, b_ada, norm1_g, w_in, mu_shift, w0, w_decay2, a0, w_a2, w_g2, k_k, k_a, r_k,
           lnx_g, lnx_b, rel_bias, w_out, norm2_g, w_router, b_router, w_gu, b_gu, w_down, b_down, normf_g):
    depth = w_in.shape[0]
    assert depth == 1, "the merged prompt+sample expert pass is written for a single layer"
    bp, sp, d = x_prompt.shape
    bs, ts, _ = x_sample.shape
    assert bp == 1 and sp % RBLK == 0 and bs % RNB == 0 and ts <= RC
    l = 0
    lw = {'mu_shift': mu_shift[l], 'w0': w0[l], 'w_decay2': w_decay2[l], 'a0': a0[l], 'w_a2': w_a2[l],
          'w_g2': w_g2[l], 'k_k': k_k[l], 'k_a': k_a[l], 'r_k': r_k[l], 'lnx_g': lnx_g[l], 'lnx_b': lnx_b[l]}
    ns = bs * ts

    c_all = jnp.concatenate([c_prompt, c_sample], axis=0)
    n_c = c_all.shape[0]
    c_all = jnp.pad(c_all, ((0, -n_c % SUBLANES), (0, 0)))
    mod = _adaln(c_all, w_ada[l], b_ada[l])
    mod_p = mod[0:1]
    mod_s = jnp.repeat(mod[bp:bp + bs], ts, axis=0)

    xp = x_prompt.reshape(sp, d)
    xs = x_sample.reshape(ns, d)
    pr_p, k_p, v_p, ki_p, wi_p, qs_p, kb_p, _, qib_p, kib_p, vt_p = _in_proj(xp, norm1_g[l], mod_p, w_in[l],
                                                                             HEAD_DIM ** -0.5 * LOG2E)
    pr_s, k_s, v_s, ki_s, wi_s, qs_s, kb_s, vb_s, qib_s, kib_s, _ = _in_proj(xs, norm1_g[l], mod_s, w_in[l],
                                                                             HEAD_DIM ** -0.5)

    ro_p, wkv_p = _rwkv(pr_p, lw, chain=True)
    att_p = _prompt_attention(qs_p, qib_p, wi_p, kib_p, kb_p, vt_p, rel_bias)

    pr_s3 = pr_s.reshape(bs, ts, RWKV_COLS)
    prev_s3 = jnp.concatenate([state_shift[l][:, None, :], pr_s3[:, :-1]], axis=1)
    padc = lambda a: jnp.pad(a, ((0, 0), (0, RC - ts), (0, 0))).reshape(bs * RC, RWKV_COLS)
    ro_s, wkv_s = _rwkv(padc(pr_s3), lw, chain=False, pprev=padc(prev_s3), s0=state_wkv[l], t_valid=ts)
    ro_s = ro_s.reshape(bs, RC, GROUP_WIDTH)[:, :ts].reshape(ns, GROUP_WIDTH)
    att_s = _sample_attention(qs_s, qib_s, wi_s, kb_s, vb_s, kib_s, cache_k[l], cache_v[l], cache_kidx[l],
                              page_table, rel_bias, bs, ts)

    x1_p, h2_p, idx_p, gate_p, rank_p, cnt_p = _out_proj_route(xp, ro_p, att_p, w_out[l], mod_p, norm2_g[l],
                                                               w_router[l], b_router[l])
    x1_s, h2_s, idx_s, gate_s, rank_s, cnt_s = _out_proj_route(xs, ro_s, att_s, w_out[l], mod_s, norm2_g[l],
                                                               w_router[l], b_router[l])
    slot_p, slot_s, blk_e, n_valid, n_slots = _moe_plan(idx_p, rank_p, cnt_p, idx_s, rank_s, cnt_s)
    rows = jnp.zeros((n_slots,) + BF16_ROW, BF16)
    rows = _dispatch(h2_p, slot_p, rows)
    rows = _dispatch(h2_s, slot_s, rows)
    out_rows = _experts(rows.reshape(n_slots, d), blk_e, n_valid, w_gu[l], b_gu[l], w_down[l], b_down[l])
    y_p = _combine(out_rows, slot_p, gate_p, x1_p, mod_p, normf_g)
    y_s = _combine(out_rows, slot_s, gate_s, x1_s, mod_s, normf_g)

    hd = (N_HEADS, HEAD_DIM)
    return (y_p.reshape(bp, sp, d), y_s.reshape(bs, ts, d),
            k_p.reshape((1, bp, sp) + hd), v_p.reshape((1, bp, sp) + hd), ki_p.reshape(1, bp, sp, IDX_DIM),
            wkv_p.reshape((1, bp) + (N_HEADS, HEAD_DIM, HEAD_DIM)), pr_p[sp - 1:sp].reshape(1, bp, RWKV_COLS),
            k_s.reshape((1, bs, ts) + hd), v_s.reshape((1, bs, ts) + hd), ki_s.reshape(1, bs, ts, IDX_DIM),
            wkv_s.reshape((1, bs) + (N_HEADS, HEAD_DIM, HEAD_DIM)), pr_s3[:, ts - 1].reshape(1, bs, RWKV_COLS))
```

```python
import functools
import math

import jax
import jax.numpy as jnp
import numpy as np
from jax import lax
from jax.experimental import pallas as pl
from jax.experimental.pallas import tpu as pltpu

F32 = jnp.float32
BF16 = jnp.bfloat16
I32 = jnp.int32

HEAD_DIM = 64
N_HEADS = 8
GROUP_WIDTH = N_HEADS * HEAD_DIM
DECAY_LORA, AAA_LORA, GATE_LORA = 64, 64, 128
RWKV_COLS = 3 * GROUP_WIDTH + DECAY_LORA + AAA_LORA + GATE_LORA
IDX_DIM = 64
N_IDX_HEADS = 8
TOPK_MAX = 256
N_BUCKETS = 32
MAX_DISTANCE = 128
N_EXPERTS = 32
TOP_K = 4
SWIGLU_LIMIT = 7.0
SWIGLU_ALPHA = 1.702
RMS_EPS = 1e-6
GN_EPS = HEAD_DIM * 1e-5
PAGE_SIZE = 128

LANES = 128
SUBLANES = 8
VMEM_LIMIT = 56 * 1024 * 1024

NEG_BIG = -1e30
LOG2E = math.log2(math.e)
KEY_NEG_INF = -2139095041
TIE_CODE = 0x7F800001


def _cparams(sem):
    return pltpu.CompilerParams(dimension_semantics=sem, vmem_limit_bytes=VMEM_LIMIT)


def _dot(a, b):
    return jnp.dot(a, b, preferred_element_type=F32)


def _dot_nt(a, b):
    return lax.dot_general(a, b, (((1,), (1,)), ((), ())), preferred_element_type=F32)


def _split3(a):
    hi = a.astype(BF16)
    r1 = a - hi.astype(F32)
    mid = r1.astype(BF16)
    lo = (r1 - mid.astype(F32)).astype(BF16)
    return hi, mid, lo


def _rep(x, n, axis):
    return jnp.concatenate([x] * n, axis=axis)


def _float_key(x):
    b = pltpu.bitcast(x, I32)
    return b ^ ((b >> 31) & 0x7FFFFFFF)


def _ada_kernel(c_ref, w_ref, b_ref, o_ref):
    c = c_ref[...]
    s = c * jax.nn.sigmoid(c)
    o_ref[...] = _dot(s.astype(BF16), w_ref[...].astype(BF16)) + b_ref[...]


def _adaln(c, w_ada, b_ada):
    r, d = c.shape
    n = w_ada.shape[1]
    tn = 1536
    return pl.pallas_call(
        _ada_kernel,
        grid=(n // tn,),
        in_specs=[pl.BlockSpec((r, d), lambda j: (0, 0)),
                  pl.BlockSpec((d, tn), lambda j: (0, j)),
                  pl.BlockSpec((1, tn), lambda j: (0, j))],
        out_specs=pl.BlockSpec((r, tn), lambda j: (0, j)),
        out_shape=jax.ShapeDtypeStruct((r, n), F32),
        compiler_params=_cparams(("arbitrary",)),
    )(c, w_ada, b_ada.reshape(1, n))


def _inproj_kernel(x_ref, g_ref, sh_ref, sc_ref, wr_ref, wa_ref, wk_ref,
                   pr_ref, k_ref, v_ref, ki_ref, wi_ref, qs_ref, kb_ref, vb_ref, qib_ref, kib_ref, vt_ref, *, q_scale):
    x = x_ref[...]
    y = x * lax.rsqrt(jnp.mean(x * x, axis=-1, keepdims=True) + RMS_EPS) * g_ref[...]
    h = (y * (1.0 + sc_ref[...]) + sh_ref[...]).astype(BF16)
    pr_ref[...] = _dot(h, wr_ref[...])
    a = _dot(h, wa_ref[...])
    w = GROUP_WIDTH
    q, k, v, qi = a[:, :w], a[:, w:2 * w], a[:, 2 * w:3 * w], a[:, 3 * w:4 * w]
    k_ref[...] = k
    v_ref[...] = v
    qs_ref[...] = (q * q_scale).astype(BF16)
    kb_ref[...] = k.astype(BF16)
    vb_ref[...] = v.astype(BF16)
    vt_ref[...] = v.T.astype(BF16)
    qib_ref[...] = (qi * IDX_DIM ** -0.5).astype(BF16)
    kw = _dot(h, wk_ref[...])
    ki = kw[:, :IDX_DIM]
    ki_ref[...] = ki
    kib_ref[...] = ki.astype(BF16)
    wi_ref[...] = kw[:, IDX_DIM:IDX_DIM + N_IDX_HEADS] * N_IDX_HEADS ** -0.5


def _mod_spec(mod, col, tm, d):
    if mod.shape[0] == 1:
        return pl.BlockSpec((1, d), lambda i: (0, col))
    return pl.BlockSpec((tm, d), lambda i: (i, col))


def _in_proj(x, norm_g, mod, w_in, q_scale):
    t, d = x.shape
    tm = min(512, t)
    w = GROUP_WIDTH
    a0 = RWKV_COLS
    wr = w_in[:, :a0].astype(BF16)
    wa = w_in[:, a0:a0 + 4 * w].astype(BF16)
    wk = jnp.pad(w_in[:, a0 + 4 * w:], ((0, 0), (0, LANES - IDX_DIM - N_IDX_HEADS))).astype(BF16)
    row = lambda n: pl.BlockSpec((tm, n), lambda i: (i, 0))
    full = lambda a: pl.BlockSpec(a.shape, lambda i: (0, 0))
    sds = lambda n, dt: jax.ShapeDtypeStruct((t, n), dt)
    return pl.pallas_call(
        functools.partial(_inproj_kernel, q_scale=q_scale),
        grid=(t // tm,),
        in_specs=[row(d), pl.BlockSpec((1, d), lambda i: (0, 0)), _mod_spec(mod, 0, tm, d), _mod_spec(mod, 1, tm, d),
                  full(wr), full(wa), full(wk)],
        out_specs=[row(a0), row(w), row(w), row(IDX_DIM), row(N_IDX_HEADS), row(w), row(w), row(w), row(w),
                   row(IDX_DIM), pl.BlockSpec((w, tm), lambda i: (0, i))],
        out_shape=[sds(a0, F32), sds(w, F32), sds(w, F32), sds(IDX_DIM, F32), sds(N_IDX_HEADS, F32),
                   sds(w, BF16), sds(w, BF16), sds(w, BF16), sds(w, BF16), sds(IDX_DIM, BF16),
                   jax.ShapeDtypeStruct((w, t), BF16)],
        compiler_params=_cparams(("arbitrary",)),
    )(x, norm_g.reshape(1, d), mod, mod, wr, wa, wk)


def _bias_band_kernel(rb_ref, o_ref, *, sign, off, scale):
    _, rows, cols = o_ref.shape
    r = lax.broadcasted_iota(I32, (rows, cols), 0)
    c = lax.broadcasted_iota(I32, (rows, cols), 1)
    n = jnp.maximum(sign * (r - c) + off, 0)
    max_exact = N_BUCKETS // 2
    nf = jnp.maximum(n, 1).astype(F32)
    large = max_exact + (jnp.log(nf / max_exact) / math.log(MAX_DISTANCE / max_exact)
                         * (N_BUCKETS - max_exact)).astype(I32)
    large = jnp.minimum(large, N_BUCKETS - 1)
    bucket = jnp.where(n < max_exact, n, large)
    for h in range(N_HEADS):
        far = rb_ref[N_BUCKETS - 1, h]
        acc = jnp.zeros((rows, cols), F32)
        for b in range(N_BUCKETS - 1):
            acc = jnp.where(bucket == b, (rb_ref[b, h] - far) * scale, acc)
        o_ref[h] = acc


def _bias_band(rel_bias, rows, cols, sign, off, scale=1.0):
    return pl.pallas_call(
        functools.partial(_bias_band_kernel, sign=sign, off=off, scale=scale),
        in_specs=[pl.BlockSpec(memory_space=pltpu.SMEM)],
        out_specs=pl.BlockSpec((N_HEADS, rows, cols), lambda: (0, 0, 0)),
        out_shape=jax.ShapeDtypeStruct((N_HEADS, rows, cols), F32),
    )(rel_bias)


def _kth_largest_key(count_ge, lo0, hi0, topk):
    def cond(carry):
        return carry[2] > 0

    def body(carry):
        lo, hi, _ = carry
        mid = (lo | hi) - ((lo ^ hi) >> 1)
        cnt = count_ge(mid)
        active = lo < hi
        exact = jnp.logical_and(active, cnt == topk)
        ge = cnt >= topk
        lo_n = jnp.where(active, jnp.where(ge, mid, lo), lo)
        hi_n = jnp.where(active, jnp.where(exact, mid, jnp.where(ge, hi, mid - 1)), hi)
        return lo_n, hi_n, jnp.max(jnp.where(lo_n < hi_n, 1.0, 0.0))

    lo, _, _ = lax.while_loop(cond, body, (lo0, hi0, jnp.max(jnp.where(lo0 < hi0, 1.0, 0.0))))
    return lo


PQB = 256
PKT = 512
PKB = 1024
PSUB = 256
CNT_ROWS = 64


def _prompt_att_kernel(qs_ref, qib_ref, wit_ref, kib_ref, kb_ref, vt_ref, band_ref, o_ref,
                       keys_ref, tau_ref, m_ref, l_ref, acc_ref, s_ref, p_ref, a_ref, *, topk, qb_rows):
    qb = pl.program_id(0)
    j = pl.program_id(1)
    q_lo = qb * qb_rows
    n_kt = (q_lo + qb_rows + PKT - 1) // PKT

    @pl.when(j == 0)
    def _index_phase():
        q_pos = q_lo + lax.broadcasted_iota(I32, (PKT, qb_rows), 1)

        def tile_body(kt, carry):
            k0 = pl.multiple_of(kt * PKT, PKT)
            ki = kib_ref[pl.ds(k0, PKT), :]
            acc = jnp.zeros((PKT, qb_rows), F32)
            for h in range(N_IDX_HEADS):
                s = _dot_nt(ki, qib_ref[:, h * IDX_DIM:(h + 1) * IDX_DIM])
                acc = acc + jnp.maximum(s, 0.0) * wit_ref[h:h + 1, :]
            k_pos = k0 + lax.broadcasted_iota(I32, (PKT, qb_rows), 0)
            causal = k_pos <= q_pos
            keys_ref[kt] = jnp.where(causal, _float_key(acc), KEY_NEG_INF)
            smax, smin = carry
            grp = lambda x: x.reshape(PKT // SUBLANES, SUBLANES, qb_rows)
            smax = jnp.maximum(smax, jnp.max(grp(jnp.where(causal, acc, -jnp.inf)), axis=0))
            smin = jnp.minimum(smin, jnp.min(grp(jnp.where(causal, acc, jnp.inf)), axis=0))
            return smax, smin

        smax, smin = lax.fori_loop(
            0, n_kt, tile_body,
            (jnp.full((SUBLANES, qb_rows), -jnp.inf, F32), jnp.full((SUBLANES, qb_rows), jnp.inf, F32)))
        fmax = jnp.max(smax, axis=0, keepdims=True)
        fmin = jnp.min(smin, axis=0, keepdims=True)
        hi0 = jnp.where(fmax == 0.0, 0, _float_key(fmax))
        lo0 = jnp.where(fmin == 0.0, -1, _float_key(fmin))
        n_causal = q_lo + 1 + lax.broadcasted_iota(I32, (1, qb_rows), 1)
        lo0 = jnp.where(n_causal < topk, KEY_NEG_INF + 1, lo0)
        hi0 = jnp.where(n_causal < topk, KEY_NEG_INF + 1, hi0)

        def count_ge(mid):
            def cbody(kt, c):
                for r0 in range(0, PKT, CNT_ROWS):
                    ge = jnp.where(keys_ref[kt, r0:r0 + CNT_ROWS, :] >= mid, 1.0, 0.0)
                    c = c + jnp.sum(ge.reshape(CNT_ROWS // SUBLANES, SUBLANES, qb_rows), axis=0)
                return c

            c = lax.fori_loop(0, n_kt, cbody, jnp.zeros((SUBLANES, qb_rows), F32))
            return jnp.sum(c, axis=0, keepdims=True)

        tau = _kth_largest_key(count_ge, lo0, hi0, topk)
        tau_ref[...] = tau
        over_f = jnp.where(count_ge(tau) > topk, 1.0, 0.0)

        @pl.when(jnp.max(over_f) > 0.0)
        def _break_ties():
            need = topk - count_ge(tau + 1)
            k_row = lax.broadcasted_iota(I32, (PKT, qb_rows), 0)
            top_code = TIE_CODE + (q_lo + qb_rows - 1)

            def tag_body(kt, _):
                k = keys_ref[kt]
                keys_ref[kt] = jnp.where(k == tau, (top_code - kt * PKT) - k_row, k)
                return 0

            lax.fori_loop(0, n_kt, tag_body, 0)
            row = jnp.zeros((1, qb_rows), I32)
            cut = _kth_largest_key(count_ge, row + TIE_CODE, row + top_code, need)
            drop_to = jnp.where(over_f > 0.5, KEY_NEG_INF, tau)

            def untag_body(kt, _):
                k = keys_ref[kt]
                keys_ref[kt] = jnp.where(k >= TIE_CODE, jnp.where(k >= cut, tau, drop_to), k)
                return 0

            lax.fori_loop(0, n_kt, untag_body, 0)

        m_ref[...] = jnp.full(m_ref.shape, NEG_BIG, F32)
        l_ref[...] = jnp.zeros(l_ref.shape, F32)
        acc_ref[...] = jnp.zeros(acc_ref.shape, F32)

    kb = j - 1
    last_kb = (q_lo + qb_rows - 1) // PKB

    n_sub = PKB // PSUB
    hsl = [slice(h * HEAD_DIM, (h + 1) * HEAD_DIM) for h in range(N_HEADS)]

    def logits_stage(u, b):
        for h in range(N_HEADS):
            s_ref[b, h] = _dot_nt(kb_ref[u * PSUB:(u + 1) * PSUB, hsl[h]], qs_ref[:, hsl[h]])

    def softmax_stage(u, b, near):
        s0 = kb * PKB + u * PSUB
        kt = s0 // PKT
        c0 = (u * PSUB) % PKT
        mask_bias = jnp.where(keys_ref[kt, c0:c0 + PSUB, :] >= tau_ref[...], 0.0, NEG_BIG)
        if near:
            which = jnp.clip((q_lo - s0) // PSUB, 0, 1)
        for h in range(N_HEADS):
            s = s_ref[b, h] + mask_bias
            if near:
                s = s + band_ref[h, 1 - which]
            m_old = m_ref[h]
            m_new = jnp.maximum(m_old, jnp.max(s, axis=0, keepdims=True))
            p = jnp.exp2(s - m_new)
            alpha = jnp.exp2(m_old - m_new)
            l_ref[h] = alpha * l_ref[h] + jnp.sum(p, axis=0, keepdims=True)
            m_ref[h] = m_new
            a_ref[b, h] = alpha
            p_ref[b, h] = p.astype(BF16)

    def values_stage(u, b):
        for h in range(N_HEADS):
            pv = _dot(vt_ref[hsl[h], u * PSUB:(u + 1) * PSUB], p_ref[b, h])
            acc_ref[h] = acc_ref[h] * a_ref[b, h] + pv

    attend = jnp.logical_and(j >= 1, kb <= last_kb)
    block_far = kb * PKB + PKB <= q_lo - PSUB

    @pl.when(jnp.logical_and(attend, block_far))
    def _attend_far_block():
        logits_stage(0, 0)
        for u in range(n_sub):
            if u + 1 < n_sub:
                logits_stage(u + 1, (u + 1) % 2)
            softmax_stage(u, u % 2, False)
            values_stage(u, u % 2)

    @pl.when(jnp.logical_and(attend, jnp.logical_not(block_far)))
    def _attend_near_block():
        for u in range(n_sub):
            s0 = kb * PKB + u * PSUB
            is_near = s0 + PSUB > q_lo - PSUB
            in_range = s0 < q_lo + qb_rows

            def sub_tile(near, u=u):
                logits_stage(u, 0)
                softmax_stage(u, 0, near)
                values_stage(u, 0)

            @pl.when(jnp.logical_and(in_range, is_near))
            def _():
                sub_tile(True)

            @pl.when(jnp.logical_and(in_range, jnp.logical_not(is_near)))
            def _():
                sub_tile(False)

    @pl.when(j == last_kb + 1)
    def _finish():
        for h in range(N_HEADS):
            hs = slice(h * HEAD_DIM, (h + 1) * HEAD_DIM)
            o_ref[:, hs] = (acc_ref[h] / l_ref[h]).T


def _prompt_attention(qs, qib, wi, kib, kb, vt, rel_bias):
    s, w = qs.shape
    topk = min(TOPK_MAX, s // 4)
    qb_rows = min(PQB, s)
    assert s % qb_rows == 0 and s % PKB == 0 and qb_rows == PSUB
    nqb = s // qb_rows
    nkb = s // PKB
    band = _bias_band(rel_bias, 2 * PSUB, qb_rows, -1, PSUB, LOG2E).reshape(N_HEADS, 2, PSUB, qb_rows)

    def last_kb(i):
        return (i * qb_rows + qb_rows - 1) // PKB

    qrow = lambda n: pl.BlockSpec((qb_rows, n), lambda i, j: (i, 0))
    return pl.pallas_call(
        functools.partial(_prompt_att_kernel, topk=topk, qb_rows=qb_rows),
        grid=(nqb, nkb + 1),
        in_specs=[qrow(w), qrow(w), pl.BlockSpec((N_IDX_HEADS, qb_rows), lambda i, j: (0, i)),
                  pl.BlockSpec((s, IDX_DIM), lambda i, j: (0, 0)),
                  pl.BlockSpec((PKB, w), lambda i, j: (jnp.minimum(jnp.maximum(j - 1, 0), last_kb(i)), 0)),
                  pl.BlockSpec((w, PKB), lambda i, j: (0, jnp.minimum(jnp.maximum(j - 1, 0), last_kb(i)))),
                  pl.BlockSpec(band.shape, lambda i, j: (0, 0, 0, 0))],
        out_specs=qrow(w),
        out_shape=jax.ShapeDtypeStruct((s, w), F32),
        scratch_shapes=[pltpu.VMEM((s // PKT, PKT, qb_rows), I32),
                        pltpu.VMEM((1, qb_rows), I32),
                        pltpu.VMEM((N_HEADS, 1, qb_rows), F32),
                        pltpu.VMEM((N_HEADS, 1, qb_rows), F32),
                        pltpu.VMEM((N_HEADS, HEAD_DIM, qb_rows), F32),
                        pltpu.VMEM((2, N_HEADS, PSUB, qb_rows), F32),
                        pltpu.VMEM((2, N_HEADS, PSUB, qb_rows), BF16),
                        pltpu.VMEM((2, N_HEADS, 1, qb_rows), F32)],
        compiler_params=_cparams(("arbitrary", "arbitrary")),
    )(qs, qib, wi.T, kib, kb, vt, band)


RC = 16
RNB = 16
RBLK = RC * RNB


def _dot_exact_lhs(m_bf, x):
    hi, mid, lo = _split3(x)
    return _dot(m_bf, hi) + (_dot(m_bf, mid) + _dot(m_bf, lo))


def _dot_exact_rhs(x, m_bf):
    hi, mid, lo = _split3(x)
    return _dot(hi, m_bf) + (_dot(mid, m_bf) + _dot(lo, m_bf))


def _rwkv_kernel(*refs, chain, t_valid):
    if chain:
        p_ref, = refs[:1]
        rest = refs[1:]
    else:
        p_ref, pprev_ref, s0_ref = refs[:3]
        rest = refs[3:]
    (mu_ref, w0_ref, wd2_ref, a0_ref, wa2_ref, wg2_ref, kk_ref, ka_ref, rk_ref, lng_ref, lnb_ref,
     lt_ref, bo_ref, bd_ref, out_ref, sfin_ref,
     carry_ref, s_ref, wa_s, rq_s, uv_s, yv_s, bt_s, kt_s, v_s, gc_s, y_s,
     np_s, ti_s, ak_s, rb_s, rk_s) = rest
    i = pl.program_id(0)
    w = GROUP_WIDTH
    p = p_ref[...]
    row = lax.broadcasted_iota(I32, (RBLK, 1), 0)
    if chain:
        @pl.when(i == 0)
        def _():
            carry_ref[...] = jnp.zeros(carry_ref.shape, F32)
            s_ref[...] = jnp.zeros(s_ref.shape, F32)

        pprev = jnp.where(row == 0, carry_ref[...], pltpu.roll(p, 1, axis=0))
        carry_ref[...] = p[RBLK - 1:RBLK, :]
    else:
        pprev = pprev_ref[...]
    ps = p + (pprev - p) * mu_ref[...]
    r, k, v = ps[:, :w], ps[:, w:2 * w], ps[:, 2 * w:3 * w]
    o = 3 * w
    xw = ps[:, o:o + DECAY_LORA]
    xa = ps[:, o + DECAY_LORA:o + DECAY_LORA + AAA_LORA]
    xg = ps[:, o + DECAY_LORA + AAA_LORA:]
    dec = w0_ref[...] + _dot(jnp.tanh(xw).astype(BF16), wd2_ref[...])
    softplus = jnp.maximum(-dec, 0.0) + jnp.log(1.0 + jnp.exp(-jnp.abs(dec)))
    lw = -jnp.exp(-softplus - 0.5)
    a = jax.nn.sigmoid(a0_ref[...] + _dot(xa.astype(BF16), wa2_ref[...]))
    g = _dot(jax.nn.sigmoid(xg).astype(BF16), wg2_ref[...])
    kk = k * kk_ref[...]
    kk = kk * lax.rsqrt(jnp.maximum(_dot_exact_rhs(kk * kk, bd_ref[...]), 1e-24))
    k2 = k * (1.0 + (a - 1.0) * ka_ref[...])
    alpha = -kk
    beta = kk * a
    if t_valid < RC:
        valid = (row % RC) < t_valid
        zero = lambda x: jnp.where(valid, x, 0.0)
        lw, alpha, beta, k2, r, v = zero(lw), zero(alpha), zero(beta), zero(k2), zero(r), zero(v)
    cl = _dot_exact_lhs(lt_ref[...], lw)
    ct = _dot_exact_lhs(bo_ref[...], lw)
    g_in = jnp.exp(cl)
    g_ex = jnp.exp(cl - lw)
    g_inv = jnp.exp(-cl)
    g_end = jnp.exp(ct - cl)
    g_all = jnp.exp(ct)
    at, rt = alpha * g_ex, r * g_in
    bh, kh = beta * g_inv, k2 * g_inv
    bt, kt = beta * g_end, k2 * g_end

    ri = lax.broadcasted_iota(I32, (RBLK, RBLK), 0)
    ci = lax.broadcasted_iota(I32, (RBLK, RBLK), 1)
    same = (ri // RC) == (ci // RC)
    strict = jnp.logical_and(same, ci < ri)
    incl = jnp.logical_and(same, ci <= ri)
    eye = (ri == ci).astype(F32)

    heads = range(N_HEADS)
    hsl = [slice(h * HEAD_DIM, (h + 1) * HEAD_DIM) for h in heads]
    for h in heads:
        hs = hsl[h]
        gm = _dot_nt(jnp.concatenate([at[:, hs], rt[:, hs]], axis=0).astype(BF16),
                     jnp.concatenate([bh[:, hs], kh[:, hs]], axis=0).astype(BF16))
        n1 = jnp.where(strict, gm[:RBLK, :RBLK], 0.0)
        np_s[h] = n1
        ti_s[h] = eye + n1
        ak_s[h] = jnp.where(strict, gm[:RBLK, RBLK:], 0.0)
        rb_s[h] = jnp.where(incl, gm[RBLK:, :RBLK], 0.0)
        rk_s[h] = jnp.where(incl, gm[RBLK:, RBLK:], 0.0)
        bt_s[h] = bt[:, hs]
        kt_s[h] = kt[:, hs]
        v_s[h] = v[:, hs]
        gc_s[h] = g_all[:, hs]
    for _ in range(int(math.log2(RC)) - 1):
        for h in heads:
            npow = np_s[h].astype(BF16)
            np_s[h] = _dot(npow, npow)
        for h in heads:
            tinv = ti_s[h]
            ti_s[h] = tinv + _dot(np_s[h].astype(BF16), tinv.astype(BF16))
    for h in heads:
        uv_s[h] = _dot(ak_s[h].astype(BF16), v_s[h].astype(BF16))
    for h in heads:
        wcat = _dot(ti_s[h].astype(BF16), jnp.concatenate([at[:, hsl[h]], uv_s[h]], axis=1).astype(BF16))
        wa_s[h] = wcat[:, :HEAD_DIM]
        uv_s[h] = wcat[:, HEAD_DIM:]
    for h in heads:
        ry =_dot(rb_s[h].astype(BF16), jnp.concatenate([wa_s[h], uv_s[h]], axis=1).astype(BF16))
        rq_s[h] = rt[:, hsl[h]] + ry[:, :HEAD_DIM]
        yv_s[h] = ry[:, HEAD_DIM:] + _dot(rk_s[h].astype(BF16), v_s[h].astype(BF16))

    def chunk_body(c, _):
        c0 = pl.multiple_of(c * RC, RC)
        sl = pl.ds(c0, RC)
        s_old = [s0_ref[c, h] if not chain else s_ref[h] for h in heads]
        res = [_dot_nt(jnp.concatenate([wa_s[h, sl, :], rq_s[h, sl, :]], axis=0).astype(BF16),
                       s_old[h].astype(BF16)) for h in heads]
        for h in heads:
            y_s[h, sl, :] = res[h][RC:] + yv_s[h, sl, :]
        upd = [lax.dot_general(jnp.concatenate([res[h][:RC] + uv_s[h, sl, :], v_s[h, sl, :]], axis=0).astype(BF16),
                               jnp.concatenate([bt_s[h, sl, :], kt_s[h, sl, :]], axis=0).astype(BF16),
                               (((0,), (0,)), ((), ())), preferred_element_type=F32) for h in heads]
        for h in heads:
            s_new = s_old[h] * gc_s[h, pl.ds(c0, 1), :] + upd[h]
            if chain:
                s_ref[h] = s_new
            else:
                sfin_ref[c, h] = s_new
        return 0

    lax.fori_loop(0, RNB, chunk_body, 0)
    if chain:
        sfin_ref[...] = s_ref[...]

    for h in range(N_HEADS):
        hs = slice(h * HEAD_DIM, (h + 1) * HEAD_DIM)
        y = y_s[h]
        mean = jnp.mean(y, axis=-1, keepdims=True)
        var = jnp.mean(jnp.square(y - mean), axis=-1, keepdims=True)
        yn = (y - mean) * lax.rsqrt(var + GN_EPS) * lng_ref[:, hs] + lnb_ref[:, hs]
        bonus = jnp.sum(r[:, hs] * k2[:, hs] * rk_ref[:, hs], axis=-1, keepdims=True) * v[:, hs]
        out_ref[:, hs] = (yn + bonus) * g[:, hs]


def _rwkv_consts():
    idx = np.arange(RBLK)
    same = (idx[:, None] // RC) == (idx[None, :] // RC)
    lt = (same & (idx[None, :] <= idx[:, None])).astype(np.float32)
    bo = same.astype(np.float32)
    lane = np.arange(GROUP_WIDTH)
    bd = ((lane[:, None] // HEAD_DIM) == (lane[None, :] // HEAD_DIM)).astype(np.float32)
    return jnp.asarray(lt, BF16), jnp.asarray(bo, BF16), jnp.asarray(bd, BF16)


def _rwkv(p, lw, chain, pprev=None, s0=None, t_valid=RC):
    rows = p.shape[0]
    nblk = rows // RBLK
    w = GROUP_WIDTH
    vec = lambda a: a.reshape(1, -1)
    consts = [vec(lw['mu_shift']), vec(lw['w0']), lw['w_decay2'].astype(BF16), vec(lw['a0']),
              lw['w_a2'].astype(BF16), lw['w_g2'].astype(BF16), vec(lw['k_k']), vec(lw['k_a']), vec(lw['r_k']),
              vec(lw['lnx_g']), vec(lw['lnx_b'])] + list(_rwkv_consts())
    full = lambda a: pl.BlockSpec(a.shape, lambda i: (0,) * a.ndim)
    blk = pl.BlockSpec((RBLK, RWKV_COLS), lambda i: (i, 0))
    hshape = (N_HEADS, HEAD_DIM, HEAD_DIM)
    if chain:
        ins, in_specs = [p], [blk]
        sfin_spec = pl.BlockSpec(hshape, lambda i: (0, 0, 0))
        sfin_shape = jax.ShapeDtypeStruct(hshape, F32)
    else:
        sspec = pl.BlockSpec((RNB,) + hshape, lambda i: (i, 0, 0, 0))
        ins, in_specs = [p, pprev, s0], [blk, blk, sspec]
        sfin_spec = sspec
        sfin_shape = jax.ShapeDtypeStruct((nblk * RNB,) + hshape, F32)
    hm = lambda: pltpu.VMEM((N_HEADS, RBLK, HEAD_DIM), F32)
    return pl.pallas_call(
        functools.partial(_rwkv_kernel, chain=chain, t_valid=t_valid),
        grid=(nblk,),
        in_specs=in_specs + [full(c) for c in consts],
        out_specs=[pl.BlockSpec((RBLK, w), lambda i: (i, 0)), sfin_spec],
        out_shape=[jax.ShapeDtypeStruct((rows, w), F32), sfin_shape],
        scratch_shapes=[pltpu.VMEM((1, RWKV_COLS), F32), pltpu.VMEM(hshape, F32)] + [hm() for _ in range(9)]
        + [pltpu.VMEM((N_HEADS, RBLK, RBLK), F32) for _ in range(5)],
        compiler_params=_cparams(("arbitrary",)),
    )(*ins, *consts)


RT_TM = 256


def _route_kernel(x_ref, ro_ref, ao_ref, wor_ref, woa_ref, g1_ref, sh_ref, sc_ref, n2_ref, wr_ref, br_ref,
                  ltri_ref, x1_ref, h2_ref, idx_ref, gate_ref, rank_ref, cnt_ref, run_ref):
    @pl.when(pl.program_id(0) == 0)
    def _():
        run_ref[...] = jnp.zeros(run_ref.shape, F32)

    tm = x_ref.shape[0]
    mix = _dot(ro_ref[...].astype(BF16), wor_ref[...]) + _dot(ao_ref[...].astype(BF16), woa_ref[...])
    x1 = x_ref[...] + g1_ref[...] * mix
    x1_ref[...] = x1
    y = x1 * lax.rsqrt(jnp.mean(x1 * x1, axis=-1, keepdims=True) + RMS_EPS) * n2_ref[...]
    h2 = (y * (1.0 + sc_ref[...]) + sh_ref[...]).astype(BF16)
    h2_ref[...] = h2
    logits = _dot(h2, wr_ref[...]) + br_ref[...]
    lane = lax.broadcasted_iota(I32, (tm, LANES), 1)
    lane_f = lane.astype(F32)
    lg = logits
    vals, idxs = [], []
    for _ in range(TOP_K):
        m = jnp.max(lg, axis=1, keepdims=True)
        idx = jnp.min(jnp.where(lg == m, lane_f, float(LANES)), axis=1, keepdims=True)
        vals.append(m)
        idxs.append(idx)
        lg = jnp.where(lane_f == idx, -3e38, lg)
    es = [jnp.exp(v - vals[0]) for v in vals]
    den = es[0] + es[1] + es[2] + es[3]
    ohs = [(lane_f == idx).astype(F32) for idx in idxs]
    oh_all = ohs[0] + ohs[1] + ohs[2] + ohs[3]
    base = run_ref[...] + _dot(ltri_ref[...], oh_all.astype(BF16))
    idx_out = jnp.zeros((tm, LANES), F32)
    gate_out = jnp.zeros((tm, LANES), F32)
    rank_out = jnp.zeros((tm, LANES), F32)
    for kk in range(TOP_K):
        rank = jnp.sum(ohs[kk] * base, axis=1, keepdims=True)
        idx_out = jnp.where(lane == kk, idxs[kk], idx_out)
        gate_out = jnp.where(lane == kk, es[kk] / den, gate_out)
        rank_out = jnp.where(lane == kk, rank, rank_out)
    idx_ref[...] = idx_out[:, :TOP_K].astype(I32)
    gate_ref[...] = gate_out[:, :TOP_K]
    rank_ref[...] = rank_out[:, :TOP_K].astype(I32)
    run_ref[...] = run_ref[...] + jnp.sum(oh_all, axis=0, keepdims=True)
    cnt_ref[...] = run_ref[...]


def _out_proj_route(x, ro, ao, w_out, mod, norm2_g, w_router, b_router):
    t, d = x.shape
    tm = min(RT_TM, t)
    w = GROUP_WIDTH
    wor = w_out[:w].astype(BF16)
    woa = w_out[w:].astype(BF16)
    wr = jnp.pad(w_router, ((0, 0), (0, LANES - N_EXPERTS))).astype(BF16)
    br = jnp.pad(b_router.reshape(1, -1), ((0, 0), (0, LANES - N_EXPERTS)), constant_values=NEG_BIG)
    ltri = jnp.asarray(np.tril(np.ones((tm, tm), np.float32), -1), BF16)
    row = lambda n: pl.BlockSpec((tm, n), lambda i: (i, 0))
    full = lambda a: pl.BlockSpec(a.shape, lambda i: (0, 0))
    return pl.pallas_call(
        _route_kernel,
        grid=(t // tm,),
        in_specs=[row(d), row(w), row(w), full(wor), full(woa),
                  _mod_spec(mod, 2, tm, d), _mod_spec(mod, 3, tm, d), _mod_spec(mod, 4, tm, d),
                  pl.BlockSpec((1, d), lambda i: (0, 0)), full(wr), full(br), full(ltri)],
        out_specs=[row(d), row(d), row(TOP_K), row(TOP_K), row(TOP_K), pl.BlockSpec((1, LANES), lambda i: (0, 0))],
        out_shape=[jax.ShapeDtypeStruct((t, d), F32), jax.ShapeDtypeStruct((t, d), BF16),
                   jax.ShapeDtypeStruct((t, TOP_K), I32), jax.ShapeDtypeStruct((t, TOP_K), F32),
                   jax.ShapeDtypeStruct((t, TOP_K), I32), jax.ShapeDtypeStruct((1, LANES), F32)],
        scratch_shapes=[pltpu.VMEM((1, LANES), F32)],
        compiler_params=_cparams(("arbitrary",)),
    )(x, ro, ao, wor, woa, mod, mod, mod, norm2_g.reshape(1, d), wr, br, ltri)


EX_TM = 256
BF16_ROW = (SUBLANES, LANES)


def _dispatch_kernel(slot_ref, h_ref, init_ref, out_ref, sem):
    del init_ref
    tm = h_ref.shape[0]

    def body(r, _):
        for kk in range(TOP_K):
            pltpu.make_async_copy(h_ref.at[r], out_ref.at[slot_ref[r * TOP_K + kk]], sem).start()
        return 0

    lax.fori_loop(0, tm, body, 0)
    for _ in range(TOP_K):
        pltpu.make_async_copy(h_ref, out_ref.at[pl.ds(0, tm)], sem).wait()


def _dispatch(h2, slot, rows_sorted):
    t, d = h2.shape
    tm = min(RT_TM, t)
    assert d == SUBLANES * LANES
    h3 = h2.reshape((t,) + BF16_ROW)
    return pl.pallas_call(
        _dispatch_kernel,
        grid=(t // tm,),
        in_specs=[pl.BlockSpec((tm * TOP_K,), lambda i: (i,), memory_space=pltpu.SMEM),
                  pl.BlockSpec((tm,) + BF16_ROW, lambda i: (i, 0, 0)),
                  pl.BlockSpec(memory_space=pl.ANY)],
        out_specs=pl.BlockSpec(memory_space=pl.ANY),
        out_shape=jax.ShapeDtypeStruct(rows_sorted.shape, rows_sorted.dtype),
        scratch_shapes=[pltpu.SemaphoreType.DMA(())],
        input_output_aliases={2: 0},
        compiler_params=_cparams(("arbitrary",)),
    )(slot.reshape(-1), h3, rows_sorted)


def _expert_kernel(be_ref, nv_ref, x_ref, wgu_ref, bgu_ref, wd_ref, bd_ref, o_ref, wgu_bf, wd_bf):
    i = pl.program_id(0)
    changed = jnp.logical_or(i == 0, be_ref[i] != be_ref[jnp.maximum(i - 1, 0)])

    @pl.when(changed)
    def _():
        wgu_bf[...] = wgu_ref[0].astype(BF16)
        wd_bf[...] = wd_ref[0].astype(BF16)

    @pl.when(i < nv_ref[0])
    def _():
        f = wd_bf.shape[0]
        gu = _dot(x_ref[...], wgu_bf[...]) + bgu_ref[0]
        glu = jnp.minimum(gu[:, :f], SWIGLU_LIMIT)
        lin = jnp.clip(gu[:, f:], -SWIGLU_LIMIT, SWIGLU_LIMIT)
        act = glu * jax.nn.sigmoid(SWIGLU_ALPHA * glu) * (lin + 1.0)
        o_ref[...] = _dot(act.astype(BF16), wd_bf[...]) + bd_ref[0]

    @pl.when(i >= nv_ref[0])
    def _():
        o_ref[...] = jnp.zeros(o_ref.shape, F32)


def _experts(rows_sorted, blk_e, n_valid, w_gu, b_gu, w_down, b_down):
    ns, d = rows_sorted.shape
    e, _, f2 = w_gu.shape
    f = f2 // 2
    grid_spec = pltpu.PrefetchScalarGridSpec(
        num_scalar_prefetch=2,
        grid=(ns // EX_TM,),
        in_specs=[pl.BlockSpec((EX_TM, d), lambda i, be, nv: (i, 0)),
                  pl.BlockSpec((1, d, f2), lambda i, be, nv: (be[i], 0, 0)),
                  pl.BlockSpec((1, 1, f2), lambda i, be, nv: (be[i], 0, 0)),
                  pl.BlockSpec((1, f, d), lambda i, be, nv: (be[i], 0, 0)),
                  pl.BlockSpec((1, 1, d), lambda i, be, nv: (be[i], 0, 0))],
        out_specs=pl.BlockSpec((EX_TM, d), lambda i, be, nv: (i, 0)),
        scratch_shapes=[pltpu.VMEM((d, f2), BF16), pltpu.VMEM((f, d), BF16)])
    return pl.pallas_call(
        _expert_kernel,
        grid_spec=grid_spec,
        out_shape=jax.ShapeDtypeStruct((ns, d), F32),
        compiler_params=_cparams(("arbitrary",)),
    )(blk_e, n_valid, rows_sorted, w_gu, b_gu.reshape(e, 1, f2), w_down, b_down.reshape(e, 1, d))


def _combine_kernel(slot_ref, rows_ref, gate_ref, x1_ref, g2_ref, nf_ref, y_ref, buf, sem):
    tm = x1_ref.shape[0]

    def body(r, _):
        for kk in range(TOP_K):
            pltpu.make_async_copy(rows_ref.at[pl.ds(slot_ref[r * TOP_K + kk], 1)],
                                  buf.at[kk, pl.ds(r, 1)], sem).start()
        return 0

    lax.fori_loop(0, tm, body, 0)
    for kk in range(TOP_K):
        pltpu.make_async_copy(rows_ref.at[pl.ds(0, tm)], buf.at[kk], sem).wait()
    gates = gate_ref[...]
    moe = gates[:, 0:1] * buf[0]
    for kk in range(1, TOP_K):
        moe = moe + gates[:, kk:kk + 1] * buf[kk]
    x2 = x1_ref[...] + g2_ref[...] * moe
    y_ref[...] = x2 * lax.rsqrt(jnp.mean(x2 * x2, axis=-1, keepdims=True) + RMS_EPS) * nf_ref[...]


def _combine(out_rows, slot, gates, x1, mod, normf_g):
    t, d = x1.shape
    tm = min(RT_TM, t)
    return pl.pallas_call(
        _combine_kernel,
        grid=(t // tm,),
        in_specs=[pl.BlockSpec((tm * TOP_K,), lambda i: (i,), memory_space=pltpu.SMEM),
                  pl.BlockSpec(memory_space=pl.ANY),
                  pl.BlockSpec((tm, TOP_K), lambda i: (i, 0)),
                  pl.BlockSpec((tm, d), lambda i: (i, 0)), _mod_spec(mod, 5, tm, d),
                  pl.BlockSpec((1, d), lambda i: (0, 0))],
        out_specs=pl.BlockSpec((tm, d), lambda i: (i, 0)),
        out_shape=jax.ShapeDtypeStruct((t, d), F32),
        scratch_shapes=[pltpu.VMEM((TOP_K, tm, d), F32), pltpu.SemaphoreType.DMA(())],
        compiler_params=_cparams(("arbitrary",)),
    )(slot.reshape(-1), out_rows, gates, x1, mod, normf_g.reshape(1, d))


def _combine_pipe_kernel(slot_ref, slot_next_ref, rows_ref, gate_ref, x1_ref, g2_ref, nf_ref, y_ref, buf, sem):
    i = pl.program_id(0)
    n = pl.num_programs(0)
    tm = x1_ref.shape[0]

    def start_gather(s_ref, b):
        def body(r, _):
            for kk in range(TOP_K):
                pltpu.make_async_copy(rows_ref.at[pl.ds(s_ref[r * TOP_K + kk], 1)],
                                      buf.at[b, kk, pl.ds(r, 1)], sem.at[b]).start()
            return 0

        lax.fori_loop(0, tm, body, 0)

    @pl.when(i == 0)
    def _():
        start_gather(slot_ref, 0)

    @pl.when(i + 1 < n)
    def _():
        start_gather(slot_next_ref, (i + 1) % 2)

    b = i % 2
    for kk in range(TOP_K):
        pltpu.make_async_copy(rows_ref.at[pl.ds(0, tm)], buf.at[b, kk], sem.at[b]).wait()
    gates = gate_ref[...]
    moe = gates[:, 0:1] * buf[b, 0]
    for kk in range(1, TOP_K):
        moe = moe + gates[:, kk:kk + 1] * buf[b, kk]
    x2 = x1_ref[...] + g2_ref[...] * moe
    y_ref[...] = x2 * lax.rsqrt(jnp.mean(x2 * x2, axis=-1, keepdims=True) + RMS_EPS) * nf_ref[...]


def _combine_pipe(out_rows, slot, gates, x1, mod, normf_g):
    t, d = x1.shape
    tm = min(RT_TM, t)
    n = t // tm
    flat = slot.reshape(-1)
    return pl.pallas_call(
        _combine_pipe_kernel,
        grid=(n,),
        in_specs=[pl.BlockSpec((tm * TOP_K,), lambda i: (i,), memory_space=pltpu.SMEM),
                  pl.BlockSpec((tm * TOP_K,), lambda i: (jnp.minimum(i + 1, n - 1),), memory_space=pltpu.SMEM),
                  pl.BlockSpec(memory_space=pl.ANY),
                  pl.BlockSpec((tm, TOP_K), lambda i: (i, 0)),
                  pl.BlockSpec((tm, d), lambda i: (i, 0)), _mod_spec(mod, 5, tm, d),
                  pl.BlockSpec((1, d), lambda i: (0, 0))],
        out_specs=pl.BlockSpec((tm, d), lambda i: (i, 0)),
        out_shape=jax.ShapeDtypeStruct((t, d), F32),
        scratch_shapes=[pltpu.VMEM((2, TOP_K, tm, d), F32), pltpu.SemaphoreType.DMA((2,))],
        compiler_params=_cparams(("arbitrary",)),
    )(flat, flat, out_rows, gates, x1, mod, normf_g.reshape(1, d))


def _moe_plan(idx_p, rank_p, cnt_p, idx_s, rank_s, cnt_s):
    n_assign = idx_p.size + idx_s.size
    n_tiles = -(-n_assign // EX_TM) + N_EXPERTS
    cp = cnt_p[0, :N_EXPERTS].astype(I32)
    cs = cnt_s[0, :N_EXPERTS].astype(I32)
    padded = (cp + cs + EX_TM - 1) // EX_TM * EX_TM
    pad_end = jnp.cumsum(padded)
    pad_start = pad_end - padded
    slot_p = pad_start[idx_p] + rank_p
    slot_s = pad_start[idx_s] + cp[idx_s] + rank_s
    tile_row = jnp.arange(n_tiles, dtype=I32)[:, None] * EX_TM
    blk_e = jnp.minimum(jnp.sum((pad_end[None, :] <= tile_row).astype(I32), axis=1), N_EXPERTS - 1)
    n_valid = (pad_end[-1:] // EX_TM).astype(I32)
    return slot_p, slot_s, blk_e, n_valid, n_tiles * EX_TM


PG = 16
QROWS = N_HEADS * SUBLANES


def _page_specs(block, n_pages):
    def spec(u):
        return pl.BlockSpec(block, lambda b, j, pt: (pt[b * n_pages + j * PG + u],) + (0,) * (len(block) - 1))
    return [spec(u) for u in range(PG)]


def _head_sum(x):
    out = x[:SUBLANES]
    for h in range(1, N_HEADS):
        out = out + x[h * SUBLANES:(h + 1) * SUBLANES]
    return out


def _sample_index_kernel(pt_ref, qi_ref, wrep_ref, kin_ref, *rest, n_pages, t_new, topk):
    del pt_ref
    pages = rest[:PG]
    keys_ref, tau_ref = rest[PG:]
    j = pl.program_id(1)
    qi = qi_ref[0]
    wrep = wrep_ref[0]

    def scores(ki_t_bf):
        s = _dot(qi, ki_t_bf)
        return _head_sum(jnp.maximum(s, 0.0) * wrep)

    for u in range(PG):
        keys_ref[0, j * PG + u] = _float_key(scores(pages[u][0].astype(BF16)))

    @pl.when(j == pl.num_programs(1) - 1)
    def _():
        qrow = lax.broadcasted_iota(I32, (SUBLANES, PAGE_SIZE), 0)
        col = lax.broadcasted_iota(I32, (SUBLANES, PAGE_SIZE), 1)
        ok = jnp.logical_and(col <= qrow, col < t_new)
        keys_ref[0, n_pages] = jnp.where(ok, _float_key(scores(kin_ref[0])), KEY_NEG_INF)

        def count_ge(mid):
            c = jnp.sum(jnp.where(keys_ref[0] >= mid[None], 1.0, 0.0), axis=0)
            return jnp.broadcast_to(jnp.sum(c, axis=1, keepdims=True), (SUBLANES, LANES))

        tau = _kth_largest_key(count_ge, jnp.full((SUBLANES, LANES), KEY_NEG_INF + 1, I32),
                               jnp.full((SUBLANES, LANES), 0x7F800000, I32), topk)
        tau_ref[0] = tau
        over_f = jnp.where(count_ge(tau) > topk, 1.0, 0.0)

        @pl.when(jnp.max(over_f) > 0.0)
        def _break_ties():
            need = topk - count_ge(tau + 1)
            shape = keys_ref.shape[1:]
            pos = lax.broadcasted_iota(I32, shape, 0) * PAGE_SIZE + lax.broadcasted_iota(I32, shape, 2)
            tied = keys_ref[0] == tau[None]

            def pbody(_, carry):
                lo, hi = carry
                mid = (lo + hi) >> 1
                c = jnp.sum(jnp.where(tied, jnp.where(pos <= mid[None], 1.0, 0.0), 0.0), axis=0)
                ok = jnp.broadcast_to(jnp.sum(c, axis=1, keepdims=True), (SUBLANES, LANES)) >= need
                return jnp.where(ok, lo, mid + 1), jnp.where(ok, mid, hi)

            n_pos = (n_pages + 1) * PAGE_SIZE
            last_pos, _ = lax.fori_loop(0, (n_pos - 1).bit_length(), pbody,
                                        (jnp.zeros((SUBLANES, LANES), I32), jnp.full((SUBLANES, LANES), n_pos - 1, I32)))
            gone = jnp.where(tied, jnp.where(pos > last_pos[None], over_f[None], 0.0), 0.0)
            keys_ref[0] = jnp.where(gone > 0.5, KEY_NEG_INF, keys_ref[0])


def _sample_attend_kernel(pt_ref, q_ref, keys_ref, tau_ref, band_ref, kn_ref, vn_ref, *rest, n_pages):
    del pt_ref
    kpages = rest[:PG]
    vpages = rest[PG:2 * PG]
    o_ref, m_ref, l_ref, acc_ref = rest[2 * PG:]
    j = pl.program_id(1)
    last = j == pl.num_programs(1) - 1

    @pl.when(j == 0)
    def _():
        m_ref[...] = jnp.full(m_ref.shape, NEG_BIG, F32)
        l_ref[...] = jnp.zeros(l_ref.shape, F32)
        acc_ref[...] = jnp.zeros(acc_ref.shape, F32)

    tau = tau_ref[0]
    heads = range(N_HEADS)

    def attend(k_of, v_of, key_tiles, bias_of):
        pages = range(len(key_tiles))
        sel = [kt >= tau for kt in key_tiles]
        s = [[_dot(q_ref[0, h], k_of(u, h)) for u in pages] for h in heads]
        ps, alphas = [], []
        for h in heads:
            sh = []
            for u in pages:
                b_uh = bias_of(u, h)
                sh.append(jnp.where(sel[u], s[h][u] if b_uh is None else s[h][u] + b_uh, NEG_BIG))
            tile_max = functools.reduce(jnp.maximum, sh)
            m_old = m_ref[h]
            m_new = jnp.maximum(m_old, jnp.broadcast_to(jnp.max(tile_max, axis=1, keepdims=True), m_old.shape))
            p = [jnp.exp(x - m_new) for x in sh]
            alpha = jnp.exp(m_old - m_new)
            p_sum = functools.reduce(jnp.add, p)
            l_ref[h] = alpha * l_ref[h] + jnp.broadcast_to(jnp.sum(p_sum, axis=1, keepdims=True), m_old.shape)
            m_ref[h] = m_new
            ps.append([x.astype(BF16) for x in p])
            alphas.append(alpha[:, :HEAD_DIM])
        pv = [[_dot_nt(ps[h][u], v_of(u, h)) for u in pages] for h in heads]
        for h in heads:
            acc_ref[h] = acc_ref[h] * alphas[h] + functools.reduce(jnp.add, pv[h])

    attend(lambda u, h: kpages[u][0, h].astype(BF16), lambda u, h: vpages[u][0, h].astype(BF16),
           [keys_ref[0, j * PG + u] for u in range(PG)],
           lambda u, h: jnp.where(last, band_ref[h, :, :PAGE_SIZE], 0.0) if u == PG - 1 else None)

    @pl.when(last)
    def _():
        hsl = lambda h: slice(h * HEAD_DIM, (h + 1) * HEAD_DIM)
        attend(lambda u, h: kn_ref[0, hsl(h), :], lambda u, h: vn_ref[0, hsl(h), :], [keys_ref[0, n_pages]],
               lambda u, h: band_ref[h, :, PAGE_SIZE:])
        for h in heads:
            o_ref[0, :, hsl(h)] = acc_ref[h] / l_ref[h][:, :HEAD_DIM]


def _sample_attention(qs, qib, wi, k_new_bf, v_new_bf, ki_new_bf, cache_k, cache_v, cache_kidx, page_table,
                      rel_bias, b, t_new):
    w = GROUP_WIDTH
    n_pages = page_table.shape[1]
    assert n_pages % PG == 0 and t_new <= SUBLANES and cache_k.shape[1] == PAGE_SIZE
    past = n_pages * PAGE_SIZE
    topk = min(TOPK_MAX, (past + t_new) // 4)
    pt = page_table.reshape(-1)
    padq = lambda a: jnp.pad(a, ((0, 0), (0, SUBLANES - t_new)) + ((0, 0),) * (a.ndim - 2))
    padk = lambda a: jnp.pad(a.reshape(b, t_new, -1), ((0, 0), (0, PAGE_SIZE - t_new), (0, 0))).transpose(0, 2, 1)
    qi_r = padq(qib.reshape(b, t_new, N_IDX_HEADS, IDX_DIM)).transpose(0, 2, 1, 3).reshape(b, QROWS, IDX_DIM)
    w_r = padq(wi.reshape(b, t_new, N_IDX_HEADS)).transpose(0, 2, 1).reshape(b, QROWS, 1)
    w_r = jnp.broadcast_to(w_r, (b, QROWS, LANES))
    q4 = padq(qs.reshape(b, t_new, N_HEADS, HEAD_DIM)).transpose(0, 2, 1, 3)
    kin, kn, vn = padk(ki_new_bf), padk(k_new_bf), padk(v_new_bf)
    band = _bias_band(rel_bias, SUBLANES, 2 * PAGE_SIZE, 1, PAGE_SIZE)
    ck_t = cache_k.transpose(0, 2, 3, 1)
    cv_t = cache_v.transpose(0, 2, 3, 1)
    cki_t = cache_kidx.transpose(0, 2, 1)
    page_block = (1, N_HEADS, HEAD_DIM, PAGE_SIZE)

    per_b = lambda shape: pl.BlockSpec((1,) + shape, lambda bb, j, p_: (bb,) + (0,) * len(shape))
    steps = n_pages // PG
    keys, tau = pl.pallas_call(
        functools.partial(_sample_index_kernel, n_pages=n_pages, t_new=t_new, topk=topk),
        grid_spec=pltpu.PrefetchScalarGridSpec(
            num_scalar_prefetch=1, grid=(b, steps),
            in_specs=[per_b((QROWS, IDX_DIM)), per_b((QROWS, LANES)), per_b((IDX_DIM, PAGE_SIZE))]
            + _page_specs((1, IDX_DIM, PAGE_SIZE), n_pages),
            out_specs=[per_b((n_pages + 1, SUBLANES, PAGE_SIZE)), per_b((SUBLANES, LANES))]),
        out_shape=[jax.ShapeDtypeStruct((b, n_pages + 1, SUBLANES, PAGE_SIZE), I32),
                   jax.ShapeDtypeStruct((b, SUBLANES, LANES), I32)],
        compiler_params=_cparams(("arbitrary", "arbitrary")),
    )(pt, qi_r, w_r, kin, *([cki_t] * PG))
    out = pl.pallas_call(
        functools.partial(_sample_attend_kernel, n_pages=n_pages),
        grid_spec=pltpu.PrefetchScalarGridSpec(
            num_scalar_prefetch=1, grid=(b, steps),
            in_specs=[per_b((N_HEADS, SUBLANES, HEAD_DIM)), per_b((n_pages + 1, SUBLANES, PAGE_SIZE)),
                      per_b((SUBLANES, LANES)), pl.BlockSpec(band.shape, lambda bb, j, p_: (0, 0, 0)),
                      per_b((w, PAGE_SIZE)), per_b((w, PAGE_SIZE))]
            + _page_specs(page_block, n_pages) + _page_specs(page_block, n_pages),
            out_specs=per_b((SUBLANES, w)),
            scratch_shapes=[pltpu.VMEM((N_HEADS, SUBLANES, LANES), F32), pltpu.VMEM((N_HEADS, SUBLANES, LANES), F32),
                            pltpu.VMEM((N_HEADS, SUBLANES, HEAD_DIM), F32)]),
        out_shape=jax.ShapeDtypeStruct((b, SUBLANES, w), F32),
        compiler_params=_cparams(("arbitrary", "arbitrary")),
    )(pt, q4, keys, tau, band, kn, vn, *([ck_t] * PG), *([cv_t] * PG))
    return out[:, :t_new].reshape(b * t_new, w)


def kernel(x_prompt, x_sample, c_prompt, c_sample, cache_k, cache_v, cache_kidx, page_table, state_wkv,
           state_shift, w_ada, b_ada, norm1_g, w_in, mu_shift, w0, w_decay2, a0, w_a2, w_g2, k_k, k_a, r_k,
           lnx_g, lnx_b, rel_bias, w_out, norm2_g, w_router, b_router, w_gu, b_gu, w_down, b_down, normf_g):
    depth = w_in.shape[0]
    assert depth == 1, "the merged prompt+sample expert pass is written for a single layer"
    bp, sp, d = x_prompt.shape
    bs, ts, _ = x_sample.shape
    assert bp == 1 and sp % RBLK == 0 and bs % RNB == 0 and ts <= RC
    l = 0
    lw = {'mu_shift': mu_shift[l], 'w0': w0[l], 'w_decay2': w_decay2[l], 'a0': a0[l], 'w_a2': w_a2[l],
          'w_g2': w_g2[l], 'k_k': k_k[l], 'k_a': k_a[l], 'r_k': r_k[l], 'lnx_g': lnx_g[l], 'lnx_b': lnx_b[l]}
    ns = bs * ts

    c_all = jnp.concatenate([c_prompt, c_sample], axis=0)
    n_c = c_all.shape[0]
    c_all = jnp.pad(c_all, ((0, -n_c % SUBLANES), (0, 0)))
    mod = _adaln(c_all, w_ada[l], b_ada[l])
    mod_p = mod[0:1]
    mod_s = jnp.repeat(mod[bp:bp + bs], ts, axis=0)

    xp = x_prompt.reshape(sp, d)
    xs = x_sample.reshape(ns, d)
    pr_p, k_p, v_p, ki_p, wi_p, qs_p, kb_p, _, qib_p, kib_p, vt_p = _in_proj(xp, norm1_g[l], mod_p, w_in[l],
                                                                             HEAD_DIM ** -0.5 * LOG2E)
    pr_s, k_s, v_s, ki_s, wi_s, qs_s, kb_s, vb_s, qib_s, kib_s, _ = _in_proj(xs, norm1_g[l], mod_s, w_in[l],
                                                                             HEAD_DIM ** -0.5)

    ro_p, wkv_p = _rwkv(pr_p, lw, chain=True)
    att_p = _prompt_attention(qs_p, qib_p, wi_p, kib_p, kb_p, vt_p, rel_bias)

    pr_s3 = pr_s.reshape(bs, ts, RWKV_COLS)
    prev_s3 = jnp.concatenate([state_shift[l][:, None, :], pr_s3[:, :-1]], axis=1)
    padc = lambda a: jnp.pad(a, ((0, 0), (0, RC - ts), (0, 0))).reshape(bs * RC, RWKV_COLS)
    ro_s, wkv_s = _rwkv(padc(pr_s3), lw, chain=False, pprev=padc(prev_s3), s0=state_wkv[l], t_valid=ts)
    ro_s = ro_s.reshape(bs, RC, GROUP_WIDTH)[:, :ts].reshape(ns, GROUP_WIDTH)
    att_s = _sample_attention(qs_s, qib_s, wi_s, kb_s, vb_s, kib_s, cache_k[l], cache_v[l], cache_kidx[l],
                              page_table, rel_bias, bs, ts)

    x1_p, h2_p, idx_p, gate_p, rank_p, cnt_p = _out_proj_route(xp, ro_p, att_p, w_out[l], mod_p, norm2_g[l],
                                                               w_router[l], b_router[l])
    x1_s, h2_s, idx_s, gate_s, rank_s, cnt_s = _out_proj_route(xs, ro_s, att_s, w_out[l], mod_s, norm2_g[l],
                                                               w_router[l], b_router[l])
    slot_p, slot_s, blk_e, n_valid, n_slots = _moe_plan(idx_p, rank_p, cnt_p, idx_s, rank_s, cnt_s)
    rows = jnp.zeros((n_slots,) + BF16_ROW, BF16)
    rows = _dispatch(h2_p, slot_p, rows)
    rows = _dispatch(h2_s, slot_s, rows)
    out_rows = _experts(rows.reshape(n_slots, d), blk_e, n_valid, w_gu[l], b_gu[l], w_down[l], b_down[l])
    y_p = _combine_pipe(out_rows, slot_p, gate_p, x1_p, mod_p, normf_g)
    y_s = _combine(out_rows, slot_s, gate_s, x1_s, mod_s, normf_g)

    hd = (N_HEADS, HEAD_DIM)
    return (y_p.reshape(bp, sp, d), y_s.reshape(bs, ts, d),
            k_p.reshape((1, bp, sp) + hd), v_p.reshape((1, bp, sp) + hd), ki_p.reshape(1, bp, sp, IDX_DIM),
            wkv_p.reshape((1, bp) + (N_HEADS, HEAD_DIM, HEAD_DIM)), pr_p[sp - 1:sp].reshape(1, bp, RWKV_COLS),
            k_s.reshape((1, bs, ts) + hd), v_s.reshape((1, bs, ts) + hd), ki_s.reshape(1, bs, ts, IDX_DIM),
            wkv_s.reshape((1, bs) + (N_HEADS, HEAD_DIM, HEAD_DIM)), pr_s3[:, ts - 1].reshape(1, bs, RWKV_COLS))
```
